```python
import math
import jax, jax.numpy as jnp
from jax import lax
import numpy as np

D_MODEL = 2048
BATCH = 16
SEQ = 2048
DEPTH = 2

HEAD_DIM = 128
N_HEADS = D_MODEL // HEAD_DIM
D_MIX = N_HEADS * HEAD_DIM
FOX_HEADS = N_HEADS // 2
NSA_HEADS = N_HEADS - FOX_HEADS
NSA_KV_HEADS = 2
NSA_GROUP = NSA_HEADS // NSA_KV_HEADS
CMP_BLOCK = 32
CMP_STRIDE = 16
SEL_BLOCK = 64
SEL_TOPK = 16
WINDOW = 512
Q_BLOCK = 128
SEL_Q_CHUNK = 16
D_FF = 5632
CONV_WIDTH = 3
ROPE_THETA = 10000.0
FORGET_BIAS_INIT = 3.0
FORCE_BONUS = 1000.0
NEG_INF = -1e30
TINY = 1e-30
EPS = 1e-6
D_IN = 3 * FOX_HEADS * HEAD_DIM + FOX_HEADS + NSA_HEADS * HEAD_DIM + 6 * NSA_KV_HEADS * HEAD_DIM + 3 * NSA_HEADS

kernel_name = "hymba_fox_nsa_convffn_adaln"


def _rmsnorm(x, g):
    xf = x.astype(jnp.float32)
    y = xf * lax.rsqrt(jnp.mean(xf * xf, axis=-1, keepdims=True) + EPS)
    return (y * g.astype(jnp.float32)).astype(x.dtype)


def _rope_tables(s):
    inv = ROPE_THETA ** (-jnp.arange(0, HEAD_DIM, 2, dtype=jnp.float32) / HEAD_DIM)
    ang = jnp.arange(s, dtype=jnp.float32)[:, None] * inv[None, :]
    return jnp.cos(ang), jnp.sin(ang)


def _rope(t, cos, sin):
    half = HEAD_DIM // 2
    tf = t.astype(jnp.float32)
    t1, t2 = tf[..., :half], tf[..., half:]
    return jnp.concatenate([t1 * cos - t2 * sin, t1 * sin + t2 * cos], axis=-1).astype(t.dtype)


def _masked_softmax(s, mask):
    s = jnp.where(mask, s, NEG_INF)
    m = jnp.max(s, axis=-1, keepdims=True)
    p = jnp.where(mask, jnp.exp(s - m), 0.0)
    return p / jnp.maximum(jnp.sum(p, axis=-1, keepdims=True), TINY)


def _heads(t, n_heads):
    b, s, _ = t.shape
    return t.reshape(b, s, n_heads, HEAD_DIM).transpose(0, 2, 1, 3)


def _fox_attention(q, k, v, logf):
    b, h, s, d = q.shape
    cum = jnp.cumsum(logf.astype(jnp.float32), axis=-1)
    scale = 1.0 / math.sqrt(d)
    kpos = jnp.arange(s)

    def block(i):
        start = i * Q_BLOCK
        qb = lax.dynamic_slice_in_dim(q, start, Q_BLOCK, axis=2)
        cb = lax.dynamic_slice_in_dim(cum, start, Q_BLOCK, axis=2)
        sc = jnp.einsum('bhqd,bhkd->bhqk', qb, k, preferred_element_type=jnp.float32) * scale \
            + cb[..., :, None] - cum[..., None, :]
        qpos = start + jnp.arange(Q_BLOCK)
        p = _masked_softmax(sc, kpos[None, :] <= qpos[:, None])
        return jnp.einsum('bhqk,bhkd->bhqd', p.astype(v.dtype), v)

    out = lax.map(block, jnp.arange(s // Q_BLOCK))
    return out.transpose(1, 2, 0, 3, 4).reshape(b, h, s, d)


def _compress(t, pos, w1, w2):
    s = t.shape[2]
    n_cmp = (s - CMP_BLOCK) // CMP_STRIDE + 1
    idx = jnp.arange(n_cmp)[:, None] * CMP_STRIDE + jnp.arange(CMP_BLOCK)[None, :]
    blk = t[:, :, idx, :] + pos
    flat = blk.reshape(blk.shape[0], blk.shape[1], n_cmp, CMP_BLOCK * HEAD_DIM)
    return jax.nn.gelu(flat @ w1) @ w2


def _gather_blocks(blocks, idx):
    return jax.vmap(jax.vmap(lambda kb, ib: kb[ib]))(blocks, idx)


def _nsa_attention(q, k_c, v_c, k_s, v_s, k_w, v_w, gates,
                   cmp_pos_k, cmp_pos_v, w_cmp1_k, w_cmp2_k, w_cmp1_v, w_cmp2_v):
    b, _, s, d = q.shape
    scale = 1.0 / math.sqrt(d)
    qg = q.reshape(b, NSA_KV_HEADS, NSA_GROUP, s, d)
    qpos_all = jnp.arange(s)

    kc = _compress(k_c, cmp_pos_k, w_cmp1_k, w_cmp2_k)
    vc = _compress(v_c, cmp_pos_v, w_cmp1_v, w_cmp2_v)
    n_cmp = kc.shape[2]
    blk_end = jnp.arange(n_cmp) * CMP_STRIDE + CMP_BLOCK - 1
    sc = jnp.einsum('bhgqd,bhnd->bhgqn', qg, kc, preferred_element_type=jnp.float32) * scale
    p_cmp = _masked_softmax(sc, blk_end[None, :] <= qpos_all[:, None])
    o_cmp = jnp.einsum('bhgqn,bhnd->bhgqd', p_cmp.astype(vc.dtype), vc)

    n_sel = s // SEL_BLOCK
    p_pad = jnp.pad(p_cmp, ((0, 0), (0, 0), (0, 0), (0, 0), (0, s // CMP_STRIDE - n_cmp)))
    imp = p_pad.reshape(b, NSA_KV_HEADS, NSA_GROUP, s, n_sel, SEL_BLOCK // CMP_STRIDE).sum(axis=(2, 5))
    cur = (qpos_all // SEL_BLOCK)[:, None]
    blk = jnp.arange(n_sel)[None, :]
    forced = (blk == 0) | (blk == cur) | (blk == cur - 1)
    score = jnp.where(blk <= cur, imp + jnp.where(forced, FORCE_BONUS, 0.0), NEG_INF)
    k_eff = min(SEL_TOPK, n_sel)
    _, sel_idx = lax.top_k(score, k_eff)

    ks_blocks = k_s.reshape(b, NSA_KV_HEADS, n_sel, SEL_BLOCK, d)
    vs_blocks = v_s.reshape(b, NSA_KV_HEADS, n_sel, SEL_BLOCK, d)

    def sel_chunk(i):
        start = i * SEL_Q_CHUNK
        qc = lax.dynamic_slice_in_dim(qg, start, SEL_Q_CHUNK, axis=3)
        ic = lax.dynamic_slice_in_dim(sel_idx, start, SEL_Q_CHUNK, axis=2)
        kg = _gather_blocks(ks_blocks, ic)
        vg = _gather_blocks(vs_blocks, ic)
        ss = jnp.einsum('bhgqd,bhqnkd->bhgqnk', qc, kg, preferred_element_type=jnp.float32) * scale
        kpos = ic[..., None] * SEL_BLOCK + jnp.arange(SEL_BLOCK)
        qpos = start + jnp.arange(SEL_Q_CHUNK)
        mask = (kpos <= qpos[:, None, None]).reshape(b, NSA_KV_HEADS, 1, SEL_Q_CHUNK, k_eff * SEL_BLOCK)
        p = _masked_softmax(ss.reshape(b, NSA_KV_HEADS, NSA_GROUP, SEL_Q_CHUNK, k_eff * SEL_BLOCK), mask)
        p = p.reshape(b, NSA_KV_HEADS, NSA_GROUP, SEL_Q_CHUNK, k_eff, SEL_BLOCK)
        return jnp.einsum('bhgqnk,bhqnkd->bhgqd', p.astype(vg.dtype), vg)

    o_sel = lax.map(sel_chunk, jnp.arange(s // SEL_Q_CHUNK))
    o_sel = o_sel.transpose(1, 2, 3, 0, 4, 5).reshape(b, NSA_KV_HEADS, NSA_GROUP, s, d)

    kw_pad = jnp.pad(k_w, ((0, 0), (0, 0), (WINDOW, 0), (0, 0)))
    vw_pad = jnp.pad(v_w, ((0, 0), (0, 0), (WINDOW, 0), (0, 0)))

    def win_block(i):
        start = i * Q_BLOCK
        qb = lax.dynamic_slice_in_dim(qg, start, Q_BLOCK, axis=3)
        kb = lax.dynamic_slice_in_dim(kw_pad, start, WINDOW + Q_BLOCK, axis=2)
        vb = lax.dynamic_slice_in_dim(vw_pad, start, WINDOW + Q_BLOCK, axis=2)
        sw = jnp.einsum('bhgqd,bhkd->bhgqk', qb, kb, preferred_element_type=jnp.float32) * scale
        kpos = start - WINDOW + jnp.arange(WINDOW + Q_BLOCK)
        qpos = start + jnp.arange(Q_BLOCK)
        mask = (kpos[None, :] <= qpos[:, None]) & (kpos[None, :] > qpos[:, None] - WINDOW) & (kpos[None, :] >= 0)
        p = _masked_softmax(sw, mask)
        return jnp.einsum('bhgqk,bhkd->bhgqd', p.astype(vb.dtype), vb)

    o_win = lax.map(win_block, jnp.arange(s // Q_BLOCK))
    o_win = o_win.transpose(1, 2, 3, 0, 4, 5).reshape(b, NSA_KV_HEADS, NSA_GROUP, s, d)

    g = gates.reshape(b, NSA_KV_HEADS, NSA_GROUP, s, 3).astype(q.dtype)
    o = g[..., 0:1] * o_cmp + g[..., 1:2] * o_sel + g[..., 2:3] * o_win
    return o.reshape(b, NSA_HEADS, s, d)


def _hybrid_mixer(h, w_in, b_fgate, cmp_pos_k, cmp_pos_v, w_cmp1_k, w_cmp2_k, w_cmp1_v, w_cmp2_v,
                  w_out, cos, sin):
    b, s, _ = h.shape
    z = h @ w_in
    sizes = [FOX_HEADS * HEAD_DIM] * 3 + [FOX_HEADS] + [NSA_HEADS * HEAD_DIM] \
        + [NSA_KV_HEADS * HEAD_DIM] * 6 + [3 * NSA_HEADS]
    (fq, fk, fv, ff, nq, kc, vc, ks, vs, kw, vw, ng) = jnp.split(z, [int(o) for o in np.cumsum(sizes)[:-1]], axis=-1)

    logf = jax.nn.log_sigmoid((ff + b_fgate).astype(jnp.float32)).transpose(0, 2, 1)
    o_fox = _fox_attention(_heads(fq, FOX_HEADS), _heads(fk, FOX_HEADS), _heads(fv, FOX_HEADS), logf)

    rq = _rope(_heads(nq, NSA_HEADS), cos, sin)
    gates = jax.nn.sigmoid(ng.astype(jnp.float32)).reshape(b, s, NSA_HEADS, 3).transpose(0, 2, 1, 3)
    o_nsa = _nsa_attention(rq,
                           _rope(_heads(kc, NSA_KV_HEADS), cos, sin), _heads(vc, NSA_KV_HEADS),
                           _rope(_heads(ks, NSA_KV_HEADS), cos, sin), _heads(vs, NSA_KV_HEADS),
                           _rope(_heads(kw, NSA_KV_HEADS), cos, sin), _heads(vw, NSA_KV_HEADS),
                           gates, cmp_pos_k, cmp_pos_v, w_cmp1_k, w_cmp2_k, w_cmp1_v, w_cmp2_v)

    o = jnp.concatenate([o_fox, o_nsa], axis=1)
    return o.transpose(0, 2, 1, 3).reshape(b, s, D_MIX) @ w_out


def _conv_ffn(h, w_up, conv_w, conv_b, w_down):
    s = h.shape[1]
    u = h @ w_up
    up = jnp.pad(u, ((0, 0), (CONV_WIDTH - 1, 0), (0, 0)))
    y = conv_b
    for j in range(CONV_WIDTH):
        y = y + conv_w[j] * up[:, j:j + s]
    gate, val = jnp.split(y, 2, axis=-1)
    return (jax.nn.silu(gate) * val) @ w_down


def setup_inputs(seed: int = 0) -> dict:
    key = jax.random.key(seed)
    ks = jax.random.split(key, 20)
    f32 = jnp.float32
    nrm = lambda k, shape, sc: jax.random.normal(k, shape, f32) * sc
    L, D, HD = DEPTH, D_MODEL, HEAD_DIM
    return {
        "x": nrm(ks[0], (BATCH, SEQ, D), 1.0),
        "c": nrm(ks[1], (BATCH, D), 1.0),
        "attn_norm_g": 1.0 + nrm(ks[2], (L, D), 0.01),
        "ffn_norm_g": 1.0 + nrm(ks[3], (L, D), 0.01),
        "w_ada": nrm(ks[4], (L, D, 6 * D), 0.5 * D ** -0.5),
        "b_ada": nrm(ks[5], (L, 6 * D), 0.01),
        "w_in": nrm(ks[6], (L, D, D_IN), D ** -0.5),
        "b_fgate": FORGET_BIAS_INIT + nrm(ks[7], (L, FOX_HEADS), 0.5),
        "cmp_pos_k": nrm(ks[8], (L, CMP_BLOCK, HD), 0.02),
        "cmp_pos_v": nrm(ks[9], (L, CMP_BLOCK, HD), 0.02),
        "w_cmp1_k": nrm(ks[10], (L, CMP_BLOCK * HD, HD), (CMP_BLOCK * HD) ** -0.5),
        "w_cmp2_k": nrm(ks[11], (L, HD, HD), HD ** -0.5),
        "w_cmp1_v": nrm(ks[12], (L, CMP_BLOCK * HD, HD), (CMP_BLOCK * HD) ** -0.5),
        "w_cmp2_v": nrm(ks[13], (L, HD, HD), HD ** -0.5),
        "w_out": nrm(ks[14], (L, D_MIX, D), D_MIX ** -0.5),
        "w_up": nrm(ks[15], (L, D, 2 * D_FF), D ** -0.5),
        "conv_w": nrm(ks[16], (L, CONV_WIDTH, 2 * D_FF), CONV_WIDTH ** -0.5),
        "conv_b": nrm(ks[17], (L, 2 * D_FF), 0.01),
        "w_down": nrm(ks[18], (L, D_FF, D), D_FF ** -0.5),
        "final_norm_g": 1.0 + nrm(ks[19], (D,), 0.01),
    }


def reference(x, c, attn_norm_g, ffn_norm_g, w_ada, b_ada, w_in, b_fgate, cmp_pos_k, cmp_pos_v,
              w_cmp1_k, w_cmp2_k, w_cmp1_v, w_cmp2_v, w_out, w_up, conv_w, conv_b, w_down,
              final_norm_g):
    s = x.shape[1]
    cos, sin = _rope_tables(s)
    c_act = jax.nn.silu(c)
    for l in range(DEPTH):
        mod = c_act @ w_ada[l] + b_ada[l]
        sh_a, sc_a, g_a, sh_f, sc_f, g_f = [m[:, None, :] for m in jnp.split(mod, 6, axis=-1)]
        h = _rmsnorm(x, attn_norm_g[l]) * (1.0 + sc_a) + sh_a
        x = x + g_a * _hybrid_mixer(h, w_in[l], b_fgate[l], cmp_pos_k[l], cmp_pos_v[l],
                                    w_cmp1_k[l], w_cmp2_k[l], w_cmp1_v[l], w_cmp2_v[l], w_out[l], cos, sin)
        h = _rmsnorm(x, ffn_norm_g[l]) * (1.0 + sc_f) + sh_f
        x = x + g_f * _conv_ffn(h, w_up[l], conv_w[l], conv_b[l], w_down[l])
    return _rmsnorm(x, final_norm_g)
```

```python
import functools
import math

import numpy as np
import jax
import jax.numpy as jnp
from jax import lax
from jax.experimental import pallas as pl
from jax.experimental.pallas import tpu as pltpu

HEAD_DIM = 128
FOX_HEADS = 8
NSA_HEADS = 8
NSA_KV_HEADS = 2
NSA_GROUP = NSA_HEADS // NSA_KV_HEADS
CMP_BLOCK = 32
CMP_STRIDE = 16
SEL_BLOCK = 64
SEL_TOPK = 16
WINDOW = 512
CONV_WIDTH = 3
ROPE_THETA = 10000.0
FORCE_BONUS = 1000.0
NEG_INF = -1e30
TINY = 1e-30
EPS = 1e-6

LANES = 128
SUBLANES = 8
VMEM_LIMIT_BYTES = 56 * 1024 * 1024

ROW_TILE = 512
COL_CHUNK = 512
NORM_ROWS = 128
FOX_TQ = 256
FOX_TK = 512
NSA_TQ = 128
NSA_TK = 512
WIN_KEYS = WINDOW + NSA_TQ
HALO = 2 * SUBLANES
FF_CHUNK = 512

BF16 = jnp.bfloat16
F32 = jnp.float32

ZQ_FQ, ZQ_FK, ZQ_FV, ZQ_NQ, ZQ_KS, ZQ_KW, ZQ_VS, ZQ_VW = 0, 8, 16, 24, 32, 34, 36, 38
ZQ_HEADS = 40
ZQ_ROPED = tuple(range(ZQ_NQ, ZQ_KW + NSA_KV_HEADS))
ZC_HEADS = 4
ZS_GATE0 = FOX_HEADS


def _cparams(sem):
    return pltpu.CompilerParams(dimension_semantics=sem, vmem_limit_bytes=VMEM_LIMIT_BYTES)


def _dot(a, b):
    return jnp.dot(a, b, preferred_element_type=F32)


def _dot_nt(a, b):
    return lax.dot_general(a, b, (((1,), (1,)), ((), ())), preferred_element_type=F32)


def _ada_kernel(c_ref, w_ref, b_ref, o_ref):
    c = c_ref[...]
    ca = (c * jax.nn.sigmoid(c)).astype(BF16)
    o_ref[0] = _dot(ca, w_ref[0].astype(BF16)) + b_ref[0]


def _ada_mod(c, w_ada, b_ada):
    depth, d, n = w_ada.shape
    b = c.shape[0]
    tn = 1024
    return pl.pallas_call(
        _ada_kernel,
        grid=(depth, n // tn),
        in_specs=[
            pl.BlockSpec((b, d), lambda l, j: (0, 0)),
            pl.BlockSpec((1, d, tn), lambda l, j: (l, 0, j)),
            pl.BlockSpec((1, 1, tn), lambda l, j: (l, 0, j)),
        ],
        out_specs=pl.BlockSpec((1, b, tn), lambda l, j: (l, 0, j)),
        out_shape=jax.ShapeDtypeStruct((depth, b, n), F32),
        compiler_params=_cparams(("arbitrary", "arbitrary")),
        name="ada_mod",
    )(c, w_ada, b_ada.reshape(depth, 1, n))


def _norm_mod_rows(x, g, shift, scale):
    ms = jnp.mean(x * x, axis=-1, keepdims=True)
    y = (x * lax.rsqrt(ms + EPS)) * g
    return y * (1.0 + scale) + shift


def _norm_mod_to_scratch(x_ref, h_scr, row0, n_rows, g, shift, scale):
    def body(r, _):
        rows = pl.ds(pl.multiple_of(r * NORM_ROWS, NORM_ROWS), NORM_ROWS)
        dst = pl.ds(pl.multiple_of(row0 + r * NORM_ROWS, SUBLANES), NORM_ROWS)
        h_scr[dst, :] = _norm_mod_rows(x_ref[rows, :], g, shift, scale).astype(BF16)
        return 0
    lax.fori_loop(0, n_rows // NORM_ROWS, body, 0)


def _rope_tile(t, cosf, sinf):
    return t * cosf + pltpu.roll(t, HEAD_DIM // 2, axis=1) * sinf


def _in_proj_kernel(x_ref, mod_ref, g_ref, w_ref, cos_ref, sin_ref, zq_ref, zc_ref, zs_ref, h_scr):
    tm = x_ref.shape[0]
    _norm_mod_to_scratch(x_ref, h_scr, 0, tm, g_ref[...], mod_ref[0, 0:1, :], mod_ref[0, 1:2, :])
    heads_per_chunk = COL_CHUNK // HEAD_DIM
    for c0 in range(0, ZQ_HEADS, heads_per_chunk):
        acc = _dot(h_scr[...], w_ref[:, c0 * HEAD_DIM:(c0 + heads_per_chunk) * HEAD_DIM])
        for hh in range(heads_per_chunk):
            t = acc[:, hh * HEAD_DIM:(hh + 1) * HEAD_DIM]
            if c0 + hh in ZQ_ROPED:
                t = _rope_tile(t, cos_ref[...], sin_ref[...])
            zq_ref[:, (c0 + hh) * HEAD_DIM:(c0 + hh + 1) * HEAD_DIM] = t.astype(BF16)
    base = ZQ_HEADS * HEAD_DIM
    acc = _dot(h_scr[...], w_ref[:, base:base + ZC_HEADS * HEAD_DIM])
    for hh in range(ZC_HEADS):
        t = acc[:, hh * HEAD_DIM:(hh + 1) * HEAD_DIM]
        if hh < NSA_KV_HEADS:
            t = _rope_tile(t, cos_ref[...], sin_ref[...])
        zc_ref[:, hh * HEAD_DIM:(hh + 1) * HEAD_DIM] = t
    base += ZC_HEADS * HEAD_DIM
    zs_ref[...] = _dot(h_scr[...], w_ref[:, base:base + LANES])


def _in_proj(x2d, mod, g, wcat, cosf, sinf, seq):
    t, d = x2d.shape
    tm = ROW_TILE
    tpb = seq // tm
    nw = wcat.shape[1]
    return pl.pallas_call(
        _in_proj_kernel,
        grid=(t // tm,),
        in_specs=[
            pl.BlockSpec((tm, d), lambda i: (i, 0)),
            pl.BlockSpec((1, 6, d), lambda i: (i // tpb, 0, 0)),
            pl.BlockSpec((1, d), lambda i: (0, 0)),
            pl.BlockSpec((d, nw), lambda i: (0, 0), pipeline_mode=pl.Buffered(1)),
            pl.BlockSpec((tm, HEAD_DIM), lambda i: (i % tpb, 0)),
            pl.BlockSpec((tm, HEAD_DIM), lambda i: (i % tpb, 0)),
        ],
        out_specs=[
            pl.BlockSpec((tm, ZQ_HEADS * HEAD_DIM), lambda i: (i, 0)),
            pl.BlockSpec((tm, ZC_HEADS * HEAD_DIM), lambda i: (i, 0)),
            pl.BlockSpec((tm, LANES), lambda i: (i, 0)),
        ],
        out_shape=[
            jax.ShapeDtypeStruct((t, ZQ_HEADS * HEAD_DIM), BF16),
            jax.ShapeDtypeStruct((t, ZC_HEADS * HEAD_DIM), F32),
            jax.ShapeDtypeStruct((t, LANES), F32),
        ],
        scratch_shapes=[pltpu.VMEM((tm, d), BF16)],
        compiler_params=_cparams(("arbitrary",)),
        name="in_proj",
    )(x2d, mod, g, wcat, cosf, sinf)


def _split3(v):
    hi = v.astype(BF16)
    r1 = v - hi.astype(F32)
    mid = r1.astype(BF16)
    lo = (r1 - mid.astype(F32)).astype(BF16)
    return hi, mid, lo


def _fgate_kernel(zs_ref, bias_ref, tri_ref, cumc_ref, cumr_ref):
    seq = zs_ref.shape[0]
    tk = cumr_ref.shape[3]
    tri = tri_ref[...]
    carry = jnp.zeros((1, LANES), F32)
    for c in range(seq // LANES):
        zf = zs_ref[c * LANES:(c + 1) * LANES, :] + bias_ref[...]
        lf = jnp.minimum(zf, 0.0) - jnp.log1p(jnp.exp(-jnp.abs(zf)))
        hi, mid, lo = _split3(lf)
        cs = (_dot(tri, lo) + _dot(tri, mid)) + _dot(tri, hi) + carry
        carry = cs[LANES - 1:LANES, :]
        cumc_ref[c * LANES:(c + 1) * LANES, :] = cs
        cst = cs.T
        j, off = divmod(c * LANES, tk)
        for hh in range(FOX_HEADS):
            cumr_ref[0, hh, j:j + 1, off:off + LANES] = cst[hh:hh + 1, :]


def _fgate(zs, bias_row, tri, batch, seq, tk):
    t = zs.shape[0]
    return pl.pallas_call(
        _fgate_kernel,
        grid=(batch,),
        in_specs=[
            pl.BlockSpec((seq, LANES), lambda b: (b, 0)),
            pl.BlockSpec((1, LANES), lambda b: (0, 0)),
            pl.BlockSpec((LANES, LANES), lambda b: (0, 0)),
        ],
        out_specs=[
            pl.BlockSpec((seq, LANES), lambda b: (b, 0)),
            pl.BlockSpec((1, FOX_HEADS, seq // tk, tk), lambda b: (b, 0, 0, 0)),
        ],
        out_shape=[
            jax.ShapeDtypeStruct((t, LANES), F32),
            jax.ShapeDtypeStruct((batch, FOX_HEADS, seq // tk, tk), F32),
        ],
        compiler_params=_cparams(("arbitrary",)),
        name="fgate_cumsum",
    )(zs, bias_row, tri)


def _lane_pick(tile, lane_idx):
    lane = lax.broadcasted_iota(jnp.int32, tile.shape, 1)
    return jnp.sum(jnp.where(lane == lane_idx, tile, 0.0), axis=1, keepdims=True)


def _online_step(carry, s, v):
    m, l, acc = carry
    m_new = jnp.maximum(m, jnp.max(s, axis=1, keepdims=True))
    alpha = jnp.exp(m - m_new)
    p = jnp.exp(s - m_new)
    l = alpha * l + jnp.sum(p, axis=1, keepdims=True)
    acc = alpha * acc + _dot(p.astype(BF16), v)
    return m_new, l, acc


def _fox_kernel(q_ref, k_ref, v_ref, cumc_ref, cumr_ref, o_ref):
    h = pl.program_id(1)
    i = pl.program_id(2)
    tq, tk = FOX_TQ, FOX_TK
    scale = 1.0 / math.sqrt(HEAD_DIM)
    q = q_ref[...]
    cq = _lane_pick(cumc_ref[...], h)
    q0 = i * tq

    def scores(j):
        ks = k_ref[pl.ds(pl.multiple_of(j * tk, tk), tk), :]
        ck = cumr_ref[0, pl.ds(h, 1), pl.ds(j, 1), :].reshape(1, tk)
        return _dot_nt(q, ks) * scale + (cq - ck)

    def vals(j):
        return v_ref[pl.ds(pl.multiple_of(j * tk, tk), tk), :]

    def full_step(j, carry):
        return _online_step(carry, scores(j), vals(j))

    init = (jnp.full((tq, 1), NEG_INF, F32), jnp.zeros((tq, 1), F32), jnp.zeros((tq, HEAD_DIM), F32))
    n_full = q0 // tk
    carry = lax.fori_loop(0, n_full, full_step, init)
    s = scores(n_full)
    qpos = q0 + lax.broadcasted_iota(jnp.int32, (tq, tk), 0)
    kpos = n_full * tk + lax.broadcasted_iota(jnp.int32, (tq, tk), 1)
    s = jnp.where(kpos <= qpos, s, NEG_INF)
    _, l, acc = _online_step(carry, s, vals(n_full))
    o_ref[...] = (acc / l).astype(o_ref.dtype)


def _fox_attention(zq, cumc, cumr, batch, seq):
    t = zq.shape[0]
    tq = FOX_TQ
    nq = seq // tq
    return pl.pallas_call(
        _fox_kernel,
        grid=(batch, FOX_HEADS, nq),
        in_specs=[
            pl.BlockSpec((tq, HEAD_DIM), lambda b, h, i: (b * nq + i, ZQ_FQ + h)),
            pl.BlockSpec((seq, HEAD_DIM), lambda b, h, i: (b, ZQ_FK + h)),
            pl.BlockSpec((seq, HEAD_DIM), lambda b, h, i: (b, ZQ_FV + h)),
            pl.BlockSpec((tq, LANES), lambda b, h, i: (b * nq + i, 0)),
            pl.BlockSpec((1, FOX_HEADS, seq // FOX_TK, FOX_TK), lambda b, h, i: (b, 0, 0, 0)),
        ],
        out_specs=pl.BlockSpec((tq, HEAD_DIM), lambda b, h, i: (b * nq + i, h)),
        out_shape=jax.ShapeDtypeStruct((t, FOX_HEADS * HEAD_DIM), BF16),
        compiler_params=_cparams(("arbitrary", "arbitrary", "arbitrary")),
        name="fox_attention",
    )(zq, zq, zq, cumc, cumr)


def _compress_to(src_ref, pos_ref, w1_ref, w2_ref, flat_scr, dst_scr):
    half = CMP_BLOCK // 2
    n_rows = dst_scr.shape[0]
    for jp in range(half):
        rows = src_ref[pl.ds(jp, n_rows, stride=CMP_STRIDE), :]
        flat_scr[:, jp * HEAD_DIM:(jp + 1) * HEAD_DIM] = (rows + pos_ref[jp:jp + 1, :]).astype(BF16)
        nxt = pltpu.roll(rows, n_rows - 1, axis=0)
        flat_scr[:, (half + jp) * HEAD_DIM:(half + jp + 1) * HEAD_DIM] = (
            nxt + pos_ref[half + jp:half + jp + 1, :]).astype(BF16)
    pre = _dot(flat_scr[...], w1_ref[...])
    dst_scr[...] = _dot(jax.nn.gelu(pre).astype(BF16), w2_ref[...]).astype(BF16)


def _nsa_kernel(q_ref, ks_ref, kw_ref, vs_ref, vw_ref, kcf_ref, vcf_ref, zs_ref,
                posk_ref, posv_ref, w1k_ref, w1v_ref, w2k_ref, w2v_ref, eneg_ref,
                o_ref, kcmp_scr, vcmp_scr, flat_scr, tr_scr):
    hk = pl.program_id(1)
    i = pl.program_id(2)
    tq, tk, grp = NSA_TQ, NSA_TK, NSA_GROUP
    rows4 = grp * tq
    n_selb = SEL_BLOCK // CMP_STRIDE
    n_sel = LANES // n_selb
    scale = 1.0 / math.sqrt(HEAD_DIM)
    q0 = i * tq

    @pl.when(i == 0)
    def _():
        _compress_to(kcf_ref, posk_ref, w1k_ref, w2k_ref, flat_scr, kcmp_scr)
        _compress_to(vcf_ref, posv_ref, w1v_ref, w2v_ref, flat_scr, vcmp_scr)

    q = q_ref[...]
    q4 = jnp.concatenate([q[:, g * HEAD_DIM:(g + 1) * HEAD_DIM] for g in range(grp)], axis=0)

    def stacked_qpos(width):
        r = lax.broadcasted_iota(jnp.int32, (rows4, width), 0)
        return q0 + (r & (tq - 1))

    sc = _dot_nt(q4, kcmp_scr[...]) * scale
    n_idx = lax.broadcasted_iota(jnp.int32, (rows4, LANES), 1)
    cvalid = (n_idx * CMP_STRIDE + (CMP_BLOCK - 1)) <= stacked_qpos(LANES)
    sc = jnp.where(cvalid, sc, NEG_INF)
    pc = jnp.where(cvalid, jnp.exp(sc - jnp.max(sc, axis=1, keepdims=True)), 0.0)
    pc = pc / jnp.maximum(jnp.sum(pc, axis=1, keepdims=True), TINY)
    o_cmp = _dot(pc.astype(BF16), vcmp_scr[...])

    psum = (pc[0:tq] + pc[tq:2 * tq]) + (pc[2 * tq:3 * tq] + pc[3 * tq:4 * tq])
    tr_scr[...] = psum.T
    imp = ((tr_scr[pl.ds(0, n_sel, stride=n_selb), :] + tr_scr[pl.ds(1, n_sel, stride=n_selb), :])
           + (tr_scr[pl.ds(2, n_sel, stride=n_selb), :] + tr_scr[pl.ds(3, n_sel, stride=n_selb), :]))
    blk = lax.broadcasted_iota(jnp.int32, (n_sel, tq), 0)
    cur = (q0 + lax.broadcasted_iota(jnp.int32, (n_sel, tq), 1)) // SEL_BLOCK
    forced = jnp.where(blk == 0, 1.0, jnp.where(blk == cur, 1.0, jnp.where(blk == cur - 1, 1.0, 0.0)))
    score = jnp.where(blk <= cur, imp + forced * FORCE_BONUS, NEG_INF)
    rank = jnp.zeros((n_sel, tq), F32)
    for kk in range(n_sel):
        row = score[kk:kk + 1, :]
        earlier = jnp.where(blk > kk, 1.0, 0.0)
        rank = rank + jnp.where(row > score, 1.0, jnp.where(row == score, earlier, 0.0))
    tr_scr[0:n_sel, :] = jnp.where(rank < float(SEL_TOPK), 0.0, 1.0)
    tr_scr[n_sel:, :] = jnp.zeros((LANES - n_sel, tq), F32)
    notsel = tr_scr[...].T.astype(BF16)

    def sel_scores(j):
        ksj = ks_ref[pl.ds(pl.multiple_of(j * tk, tk), tk), :]
        bias = _dot(notsel, eneg_ref[j])
        return _dot_nt(q4, ksj) * scale + jnp.concatenate([bias] * grp, axis=0)

    def sel_vals(j):
        return vs_ref[pl.ds(pl.multiple_of(j * tk, tk), tk), :]

    init = (jnp.full((rows4, 1), NEG_INF, F32), jnp.zeros((rows4, 1), F32), jnp.zeros((rows4, HEAD_DIM), F32))
    n_full = q0 // tk
    carry = lax.fori_loop(0, n_full, lambda j, c: _online_step(c, sel_scores(j), sel_vals(j)), init)
    s = sel_scores(n_full)
    kpos = n_full * tk + lax.broadcasted_iota(jnp.int32, (rows4, tk), 1)
    s = jnp.where(kpos <= stacked_qpos(tk), s, NEG_INF)
    _, l, acc = _online_step(carry, s, sel_vals(n_full))
    o_sel = acc / l

    w0 = pl.multiple_of(jnp.maximum(q0 - WINDOW, 0), tq)
    kwj = kw_ref[pl.ds(w0, WIN_KEYS), :]
    sw = _dot_nt(q4, kwj) * scale
    kposw = w0 + lax.broadcasted_iota(jnp.int32, (rows4, WIN_KEYS), 1)
    qposw = stacked_qpos(WIN_KEYS)
    wvalid = jnp.abs(2 * (qposw - kposw) - (WINDOW - 1)) <= (WINDOW - 1)
    sw = jnp.where(wvalid, sw, NEG_INF)
    pw = jnp.where(wvalid, jnp.exp(sw - jnp.max(sw, axis=1, keepdims=True)), 0.0)
    pw = pw / jnp.maximum(jnp.sum(pw, axis=1, keepdims=True), TINY)
    o_win = _dot(pw.astype(BF16), vw_ref[pl.ds(w0, WIN_KEYS), :])

    gl = jax.nn.sigmoid(zs_ref[...])
    def gate(c):
        cols = [_lane_pick(gl, ZS_GATE0 + 3 * (hk * grp + g) + c) for g in range(grp)]
        return jnp.concatenate(cols, axis=0)
    o4 = gate(0) * o_cmp + gate(1) * o_sel + gate(2) * o_win
    for g in range(grp):
        o_ref[:, g * HEAD_DIM:(g + 1) * HEAD_DIM] = o4[g * tq:(g + 1) * tq, :].astype(o_ref.dtype)


def _nsa_attention(zq, zc, zs, posk, posv, w1k, w1v, w2k, w2v, eneg, batch, seq):
    t = zq.shape[0]
    tq = NSA_TQ
    nq = seq // tq
    gw = NSA_GROUP * HEAD_DIM
    kv = lambda col: pl.BlockSpec((seq, HEAD_DIM), lambda b, hk, i: (b, col + hk))
    const = lambda shape: pl.BlockSpec(shape, lambda b, hk, i: (0,) * len(shape))
    return pl.pallas_call(
        _nsa_kernel,
        grid=(batch, NSA_KV_HEADS, nq),
        in_specs=[
            pl.BlockSpec((tq, gw), lambda b, hk, i: (b * nq + i, ZQ_NQ // NSA_GROUP + hk)),
            kv(ZQ_KS), kv(ZQ_KW), kv(ZQ_VS), kv(ZQ_VW),
            pl.BlockSpec((seq, HEAD_DIM), lambda b, hk, i: (b, hk)),
            pl.BlockSpec((seq, HEAD_DIM), lambda b, hk, i: (b, NSA_KV_HEADS + hk)),
            pl.BlockSpec((tq, LANES), lambda b, hk, i: (b * nq + i, 0)),
            const((CMP_BLOCK, HEAD_DIM)), const((CMP_BLOCK, HEAD_DIM)),
            const((CMP_BLOCK * HEAD_DIM, HEAD_DIM)), const((CMP_BLOCK * HEAD_DIM, HEAD_DIM)),
            const((HEAD_DIM, HEAD_DIM)), const((HEAD_DIM, HEAD_DIM)),
            const((seq // NSA_TK, LANES, NSA_TK)),
        ],
        out_specs=pl.BlockSpec((tq, gw), lambda b, hk, i: (b * nq + i, hk)),
        out_shape=jax.ShapeDtypeStruct((t, NSA_HEADS * HEAD_DIM), BF16),
        scratch_shapes=[
            pltpu.VMEM((LANES, HEAD_DIM), BF16),
            pltpu.VMEM((LANES, HEAD_DIM), BF16),
            pltpu.VMEM((LANES, CMP_BLOCK * HEAD_DIM), BF16),
            pltpu.VMEM((LANES, tq), F32),
        ],
        compiler_params=_cparams(("arbitrary", "arbitrary", "arbitrary")),
        name="nsa_attention",
    )(zq, zq, zq, zq, zq, zc, zc, zs, posk, posv, w1k, w1v, w2k, w2v, eneg)


def _out_proj_kernel(of_ref, on_ref, wf_ref, wn_ref, x_ref, mod_ref, o_ref):
    n = o_ref.shape[1]
    for c0 in range(0, n, COL_CHUNK):
        cols = slice(c0, c0 + COL_CHUNK)
        acc = _dot(of_ref[...], wf_ref[:, cols]) + _dot(on_ref[...], wn_ref[:, cols])
        o_ref[:, cols] = x_ref[:, cols] + mod_ref[0, 2:3, cols] * acc


def _out_proj(o_fox, o_nsa, w_fox, w_nsa, x2d, mod, seq):
    t, d = x2d.shape
    tm = ROW_TILE
    tpb = seq // tm
    kf, kn = o_fox.shape[1], o_nsa.shape[1]
    return pl.pallas_call(
        _out_proj_kernel,
        grid=(t // tm,),
        in_specs=[
            pl.BlockSpec((tm, kf), lambda i: (i, 0)),
            pl.BlockSpec((tm, kn), lambda i: (i, 0)),
            pl.BlockSpec((kf, d), lambda i: (0, 0), pipeline_mode=pl.Buffered(1)),
            pl.BlockSpec((kn, d), lambda i: (0, 0), pipeline_mode=pl.Buffered(1)),
            pl.BlockSpec((tm, d), lambda i: (i, 0)),
            pl.BlockSpec((1, 6, d), lambda i: (i // tpb, 0, 0)),
        ],
        out_specs=pl.BlockSpec((tm, d), lambda i: (i, 0)),
        out_shape=jax.ShapeDtypeStruct((t, d), F32),
        compiler_params=_cparams(("arbitrary",)),
        name="out_proj",
    )(o_fox, o_nsa, w_fox, w_nsa, x2d, mod)


def _conv_rows(u, cw_ref, cb_ref):
    y = cb_ref[...] + cw_ref[0:1, :] * pltpu.roll(u, 2, axis=0)
    y = y + cw_ref[1:2, :] * pltpu.roll(u, 1, axis=0)
    return y + cw_ref[2:3, :] * u


def _ffn_kernel(x_ref, xh_ref, mod_ref, g_ref, wg_ref, wv_ref, cwg_ref, cwv_ref, cbg_ref, cbv_ref,
                wd_ref, fg_ref, o_ref, h_scr, acc_scr, *, tiles_per_batch, final_norm):
    i = pl.program_id(0)
    c = pl.program_id(1)
    tm = x_ref.shape[0]

    @pl.when(c == 0)
    def _():
        g, shift, scale = g_ref[...], mod_ref[0, 3:4, :], mod_ref[0, 4:5, :]
        halo = _norm_mod_rows(xh_ref[...], g, shift, scale)
        first = (i % tiles_per_batch) == 0
        h_scr[0:HALO, :] = jnp.where(first, 0.0, halo).astype(BF16)
        _norm_mod_to_scratch(x_ref, h_scr, HALO, tm, g, shift, scale)
        acc_scr[...] = jnp.zeros_like(acc_scr)

    h = h_scr[...]
    yg = _conv_rows(_dot(h, wg_ref[...]), cwg_ref, cbg_ref)[HALO:, :]
    yv = _conv_rows(_dot(h, wv_ref[...]), cwv_ref, cbv_ref)[HALO:, :]
    act = (yg * jax.nn.sigmoid(yg)) * yv
    acc_scr[...] += _dot(act.astype(BF16), wd_ref[...])

    @pl.when(c == pl.num_programs(1) - 1)
    def _():
        y = x_ref[...] + mod_ref[0, 5:6, :] * acc_scr[...]
        if final_norm:
            ms = jnp.mean(y * y, axis=-1, keepdims=True)
            y = (y * lax.rsqrt(ms + EPS)) * fg_ref[...]
        o_ref[...] = y


def _ffn(x2d, mod, g, w_up, conv_w, conv_b, w_down, final_g, seq, final_norm):
    t, d = x2d.shape
    dff = w_down.shape[0]
    tm, tf = ROW_TILE, FF_CHUNK
    tpb = seq // tm
    nc = dff // tf
    hb = tm // HALO
    kern = functools.partial(_ffn_kernel, tiles_per_batch=tpb, final_norm=final_norm)
    return pl.pallas_call(
        kern,
        grid=(t // tm, nc),
        in_specs=[
            pl.BlockSpec((tm, d), lambda i, c: (i, 0)),
            pl.BlockSpec((HALO, d), lambda i, c: (jnp.maximum(i * hb - 1, 0), 0)),
            pl.BlockSpec((1, 6, d), lambda i, c: (i // tpb, 0, 0)),
            pl.BlockSpec((1, d), lambda i, c: (0, 0)),
            pl.BlockSpec((d, tf), lambda i, c: (0, c)),
            pl.BlockSpec((d, tf), lambda i, c: (0, nc + c)),
            pl.BlockSpec((CONV_WIDTH, tf), lambda i, c: (0, c)),
            pl.BlockSpec((CONV_WIDTH, tf), lambda i, c: (0, nc + c)),
            pl.BlockSpec((1, tf), lambda i, c: (0, c)),
            pl.BlockSpec((1, tf), lambda i, c: (0, nc + c)),
            pl.BlockSpec((tf, d), lambda i, c: (c, 0)),
            pl.BlockSpec((1, d), lambda i, c: (0, 0)),
        ],
        out_specs=pl.BlockSpec((tm, d), lambda i, c: (i, 0)),
        out_shape=jax.ShapeDtypeStruct((t, d), F32),
        scratch_shapes=[pltpu.VMEM((HALO + tm, d), BF16), pltpu.VMEM((tm, d), F32)],
        compiler_params=_cparams(("arbitrary", "arbitrary")),
        name="conv_ffn",
    )(x2d, x2d, mod, g, w_up, w_up, conv_w, conv_w, conv_b, conv_b, w_down, final_g)


def _rope_tables(seq):
    inv = ROPE_THETA ** (-jnp.arange(0, HEAD_DIM, 2, dtype=F32) / HEAD_DIM)
    ang = jnp.arange(seq, dtype=F32)[:, None] * inv[None, :]
    cos, sin = jnp.cos(ang), jnp.sin(ang)
    return jnp.concatenate([cos, cos], axis=-1), jnp.concatenate([-sin, sin], axis=-1)


def _select_mask_table(seq):
    key_blk = (np.arange(seq) // SEL_BLOCK).reshape(seq // NSA_TK, 1, NSA_TK)
    rows = np.arange(LANES).reshape(1, LANES, 1)
    return jnp.asarray(np.where(rows == key_blk, NEG_INF, 0.0), dtype=BF16)


def _split_w_in(w):
    hd = HEAD_DIM
    sizes = [FOX_HEADS * hd] * 3 + [FOX_HEADS] + [NSA_HEADS * hd] + [NSA_KV_HEADS * hd] * 6 + [3 * NSA_HEADS]
    fq, fk, fv, ff, nq, kc, vc, ks, vs, kw, vw, ng = jnp.split(w, [int(o) for o in np.cumsum(sizes)[:-1]], axis=-1)
    pad = jnp.zeros((w.shape[0], LANES - FOX_HEADS - 3 * NSA_HEADS), w.dtype)
    return jnp.concatenate([fq, fk, fv, nq, ks, kw, vs, vw, kc, vc, ff, ng, pad], axis=-1).astype(BF16)


def kernel(x, c, attn_norm_g, ffn_norm_g, w_ada, b_ada, w_in, b_fgate, cmp_pos_k, cmp_pos_v,
           w_cmp1_k, w_cmp2_k, w_cmp1_v, w_cmp2_v, w_out, w_up, conv_w, conv_b, w_down, final_norm_g):
    batch, seq, d = x.shape
    depth = w_ada.shape[0]
    assert seq % ROW_TILE == 0 and seq % FOX_TK == 0 and seq % NSA_TK == 0 and seq // SEL_BLOCK == LANES // 4
    assert seq >= WIN_KEYS and w_down.shape[1] % FF_CHUNK == 0 and d % COL_CHUNK == 0

    cosf, sinf = _rope_tables(seq)
    eneg = _select_mask_table(seq)
    tri = jnp.asarray(np.tril(np.ones((LANES, LANES))), dtype=BF16)
    mod_all = _ada_mod(c, w_ada, b_ada)

    x2d = x.reshape(batch * seq, d)
    n_fox = FOX_HEADS * HEAD_DIM
    for l in range(depth):
        mod = mod_all[l].reshape(batch, 6, d)
        zq, zc, zs = _in_proj(x2d, mod, attn_norm_g[l].reshape(1, d), _split_w_in(w_in[l]), cosf, sinf, seq)
        bias_row = jnp.zeros((1, LANES), F32).at[0, :FOX_HEADS].set(b_fgate[l])
        cumc, cumr = _fgate(zs, bias_row, tri, batch, seq, FOX_TK)
        o_fox = _fox_attention(zq, cumc, cumr, batch, seq)
        o_nsa = _nsa_attention(zq, zc, zs, cmp_pos_k[l], cmp_pos_v[l],
                               w_cmp1_k[l].astype(BF16), w_cmp1_v[l].astype(BF16),
                               w_cmp2_k[l].astype(BF16), w_cmp2_v[l].astype(BF16), eneg, batch, seq)
        w_o = w_out[l].astype(BF16)
        x2d = _out_proj(o_fox, o_nsa, w_o[:n_fox], w_o[n_fox:], x2d, mod, seq)
        x2d = _ffn(x2d, mod, ffn_norm_g[l].reshape(1, d), w_up[l].astype(BF16), conv_w[l],
                   conv_b[l].reshape(1, -1), w_down[l].astype(BF16), final_norm_g.reshape(1, d), seq,
                   final_norm=(l == depth - 1))
    return x2d.reshape(batch, seq, d)
```

```python
import functools
import math

import numpy as np
import jax
import jax.numpy as jnp
from jax import lax
from jax.experimental import pallas as pl
from jax.experimental.pallas import tpu as pltpu

HEAD_DIM = 128
FOX_HEADS = 8
NSA_HEADS = 8
NSA_KV_HEADS = 2
NSA_GROUP = NSA_HEADS // NSA_KV_HEADS
CMP_BLOCK = 32
CMP_STRIDE = 16
SEL_BLOCK = 64
SEL_TOPK = 16
WINDOW = 512
CONV_WIDTH = 3
ROPE_THETA = 10000.0
FORCE_BONUS = 1000.0
NEG_INF = -1e30
TINY = 1e-30
EPS = 1e-6
LOG2E = math.log2(math.e)

LANES = 128
SUBLANES = 8
VMEM_LIMIT_BYTES = 56 * 1024 * 1024

ROW_TILE = 512
COL_CHUNK = 512
NORM_ROWS = 128
FOX_TQ = 512
FOX_GROUP = 2
NSA_TQ = 256
NSA_TK = 512
WIN_KEYS = WINDOW + NSA_TQ
HALO = 2 * SUBLANES
FF_CHUNK = 512

BF16 = jnp.bfloat16
F32 = jnp.float32

ZQ_FQ, ZQ_FK, ZQ_FV, ZQ_NQ, ZQ_KS, ZQ_KW, ZQ_VS, ZQ_VW = 0, 8, 16, 24, 32, 34, 36, 38
ZQ_HEADS = 40
ZQ_ROPED = tuple(range(ZQ_NQ, ZQ_KW + NSA_KV_HEADS))
ZC_HEADS = 4
ZS_GATE0 = FOX_HEADS


def _cparams(sem):
    return pltpu.CompilerParams(dimension_semantics=sem, vmem_limit_bytes=VMEM_LIMIT_BYTES)


def _dot(a, b):
    return jnp.dot(a, b, preferred_element_type=F32)


def _dot_nt(a, b):
    return lax.dot_general(a, b, (((1,), (1,)), ((), ())), preferred_element_type=F32)


def _ada_kernel(c_ref, w_ref, b_ref, o_ref):
    c = c_ref[...]
    ca = (c * jax.nn.sigmoid(c)).astype(BF16)
    o_ref[0] = _dot(ca, w_ref[0].astype(BF16)) + b_ref[0]


def _ada_mod(c, w_ada, b_ada):
    depth, d, n = w_ada.shape
    b = c.shape[0]
    tn = 1024
    return pl.pallas_call(
        _ada_kernel,
        grid=(depth, n // tn),
        in_specs=[
            pl.BlockSpec((b, d), lambda l, j: (0, 0)),
            pl.BlockSpec((1, d, tn), lambda l, j: (l, 0, j)),
            pl.BlockSpec((1, 1, tn), lambda l, j: (l, 0, j)),
        ],
        out_specs=pl.BlockSpec((1, b, tn), lambda l, j: (l, 0, j)),
        out_shape=jax.ShapeDtypeStruct((depth, b, n), F32),
        compiler_params=_cparams(("arbitrary", "arbitrary")),
        name="ada_mod",
    )(c, w_ada, b_ada.reshape(depth, 1, n))


def _norm_mod_rows(x, g, shift, scale):
    ms = jnp.mean(x * x, axis=-1, keepdims=True)
    y = (x * lax.rsqrt(ms + EPS)) * g
    return y * (1.0 + scale) + shift


def _norm_mod_to_scratch(x_ref, h_scr, row0, n_rows, g, shift, scale):
    def body(r, _):
        rows = pl.ds(pl.multiple_of(r * NORM_ROWS, NORM_ROWS), NORM_ROWS)
        dst = pl.ds(pl.multiple_of(row0 + r * NORM_ROWS, SUBLANES), NORM_ROWS)
        h_scr[dst, :] = _norm_mod_rows(x_ref[rows, :], g, shift, scale).astype(BF16)
        return 0
    lax.fori_loop(0, n_rows // NORM_ROWS, body, 0)


def _rope_tile(t, cosf, sinf):
    return t * cosf + pltpu.roll(t, HEAD_DIM // 2, axis=1) * sinf


def _in_proj_kernel(x_ref, mod_ref, g_ref, w_ref, cos_ref, sin_ref, zq_ref, zc_ref, zs_ref, h_scr):
    tm = x_ref.shape[0]
    _norm_mod_to_scratch(x_ref, h_scr, 0, tm, g_ref[...], mod_ref[0, 0:1, :], mod_ref[0, 1:2, :])
    heads_per_chunk = COL_CHUNK // HEAD_DIM
    for c0 in range(0, ZQ_HEADS, heads_per_chunk):
        acc = _dot(h_scr[...], w_ref[:, c0 * HEAD_DIM:(c0 + heads_per_chunk) * HEAD_DIM])
        for hh in range(heads_per_chunk):
            t = acc[:, hh * HEAD_DIM:(hh + 1) * HEAD_DIM]
            if c0 + hh in ZQ_ROPED:
                t = _rope_tile(t, cos_ref[...], sin_ref[...])
            zq_ref[:, (c0 + hh) * HEAD_DIM:(c0 + hh + 1) * HEAD_DIM] = t.astype(BF16)
    base = ZQ_HEADS * HEAD_DIM
    acc = _dot(h_scr[...], w_ref[:, base:base + ZC_HEADS * HEAD_DIM])
    for hh in range(ZC_HEADS):
        t = acc[:, hh * HEAD_DIM:(hh + 1) * HEAD_DIM]
        if hh < NSA_KV_HEADS:
            t = _rope_tile(t, cos_ref[...], sin_ref[...])
        zc_ref[:, hh * HEAD_DIM:(hh + 1) * HEAD_DIM] = t
    base += ZC_HEADS * HEAD_DIM
    zs_ref[...] = _dot(h_scr[...], w_ref[:, base:base + LANES])


def _in_proj(x2d, mod, g, wcat, cosf, sinf, seq):
    t, d = x2d.shape
    tm = ROW_TILE
    tpb = seq // tm
    nw = wcat.shape[1]
    return pl.pallas_call(
        _in_proj_kernel,
        grid=(t // tm,),
        in_specs=[
            pl.BlockSpec((tm, d), lambda i: (i, 0)),
            pl.BlockSpec((1, 6, d), lambda i: (i // tpb, 0, 0)),
            pl.BlockSpec((1, d), lambda i: (0, 0)),
            pl.BlockSpec((d, nw), lambda i: (0, 0), pipeline_mode=pl.Buffered(1)),
            pl.BlockSpec((tm, HEAD_DIM), lambda i: (i % tpb, 0)),
            pl.BlockSpec((tm, HEAD_DIM), lambda i: (i % tpb, 0)),
        ],
        out_specs=[
            pl.BlockSpec((tm, ZQ_HEADS * HEAD_DIM), lambda i: (i, 0)),
            pl.BlockSpec((tm, ZC_HEADS * HEAD_DIM), lambda i: (i, 0)),
            pl.BlockSpec((tm, LANES), lambda i: (i, 0)),
        ],
        out_shape=[
            jax.ShapeDtypeStruct((t, ZQ_HEADS * HEAD_DIM), BF16),
            jax.ShapeDtypeStruct((t, ZC_HEADS * HEAD_DIM), F32),
            jax.ShapeDtypeStruct((t, LANES), F32),
        ],
        scratch_shapes=[pltpu.VMEM((tm, d), BF16)],
        compiler_params=_cparams(("arbitrary",)),
        name="in_proj",
    )(x2d, mod, g, wcat, cosf, sinf)


def _split3(v):
    hi = v.astype(BF16)
    r1 = v - hi.astype(F32)
    mid = r1.astype(BF16)
    lo = (r1 - mid.astype(F32)).astype(BF16)
    return hi, mid, lo


def _fgate_kernel(zs_ref, bias_ref, tri_ref, cumc_ref):
    seq = zs_ref.shape[0]
    tri = tri_ref[...]
    carry = jnp.zeros((1, LANES), F32)
    for c in range(seq // LANES):
        zf = zs_ref[c * LANES:(c + 1) * LANES, :] + bias_ref[...]
        lf = jnp.minimum(zf, 0.0) - jnp.log1p(jnp.exp(-jnp.abs(zf)))
        hi, mid, lo = _split3(lf)
        cs = (_dot(tri, lo) + _dot(tri, mid)) + _dot(tri, hi) + carry
        carry = cs[LANES - 1:LANES, :]
        cumc_ref[c * LANES:(c + 1) * LANES, :] = cs


def _fgate(zs, bias_row, tri, batch, seq):
    t = zs.shape[0]
    return pl.pallas_call(
        _fgate_kernel,
        grid=(batch,),
        in_specs=[
            pl.BlockSpec((seq, LANES), lambda b: (b, 0)),
            pl.BlockSpec((1, LANES), lambda b: (0, 0)),
            pl.BlockSpec((LANES, LANES), lambda b: (0, 0)),
        ],
        out_specs=pl.BlockSpec((seq, LANES), lambda b: (b, 0)),
        out_shape=jax.ShapeDtypeStruct((t, LANES), F32),
        compiler_params=_cparams(("arbitrary",)),
        name="fgate_cumsum",
    )(zs, bias_row, tri)


def _lane_pick(tile, lane_idx):
    lane = lax.broadcasted_iota(jnp.int32, tile.shape, 1)
    return jnp.sum(jnp.where(lane == lane_idx, tile, 0.0), axis=1, keepdims=True)


def _online_step(carry, s2, v):
    m, l, acc = carry
    m_new = jnp.maximum(m, jnp.max(s2, axis=1, keepdims=True))
    alpha = jnp.exp2(m - m_new)
    p = jnp.exp2(s2 - m_new)
    l = alpha * l + jnp.sum(p, axis=1, keepdims=True)
    acc = alpha * acc + _dot(p.astype(BF16), v)
    return m_new, l, acc


def _online_init(rows):
    return (jnp.full((rows, 1), NEG_INF, F32), jnp.zeros((rows, 1), F32), jnp.zeros((rows, HEAD_DIM), F32))


def _placed_pieces(pieces, head, first_lane, sign, const_lanes):
    row = lax.broadcasted_iota(jnp.int32, (LANES, LANES), 0)
    col = lax.broadcasted_iota(jnp.int32, (LANES, LANES), 1)
    out = None
    for p, piece in enumerate(pieces):
        place = jnp.where(row == head, jnp.where(col == first_lane + p, sign, 0.0), 0.0).astype(BF16)
        term = _dot(piece, place)
        out = term if out is None else out + term
    lane = lax.broadcasted_iota(jnp.int32, (1, LANES), 1)
    ones = jnp.where(lane >= const_lanes[0], jnp.where(lane < const_lanes[1], 1.0, 0.0), 0.0)
    return (out + ones).astype(BF16)


def _fox_kernel(q_ref, k_ref, v_ref, cum_ref, o_ref, qaug_scr, kaug_scr):
    hb = pl.program_id(1)
    tile, grp = FOX_TQ, FOX_GROUP
    seq = q_ref.shape[0]
    scale = 1.0 / math.sqrt(HEAD_DIM)
    c2 = scale * LOG2E
    npc = 3
    head = lambda h: slice(h * HEAD_DIM, (h + 1) * HEAD_DIM)

    pieces = _split3(cum_ref[...] * (1.0 / scale))
    for g in range(grp):
        qaug_scr[:, head(2 * g)] = q_ref[:, head(g)]
        qaug_scr[:, head(2 * g + 1)] = _placed_pieces(pieces, hb * grp + g, 0, 1.0, (npc, 2 * npc))
        kaug_scr[:, head(2 * g)] = k_ref[:, head(g)]
        kaug_scr[:, head(2 * g + 1)] = _placed_pieces(pieces, hb * grp + g, npc, -1.0, (0, npc))

    causal = (lax.broadcasted_iota(jnp.int32, (tile, tile), 1) <= lax.broadcasted_iota(jnp.int32, (tile, tile), 0))
    for i in range(seq // tile):
        qrows = slice(i * tile, (i + 1) * tile)
        for g in range(grp):
            carry = _online_init(tile)
            for j in range(i + 1):
                krows = slice(j * tile, (j + 1) * tile)
                s2 = _dot_nt(qaug_scr[qrows, 2 * g * HEAD_DIM:(2 * g + 2) * HEAD_DIM],
                             kaug_scr[krows, 2 * g * HEAD_DIM:(2 * g + 2) * HEAD_DIM]) * c2
                if j == i:
                    s2 = jnp.where(causal, s2, NEG_INF)
                carry = _online_step(carry, s2, v_ref[krows, head(g)])
            _, l, acc = carry
            o_ref[qrows, head(g)] = (acc / l).astype(o_ref.dtype)


def _fox_attention(zq, cumc, batch, seq):
    t = zq.shape[0]
    grp = FOX_GROUP
    gw = grp * HEAD_DIM
    return pl.pallas_call(
        _fox_kernel,
        grid=(batch, FOX_HEADS // grp),
        in_specs=[
            pl.BlockSpec((seq, gw), lambda b, h: (b, ZQ_FQ // grp + h)),
            pl.BlockSpec((seq, gw), lambda b, h: (b, ZQ_FK // grp + h)),
            pl.BlockSpec((seq, gw), lambda b, h: (b, ZQ_FV // grp + h)),
            pl.BlockSpec((seq, LANES), lambda b, h: (b, 0)),
        ],
        out_specs=pl.BlockSpec((seq, gw), lambda b, h: (b, h)),
        out_shape=jax.ShapeDtypeStruct((t, FOX_HEADS * HEAD_DIM), BF16),
        scratch_shapes=[pltpu.VMEM((seq, 2 * gw), BF16), pltpu.VMEM((seq, 2 * gw), BF16)],
        compiler_params=_cparams(("arbitrary", "arbitrary")),
        name="fox_attention",
    )(zq, zq, zq, cumc)


def _compress_to(src_ref, hk, pos_ref, w1_ref, w2_ref, flat_scr, dst_scr):
    half = CMP_BLOCK // 2
    n_rows = dst_scr.shape[0]
    cols = slice(hk * HEAD_DIM, (hk + 1) * HEAD_DIM)
    for jp in range(half):
        rows = src_ref[pl.ds(jp, n_rows, stride=CMP_STRIDE), :]
        flat_scr[:, jp * HEAD_DIM:(jp + 1) * HEAD_DIM] = (rows + pos_ref[jp:jp + 1, :]).astype(BF16)
        nxt = pltpu.roll(rows, n_rows - 1, axis=0)
        flat_scr[:, (half + jp) * HEAD_DIM:(half + jp + 1) * HEAD_DIM] = (
            nxt + pos_ref[half + jp:half + jp + 1, :]).astype(BF16)
    pre = _dot(flat_scr[...], w1_ref[...])
    dst_scr[:, cols] = _dot(jax.nn.gelu(pre).astype(BF16), w2_ref[...]).astype(BF16)


def _nsa_kernel(q_ref, ks_ref, kw_ref, vs_ref, vw_ref, kcf0_ref, kcf1_ref, vcf0_ref, vcf1_ref, zs_ref,
                posk_ref, posv_ref, w1k_ref, w1v_ref, w2k_ref, w2v_ref, emask_ref,
                o_ref, ksaug_scr, kcmp_scr, vcmp_scr, flat_scr, tr_scr):
    i = pl.program_id(1)
    tq, tk, grp, nkv = NSA_TQ, NSA_TK, NSA_GROUP, NSA_KV_HEADS
    rows4 = grp * tq
    n_selb = SEL_BLOCK // CMP_STRIDE
    n_sel = LANES // n_selb
    c2 = LOG2E / math.sqrt(HEAD_DIM)
    q0 = i * tq
    head = lambda h: slice(h * HEAD_DIM, (h + 1) * HEAD_DIM)

    @pl.when(i == 0)
    def _():
        for hk in range(nkv):
            ksaug_scr[:, head(2 * hk)] = ks_ref[:, head(hk)]
            ksaug_scr[:, head(2 * hk + 1)] = emask_ref[...]
            _compress_to((kcf0_ref, kcf1_ref)[hk], hk, posk_ref, w1k_ref, w2k_ref, flat_scr, kcmp_scr)
            _compress_to((vcf0_ref, vcf1_ref)[hk], hk, posv_ref, w1v_ref, w2v_ref, flat_scr, vcmp_scr)

    def stack4(tile):
        return jnp.concatenate([tile] * grp, axis=0)

    rloc = lax.broadcasted_iota(jnp.int32, (tq, LANES), 0)
    lane = lax.broadcasted_iota(jnp.int32, (tq, LANES), 1)
    cvalid4 = stack4(jnp.where((lane * CMP_STRIDE + (CMP_BLOCK - 1)) <= q0 + rloc, 1.0, 0.0)) > 0.5
    blk = lax.broadcasted_iota(jnp.int32, (n_sel, tq), 0)
    cur = lax.shift_right_logical(q0 + lax.broadcasted_iota(jnp.int32, (n_sel, tq), 1), int(math.log2(SEL_BLOCK)))
    forced = jnp.where(blk == 0, 1.0, jnp.where(blk == cur, 1.0, jnp.where(blk == cur - 1, 1.0, 0.0)))
    gl = jax.nn.sigmoid(zs_ref[...])

    def prepare(hk):
        q4 = jnp.concatenate([q_ref[:, head(hk * grp + g)] for g in range(grp)], axis=0)
        sc = jnp.where(cvalid4, _dot_nt(q4, kcmp_scr[:, head(hk)]) * c2, NEG_INF)
        pc = jnp.where(cvalid4, jnp.exp2(sc - jnp.max(sc, axis=1, keepdims=True)), 0.0)
        pc = pc / jnp.maximum(jnp.sum(pc, axis=1, keepdims=True), TINY)
        o_cmp = _dot(pc.astype(BF16), vcmp_scr[:, head(hk)])
        psum = (pc[0:tq] + pc[tq:2 * tq]) + (pc[2 * tq:3 * tq] + pc[3 * tq:4 * tq])
        nsub = tq // LANES
        for u in range(nsub):
            tr_scr[hk * nsub + u] = psum[u * LANES:(u + 1) * LANES, :].T
        pooled = lambda t: ((tr_scr[t, pl.ds(0, n_sel, stride=n_selb), :] + tr_scr[t, pl.ds(1, n_sel, stride=n_selb), :])
                            + (tr_scr[t, pl.ds(2, n_sel, stride=n_selb), :] + tr_scr[t, pl.ds(3, n_sel, stride=n_selb), :]))
        imp = jnp.concatenate([pooled(hk * nsub + u) for u in range(nsub)], axis=1)
        score = jnp.where(blk <= cur, imp + forced * FORCE_BONUS, NEG_INF)
        rank = jnp.zeros((n_sel, tq), F32)
        for kk in range(n_sel):
            row = score[kk:kk + 1, :]
            earlier = jnp.where(blk > kk, 1.0, 0.0)
            rank = rank + jnp.where(row > score, 1.0, jnp.where(row == score, earlier, 0.0))
        dropped = jnp.where(rank < float(SEL_TOPK), 0.0, 1.0)
        pieces = []
        for u in range(nsub):
            tr_scr[hk * nsub + u, 0:n_sel, :] = dropped[:, u * LANES:(u + 1) * LANES]
            tr_scr[hk * nsub + u, n_sel:, :] = jnp.zeros((LANES - n_sel, LANES), F32)
            pieces.append(tr_scr[hk * nsub + u].T)
        notsel = jnp.concatenate(pieces, axis=0).astype(BF16)
        return q4, jnp.concatenate([q4, stack4(notsel)], axis=1), o_cmp

    pre = [prepare(hk) for hk in range(nkv)]

    def sel_step(j, carries, bias4):
        rows = pl.ds(pl.multiple_of(j * tk, tk), tk)
        out = []
        for hk in range(nkv):
            s2 = _dot_nt(pre[hk][1], ksaug_scr[rows, 2 * hk * HEAD_DIM:(2 * hk + 2) * HEAD_DIM]) * c2
            if bias4 is not None:
                s2 = s2 + bias4
            out.append(_online_step(carries[hk], s2, vs_ref[rows, head(hk)]))
        return tuple(out)

    n_full = q0 // tk
    carries = lax.fori_loop(0, n_full, lambda j, c: sel_step(j, c, None),
                            tuple(_online_init(rows4) for _ in range(nkv)))
    rk = lax.broadcasted_iota(jnp.int32, (tq, tk), 0)
    ck = lax.broadcasted_iota(jnp.int32, (tq, tk), 1)
    causal4 = stack4(jnp.where(n_full * tk + ck <= q0 + rk, 0.0, NEG_INF))
    carries = sel_step(n_full, carries, causal4)

    w0 = pl.multiple_of(jnp.maximum(q0 - WINDOW, 0), tq)
    rw = lax.broadcasted_iota(jnp.int32, (tq, WIN_KEYS), 0)
    cw = lax.broadcasted_iota(jnp.int32, (tq, WIN_KEYS), 1)
    dist = (q0 - w0) + rw - cw
    wbias4 = stack4(jnp.where(jnp.abs(2 * dist - (WINDOW - 1)) <= (WINDOW - 1), 0.0, NEG_INF))
    wrows = pl.ds(w0, WIN_KEYS)

    for hk in range(nkv):
        q4, _, o_cmp = pre[hk]
        _, l, acc = carries[hk]
        o_sel = acc / l
        s2 = _dot_nt(q4, kw_ref[wrows, head(hk)]) * c2 + wbias4
        _, lw, accw = _online_step(_online_init(rows4), s2, vw_ref[wrows, head(hk)])
        o_win = accw / lw
        gate = lambda c: jnp.concatenate(
            [_lane_pick(gl, ZS_GATE0 + 3 * (hk * grp + g) + c) for g in range(grp)], axis=0)
        o4 = gate(0) * o_cmp + gate(1) * o_sel + gate(2) * o_win
        for g in range(grp):
            o_ref[:, head(hk * grp + g)] = o4[g * tq:(g + 1) * tq, :].astype(o_ref.dtype)


def _nsa_attention(zq, zc, zs, posk, posv, w1k, w1v, w2k, w2v, emask, batch, seq):
    t = zq.shape[0]
    tq, nkv = NSA_TQ, NSA_KV_HEADS
    nq = seq // tq
    qw = NSA_HEADS * HEAD_DIM
    kvw = nkv * HEAD_DIM
    kv = lambda col: pl.BlockSpec((seq, kvw), lambda b, i: (b, col // nkv))
    const = lambda shape: pl.BlockSpec(shape, lambda b, i: (0,) * len(shape))
    return pl.pallas_call(
        _nsa_kernel,
        grid=(batch, nq),
        in_specs=[
            pl.BlockSpec((tq, qw), lambda b, i: (b * nq + i, ZQ_NQ // NSA_HEADS)),
            kv(ZQ_KS), kv(ZQ_KW), kv(ZQ_VS), kv(ZQ_VW),
            *[pl.BlockSpec((seq, HEAD_DIM), functools.partial(lambda b, i, h: (b, h), h=h)) for h in range(ZC_HEADS)],
            pl.BlockSpec((tq, LANES), lambda b, i: (b * nq + i, 0)),
            const((CMP_BLOCK, HEAD_DIM)), const((CMP_BLOCK, HEAD_DIM)),
            const((CMP_BLOCK * HEAD_DIM, HEAD_DIM)), const((CMP_BLOCK * HEAD_DIM, HEAD_DIM)),
            const((HEAD_DIM, HEAD_DIM)), const((HEAD_DIM, HEAD_DIM)),
            const((seq, LANES)),
        ],
        out_specs=pl.BlockSpec((tq, qw), lambda b, i: (b * nq + i, 0)),
        out_shape=jax.ShapeDtypeStruct((t, qw), BF16),
        scratch_shapes=[
            pltpu.VMEM((seq, 2 * kvw), BF16),
            pltpu.VMEM((LANES, kvw), BF16),
            pltpu.VMEM((LANES, kvw), BF16),
            pltpu.VMEM((LANES, CMP_BLOCK * HEAD_DIM), BF16),
            pltpu.VMEM((nkv * (tq // LANES), LANES, LANES), F32),
        ],
        compiler_params=_cparams(("arbitrary", "arbitrary")),
        name="nsa_attention",
    )(zq, zq, zq, zq, zq, zc, zc, zc, zc, zs, posk, posv, w1k, w1v, w2k, w2v, emask)


def _out_proj_kernel(of_ref, on_ref, wf_ref, wn_ref, x_ref, mod_ref, o_ref):
    n = o_ref.shape[1]
    for c0 in range(0, n, COL_CHUNK):
        cols = slice(c0, c0 + COL_CHUNK)
        acc = _dot(of_ref[...], wf_ref[:, cols]) + _dot(on_ref[...], wn_ref[:, cols])
        o_ref[:, cols] = x_ref[:, cols] + mod_ref[0, 2:3, cols] * acc


def _out_proj(o_fox, o_nsa, w_fox, w_nsa, x2d, mod, seq):
    t, d = x2d.shape
    tm = ROW_TILE
    tpb = seq // tm
    kf, kn = o_fox.shape[1], o_nsa.shape[1]
    return pl.pallas_call(
        _out_proj_kernel,
        grid=(t // tm,),
        in_specs=[
            pl.BlockSpec((tm, kf), lambda i: (i, 0)),
            pl.BlockSpec((tm, kn), lambda i: (i, 0)),
            pl.BlockSpec((kf, d), lambda i: (0, 0), pipeline_mode=pl.Buffered(1)),
            pl.BlockSpec((kn, d), lambda i: (0, 0), pipeline_mode=pl.Buffered(1)),
            pl.BlockSpec((tm, d), lambda i: (i, 0)),
            pl.BlockSpec((1, 6, d), lambda i: (i // tpb, 0, 0)),
        ],
        out_specs=pl.BlockSpec((tm, d), lambda i: (i, 0)),
        out_shape=jax.ShapeDtypeStruct((t, d), F32),
        compiler_params=_cparams(("arbitrary",)),
        name="out_proj",
    )(o_fox, o_nsa, w_fox, w_nsa, x2d, mod)


def _conv_rows(u, cw_ref, cb_ref):
    y = cb_ref[...] + cw_ref[0:1, :] * pltpu.roll(u, 2, axis=0)
    y = y + cw_ref[1:2, :] * pltpu.roll(u, 1, axis=0)
    return y + cw_ref[2:3, :] * u


def _ffn_kernel(x_ref, xh_ref, mod_ref, g_ref, wg_ref, wv_ref, cwg_ref, cwv_ref, cbg_ref, cbv_ref,
                wd_ref, fg_ref, o_ref, h_scr, acc_scr, *, tiles_per_batch, final_norm):
    i = pl.program_id(0)
    c = pl.program_id(1)
    tm = x_ref.shape[0]

    @pl.when(c == 0)
    def _():
        g, shift, scale = g_ref[...], mod_ref[0, 3:4, :], mod_ref[0, 4:5, :]
        halo = _norm_mod_rows(xh_ref[...], g, shift, scale)
        first = (i % tiles_per_batch) == 0
        h_scr[0:HALO, :] = jnp.where(first, 0.0, halo).astype(BF16)
        _norm_mod_to_scratch(x_ref, h_scr, HALO, tm, g, shift, scale)
        acc_scr[...] = jnp.zeros_like(acc_scr)

    h = h_scr[...]
    yg = _conv_rows(_dot(h, wg_ref[...]), cwg_ref, cbg_ref)[HALO:, :]
    yv = _conv_rows(_dot(h, wv_ref[...]), cwv_ref, cbv_ref)[HALO:, :]
    act = (yg * jax.nn.sigmoid(yg)) * yv
    acc_scr[...] += _dot(act.astype(BF16), wd_ref[...])

    @pl.when(c == pl.num_programs(1) - 1)
    def _():
        y = x_ref[...] + mod_ref[0, 5:6, :] * acc_scr[...]
        if final_norm:
            ms = jnp.mean(y * y, axis=-1, keepdims=True)
            y = (y * lax.rsqrt(ms + EPS)) * fg_ref[...]
        o_ref[...] = y


def _ffn(x2d, mod, g, w_up, conv_w, conv_b, w_down, final_g, seq, final_norm):
    t, d = x2d.shape
    dff = w_down.shape[0]
    tm, tf = ROW_TILE, FF_CHUNK
    tpb = seq // tm
    nc = dff // tf
    hb = tm // HALO
    kern = functools.partial(_ffn_kernel, tiles_per_batch=tpb, final_norm=final_norm)
    return pl.pallas_call(
        kern,
        grid=(t // tm, nc),
        in_specs=[
            pl.BlockSpec((tm, d), lambda i, c: (i, 0)),
            pl.BlockSpec((HALO, d), lambda i, c: (jnp.maximum(i * hb - 1, 0), 0)),
            pl.BlockSpec((1, 6, d), lambda i, c: (i // tpb, 0, 0)),
            pl.BlockSpec((1, d), lambda i, c: (0, 0)),
            pl.BlockSpec((d, tf), lambda i, c: (0, c)),
            pl.BlockSpec((d, tf), lambda i, c: (0, nc + c)),
            pl.BlockSpec((CONV_WIDTH, tf), lambda i, c: (0, c)),
            pl.BlockSpec((CONV_WIDTH, tf), lambda i, c: (0, nc + c)),
            pl.BlockSpec((1, tf), lambda i, c: (0, c)),
            pl.BlockSpec((1, tf), lambda i, c: (0, nc + c)),
            pl.BlockSpec((tf, d), lambda i, c: (c, 0)),
            pl.BlockSpec((1, d), lambda i, c: (0, 0)),
        ],
        out_specs=pl.BlockSpec((tm, d), lambda i, c: (i, 0)),
        out_shape=jax.ShapeDtypeStruct((t, d), F32),
        scratch_shapes=[pltpu.VMEM((HALO + tm, d), BF16), pltpu.VMEM((tm, d), F32)],
        compiler_params=_cparams(("arbitrary", "arbitrary")),
        name="conv_ffn",
    )(x2d, x2d, mod, g, w_up, w_up, conv_w, conv_w, conv_b, conv_b, w_down, final_g)


def _rope_tables(seq):
    inv = ROPE_THETA ** (-jnp.arange(0, HEAD_DIM, 2, dtype=F32) / HEAD_DIM)
    ang = jnp.arange(seq, dtype=F32)[:, None] * inv[None, :]
    cos, sin = jnp.cos(ang), jnp.sin(ang)
    return jnp.concatenate([cos, cos], axis=-1), jnp.concatenate([-sin, sin], axis=-1)


def _select_mask_table(seq):
    key_blk = (np.arange(seq) // SEL_BLOCK).reshape(seq, 1)
    return jnp.asarray(np.where(np.arange(LANES).reshape(1, LANES) == key_blk, NEG_INF, 0.0), dtype=BF16)


def _split_w_in(w):
    hd = HEAD_DIM
    sizes = [FOX_HEADS * hd] * 3 + [FOX_HEADS] + [NSA_HEADS * hd] + [NSA_KV_HEADS * hd] * 6 + [3 * NSA_HEADS]
    fq, fk, fv, ff, nq, kc, vc, ks, vs, kw, vw, ng = jnp.split(w, [int(o) for o in np.cumsum(sizes)[:-1]], axis=-1)
    pad = jnp.zeros((w.shape[0], LANES - FOX_HEADS - 3 * NSA_HEADS), w.dtype)
    return jnp.concatenate([fq, fk, fv, nq, ks, kw, vs, vw, kc, vc, ff, ng, pad], axis=-1).astype(BF16)


def kernel(x, c, attn_norm_g, ffn_norm_g, w_ada, b_ada, w_in, b_fgate, cmp_pos_k, cmp_pos_v,
           w_cmp1_k, w_cmp2_k, w_cmp1_v, w_cmp2_v, w_out, w_up, conv_w, conv_b, w_down, final_norm_g):
    batch, seq, d = x.shape
    depth = w_ada.shape[0]
    assert seq % ROW_TILE == 0 and seq % FOX_TQ == 0 and seq % NSA_TK == 0 and seq // SEL_BLOCK == LANES // 4
    assert seq >= WIN_KEYS and w_down.shape[1] % FF_CHUNK == 0 and d % COL_CHUNK == 0

    cosf, sinf = _rope_tables(seq)
    emask = _select_mask_table(seq)
    tri = jnp.asarray(np.tril(np.ones((LANES, LANES))), dtype=BF16)
    mod_all = _ada_mod(c, w_ada, b_ada)

    x2d = x.reshape(batch * seq, d)
    n_fox = FOX_HEADS * HEAD_DIM
    for l in range(depth):
        mod = mod_all[l].reshape(batch, 6, d)
        zq, zc, zs = _in_proj(x2d, mod, attn_norm_g[l].reshape(1, d), _split_w_in(w_in[l]), cosf, sinf, seq)
        bias_row = jnp.zeros((1, LANES), F32).at[0, :FOX_HEADS].set(b_fgate[l])
        cumc = _fgate(zs, bias_row, tri, batch, seq)
        o_fox = _fox_attention(zq, cumc, batch, seq)
        o_nsa = _nsa_attention(zq, zc, zs, cmp_pos_k[l], cmp_pos_v[l],
                               w_cmp1_k[l].astype(BF16), w_cmp1_v[l].astype(BF16),
                               w_cmp2_k[l].astype(BF16), w_cmp2_v[l].astype(BF16), emask, batch, seq)
        w_o = w_out[l].astype(BF16)
        x2d = _out_proj(o_fox, o_nsa, w_o[:n_fox], w_o[n_fox:], x2d, mod, seq)
        x2d = _ffn(x2d, mod, ffn_norm_g[l].reshape(1, d), w_up[l].astype(BF16), conv_w[l],
                   conv_b[l].reshape(1, -1), w_down[l].astype(BF16), final_norm_g.reshape(1, d), seq,
                   final_norm=(l == depth - 1))
    return x2d.reshape(batch, seq, d)
```

```python
import functools
import math

import numpy as np
import jax
import jax.numpy as jnp
from jax import lax
from jax.experimental import pallas as pl
from jax.experimental.pallas import tpu as pltpu

HEAD_DIM = 128
FOX_HEADS = 8
NSA_HEADS = 8
NSA_KV_HEADS = 2
NSA_GROUP = NSA_HEADS // NSA_KV_HEADS
CMP_BLOCK = 32
CMP_STRIDE = 16
SEL_BLOCK = 64
SEL_TOPK = 16
WINDOW = 512
CONV_WIDTH = 3
ROPE_THETA = 10000.0
FORCE_BONUS = 1000.0
NEG_INF = -1e30
TINY = 1e-30
EPS = 1e-6
LOG2E = math.log2(math.e)

LANES = 128
SUBLANES = 8
VMEM_LIMIT_BYTES = 56 * 1024 * 1024

ROW_TILE = 512
FFN_TILE = 1024
COL_CHUNK = 512
NORM_ROWS = 128
FOX_TQ = 512
FOX_GROUP = 2
NSA_TQ = 256
NSA_TK = 512
WIN_KEYS = WINDOW + NSA_TQ
HALO = 2 * SUBLANES
FF_CHUNK = 512

BF16 = jnp.bfloat16
F32 = jnp.float32

ZQ_FQ, ZQ_FK, ZQ_FV, ZQ_NQ, ZQ_KS, ZQ_KW, ZQ_VS, ZQ_VW = 0, 8, 16, 24, 32, 34, 36, 38
ZQ_HEADS = 40
ZQ_ROPED = tuple(range(ZQ_NQ, ZQ_KW + NSA_KV_HEADS))
ZC_HEADS = 4
ZS_GATE0 = FOX_HEADS


def _cparams(sem):
    return pltpu.CompilerParams(dimension_semantics=sem, vmem_limit_bytes=VMEM_LIMIT_BYTES)


def _dot(a, b):
    return jnp.dot(a, b, preferred_element_type=F32)


def _dot_nt(a, b):
    return lax.dot_general(a, b, (((1,), (1,)), ((), ())), preferred_element_type=F32)


def _ada_kernel(c_ref, w_ref, b_ref, o_ref):
    c = c_ref[...]
    ca = (c * jax.nn.sigmoid(c)).astype(BF16)
    o_ref[0] = _dot(ca, w_ref[0].astype(BF16)) + b_ref[0]


def _ada_mod(c, w_ada, b_ada):
    depth, d, n = w_ada.shape
    b = c.shape[0]
    tn = 1024
    return pl.pallas_call(
        _ada_kernel,
        grid=(depth, n // tn),
        in_specs=[
            pl.BlockSpec((b, d), lambda l, j: (0, 0)),
            pl.BlockSpec((1, d, tn), lambda l, j: (l, 0, j)),
            pl.BlockSpec((1, 1, tn), lambda l, j: (l, 0, j)),
        ],
        out_specs=pl.BlockSpec((1, b, tn), lambda l, j: (l, 0, j)),
        out_shape=jax.ShapeDtypeStruct((depth, b, n), F32),
        compiler_params=_cparams(("arbitrary", "arbitrary")),
        name="ada_mod",
    )(c, w_ada, b_ada.reshape(depth, 1, n))


def _norm_mod_rows(x, g, shift, scale):
    ms = jnp.mean(x * x, axis=-1, keepdims=True)
    y = (x * lax.rsqrt(ms + EPS)) * g
    return y * (1.0 + scale) + shift


def _norm_mod_to_scratch(x_ref, h_scr, row0, n_rows, g, shift, scale):
    def body(r, _):
        rows = pl.ds(pl.multiple_of(r * NORM_ROWS, NORM_ROWS), NORM_ROWS)
        dst = pl.ds(pl.multiple_of(row0 + r * NORM_ROWS, SUBLANES), NORM_ROWS)
        h_scr[dst, :] = _norm_mod_rows(x_ref[rows, :], g, shift, scale).astype(BF16)
        return 0
    lax.fori_loop(0, n_rows // NORM_ROWS, body, 0)


def _rope_tile(t, cosf, sinf):
    return t * cosf + pltpu.roll(t, HEAD_DIM // 2, axis=1) * sinf


def _in_proj_kernel(x_ref, mod_ref, g_ref, w_ref, cos_ref, sin_ref, zq_ref, zc_ref, zs_ref, h_scr):
    tm = x_ref.shape[0]
    _norm_mod_to_scratch(x_ref, h_scr, 0, tm, g_ref[...], mod_ref[0, 0:1, :], mod_ref[0, 1:2, :])
    heads_per_chunk = COL_CHUNK // HEAD_DIM
    for c0 in range(0, ZQ_HEADS, heads_per_chunk):
        acc = _dot(h_scr[...], w_ref[:, c0 * HEAD_DIM:(c0 + heads_per_chunk) * HEAD_DIM])
        for hh in range(heads_per_chunk):
            t = acc[:, hh * HEAD_DIM:(hh + 1) * HEAD_DIM]
            if c0 + hh in ZQ_ROPED:
                t = _rope_tile(t, cos_ref[...], sin_ref[...])
            zq_ref[:, (c0 + hh) * HEAD_DIM:(c0 + hh + 1) * HEAD_DIM] = t.astype(BF16)
    base = ZQ_HEADS * HEAD_DIM
    acc = _dot(h_scr[...], w_ref[:, base:base + ZC_HEADS * HEAD_DIM])
    for hh in range(ZC_HEADS):
        t = acc[:, hh * HEAD_DIM:(hh + 1) * HEAD_DIM]
        if hh < NSA_KV_HEADS:
            t = _rope_tile(t, cos_ref[...], sin_ref[...])
        zc_ref[:, hh * HEAD_DIM:(hh + 1) * HEAD_DIM] = t
    base += ZC_HEADS * HEAD_DIM
    zs_ref[...] = _dot(h_scr[...], w_ref[:, base:base + LANES])


def _in_proj(x2d, mod, g, wcat, cosf, sinf, seq):
    t, d = x2d.shape
    tm = ROW_TILE
    tpb = seq // tm
    nw = wcat.shape[1]
    return pl.pallas_call(
        _in_proj_kernel,
        grid=(t // tm,),
        in_specs=[
            pl.BlockSpec((tm, d), lambda i: (i, 0)),
            pl.BlockSpec((1, 6, d), lambda i: (i // tpb, 0, 0)),
            pl.BlockSpec((1, d), lambda i: (0, 0)),
            pl.BlockSpec((d, nw), lambda i: (0, 0), pipeline_mode=pl.Buffered(1)),
            pl.BlockSpec((tm, HEAD_DIM), lambda i: (i % tpb, 0)),
            pl.BlockSpec((tm, HEAD_DIM), lambda i: (i % tpb, 0)),
        ],
        out_specs=[
            pl.BlockSpec((tm, ZQ_HEADS * HEAD_DIM), lambda i: (i, 0)),
            pl.BlockSpec((tm, ZC_HEADS * HEAD_DIM), lambda i: (i, 0)),
            pl.BlockSpec((tm, LANES), lambda i: (i, 0)),
        ],
        out_shape=[
            jax.ShapeDtypeStruct((t, ZQ_HEADS * HEAD_DIM), BF16),
            jax.ShapeDtypeStruct((t, ZC_HEADS * HEAD_DIM), F32),
            jax.ShapeDtypeStruct((t, LANES), F32),
        ],
        scratch_shapes=[pltpu.VMEM((tm, d), BF16)],
        compiler_params=_cparams(("arbitrary",)),
        name="in_proj",
    )(x2d, mod, g, wcat, cosf, sinf)


def _split3(v):
    hi = v.astype(BF16)
    r1 = v - hi.astype(F32)
    mid = r1.astype(BF16)
    lo = (r1 - mid.astype(F32)).astype(BF16)
    return hi, mid, lo


def _fgate_kernel(zs_ref, bias_ref, tri_ref, cumc_ref):
    seq = zs_ref.shape[0]
    tri = tri_ref[...]
    carry = jnp.zeros((1, LANES), F32)
    for c in range(seq // LANES):
        zf = zs_ref[c * LANES:(c + 1) * LANES, :] + bias_ref[...]
        lf = jnp.minimum(zf, 0.0) - jnp.log1p(jnp.exp(-jnp.abs(zf)))
        hi, mid, lo = _split3(lf)
        cs = (_dot(tri, lo) + _dot(tri, mid)) + _dot(tri, hi) + carry
        carry = cs[LANES - 1:LANES, :]
        cumc_ref[c * LANES:(c + 1) * LANES, :] = cs


def _fgate(zs, bias_row, tri, batch, seq):
    t = zs.shape[0]
    return pl.pallas_call(
        _fgate_kernel,
        grid=(batch,),
        in_specs=[
            pl.BlockSpec((seq, LANES), lambda b: (b, 0)),
            pl.BlockSpec((1, LANES), lambda b: (0, 0)),
            pl.BlockSpec((LANES, LANES), lambda b: (0, 0)),
        ],
        out_specs=pl.BlockSpec((seq, LANES), lambda b: (b, 0)),
        out_shape=jax.ShapeDtypeStruct((t, LANES), F32),
        compiler_params=_cparams(("arbitrary",)),
        name="fgate_cumsum",
    )(zs, bias_row, tri)


def _lane_pick(tile, lane_idx):
    lane = lax.broadcasted_iota(jnp.int32, tile.shape, 1)
    return jnp.sum(jnp.where(lane == lane_idx, tile, 0.0), axis=1, keepdims=True)


def _online_step(carry, s2, v):
    m, l, acc = carry
    m_new = jnp.maximum(m, jnp.max(s2, axis=1, keepdims=True))
    alpha = jnp.exp2(m - m_new)
    p = jnp.exp2(s2 - m_new)
    l = alpha * l + jnp.sum(p, axis=1, keepdims=True)
    acc = alpha * acc + _dot(p.astype(BF16), v)
    return m_new, l, acc


def _online_init(rows):
    return (jnp.full((rows, 1), NEG_INF, F32), jnp.zeros((rows, 1), F32), jnp.zeros((rows, HEAD_DIM), F32))


def _placed_pieces(pieces, head, first_lane, sign, const_lanes):
    row = lax.broadcasted_iota(jnp.int32, (LANES, LANES), 0)
    col = lax.broadcasted_iota(jnp.int32, (LANES, LANES), 1)
    out = None
    for p, piece in enumerate(pieces):
        place = jnp.where(row == head, jnp.where(col == first_lane + p, sign, 0.0), 0.0).astype(BF16)
        term = _dot(piece, place)
        out = term if out is None else out + term
    lane = lax.broadcasted_iota(jnp.int32, (1, LANES), 1)
    ones = jnp.where(lane >= const_lanes[0], jnp.where(lane < const_lanes[1], 1.0, 0.0), 0.0)
    return (out + ones).astype(BF16)


def _fox_kernel(q_ref, k_ref, v_ref, cum_ref, o_ref, qaug_scr, kaug_scr):
    hb = pl.program_id(1)
    tile, grp = FOX_TQ, FOX_GROUP
    seq = q_ref.shape[0]
    scale = 1.0 / math.sqrt(HEAD_DIM)
    c2 = scale * LOG2E
    npc = 3
    head = lambda h: slice(h * HEAD_DIM, (h + 1) * HEAD_DIM)

    pieces = _split3(cum_ref[...] * (1.0 / scale))
    for g in range(grp):
        qaug_scr[:, head(2 * g)] = q_ref[:, head(g)]
        qaug_scr[:, head(2 * g + 1)] = _placed_pieces(pieces, hb * grp + g, 0, 1.0, (npc, 2 * npc))
        kaug_scr[:, head(2 * g)] = k_ref[:, head(g)]
        kaug_scr[:, head(2 * g + 1)] = _placed_pieces(pieces, hb * grp + g, npc, -1.0, (0, npc))

    causal = (lax.broadcasted_iota(jnp.int32, (tile, tile), 1) <= lax.broadcasted_iota(jnp.int32, (tile, tile), 0))
    for i in range(seq // tile):
        qrows = slice(i * tile, (i + 1) * tile)
        for g in range(grp):
            carry = _online_init(tile)
            for j in range(i + 1):
                krows = slice(j * tile, (j + 1) * tile)
                s2 = _dot_nt(qaug_scr[qrows, 2 * g * HEAD_DIM:(2 * g + 2) * HEAD_DIM],
                             kaug_scr[krows, 2 * g * HEAD_DIM:(2 * g + 2) * HEAD_DIM]) * c2
                if j == i:
                    s2 = jnp.where(causal, s2, NEG_INF)
                carry = _online_step(carry, s2, v_ref[krows, head(g)])
            _, l, acc = carry
            o_ref[qrows, head(g)] = (acc / l).astype(o_ref.dtype)


def _fox_attention(zq, cumc, batch, seq):
    t = zq.shape[0]
    grp = FOX_GROUP
    gw = grp * HEAD_DIM
    return pl.pallas_call(
        _fox_kernel,
        grid=(batch, FOX_HEADS // grp),
        in_specs=[
            pl.BlockSpec((seq, gw), lambda b, h: (b, ZQ_FQ // grp + h)),
            pl.BlockSpec((seq, gw), lambda b, h: (b, ZQ_FK // grp + h)),
            pl.BlockSpec((seq, gw), lambda b, h: (b, ZQ_FV // grp + h)),
            pl.BlockSpec((seq, LANES), lambda b, h: (b, 0)),
        ],
        out_specs=pl.BlockSpec((seq, gw), lambda b, h: (b, h)),
        out_shape=jax.ShapeDtypeStruct((t, FOX_HEADS * HEAD_DIM), BF16),
        scratch_shapes=[pltpu.VMEM((seq, 2 * gw), BF16), pltpu.VMEM((seq, 2 * gw), BF16)],
        compiler_params=_cparams(("arbitrary", "arbitrary")),
        name="fox_attention",
    )(zq, zq, zq, cumc)


def _compress_to(src_ref, hk, pos_ref, w1_ref, w2_ref, flat_scr, dst_scr):
    half = CMP_BLOCK // 2
    n_rows = dst_scr.shape[0]
    cols = slice(hk * HEAD_DIM, (hk + 1) * HEAD_DIM)
    for jp in range(half):
        rows = src_ref[pl.ds(jp, n_rows, stride=CMP_STRIDE), :]
        flat_scr[:, jp * HEAD_DIM:(jp + 1) * HEAD_DIM] = (rows + pos_ref[jp:jp + 1, :]).astype(BF16)
        nxt = pltpu.roll(rows, n_rows - 1, axis=0)
        flat_scr[:, (half + jp) * HEAD_DIM:(half + jp + 1) * HEAD_DIM] = (
            nxt + pos_ref[half + jp:half + jp + 1, :]).astype(BF16)
    pre = _dot(flat_scr[...], w1_ref[...])
    dst_scr[:, cols] = _dot(jax.nn.gelu(pre).astype(BF16), w2_ref[...]).astype(BF16)


def _nsa_kernel(q_ref, ks_ref, kw_ref, vs_ref, vw_ref, kcf0_ref, kcf1_ref, vcf0_ref, vcf1_ref, zs_ref,
                posk_ref, posv_ref, w1k_ref, w1v_ref, w2k_ref, w2v_ref, emask_ref,
                o_ref, ksaug_scr, kcmp_scr, vcmp_scr, flat_scr, tr_scr):
    i = pl.program_id(1)
    tq, tk, grp, nkv = NSA_TQ, NSA_TK, NSA_GROUP, NSA_KV_HEADS
    rows4 = grp * tq
    n_selb = SEL_BLOCK // CMP_STRIDE
    n_sel = LANES // n_selb
    c2 = LOG2E / math.sqrt(HEAD_DIM)
    q0 = i * tq
    head = lambda h: slice(h * HEAD_DIM, (h + 1) * HEAD_DIM)

    @pl.when(i == 0)
    def _():
        for hk in range(nkv):
            ksaug_scr[:, head(2 * hk)] = ks_ref[:, head(hk)]
            ksaug_scr[:, head(2 * hk + 1)] = emask_ref[...]
            _compress_to((kcf0_ref, kcf1_ref)[hk], hk, posk_ref, w1k_ref, w2k_ref, flat_scr, kcmp_scr)
            _compress_to((vcf0_ref, vcf1_ref)[hk], hk, posv_ref, w1v_ref, w2v_ref, flat_scr, vcmp_scr)

    def stack4(tile):
        return jnp.concatenate([tile] * grp, axis=0)

    rloc = lax.broadcasted_iota(jnp.int32, (tq, LANES), 0)
    lane = lax.broadcasted_iota(jnp.int32, (tq, LANES), 1)
    cvalid4 = stack4(jnp.where((lane * CMP_STRIDE + (CMP_BLOCK - 1)) <= q0 + rloc, 1.0, 0.0)) > 0.5
    blk = lax.broadcasted_iota(jnp.int32, (n_sel, tq), 0)
    cur = lax.shift_right_logical(q0 + lax.broadcasted_iota(jnp.int32, (n_sel, tq), 1), int(math.log2(SEL_BLOCK)))
    forced = jnp.where(blk == 0, 1.0, jnp.where(blk == cur, 1.0, jnp.where(blk == cur - 1, 1.0, 0.0)))
    gl = jax.nn.sigmoid(zs_ref[...])

    def prepare(hk):
        q4 = jnp.concatenate([q_ref[:, head(hk * grp + g)] for g in range(grp)], axis=0)
        sc = jnp.where(cvalid4, _dot_nt(q4, kcmp_scr[:, head(hk)]) * c2, NEG_INF)
        pc = jnp.where(cvalid4, jnp.exp2(sc - jnp.max(sc, axis=1, keepdims=True)), 0.0)
        pc = pc / jnp.maximum(jnp.sum(pc, axis=1, keepdims=True), TINY)
        o_cmp = _dot(pc.astype(BF16), vcmp_scr[:, head(hk)])
        psum = (pc[0:tq] + pc[tq:2 * tq]) + (pc[2 * tq:3 * tq] + pc[3 * tq:4 * tq])
        nsub = tq // LANES
        for u in range(nsub):
            tr_scr[hk * nsub + u] = psum[u * LANES:(u + 1) * LANES, :].T
        pooled = lambda t: ((tr_scr[t, pl.ds(0, n_sel, stride=n_selb), :] + tr_scr[t, pl.ds(1, n_sel, stride=n_selb), :])
                            + (tr_scr[t, pl.ds(2, n_sel, stride=n_selb), :] + tr_scr[t, pl.ds(3, n_sel, stride=n_selb), :]))
        imp = jnp.concatenate([pooled(hk * nsub + u) for u in range(nsub)], axis=1)
        score = jnp.where(blk <= cur, imp + forced * FORCE_BONUS, NEG_INF)
        rank = jnp.zeros((n_sel, tq), F32)
        for kk in range(n_sel):
            row = score[kk:kk + 1, :]
            earlier = jnp.where(blk > kk, 1.0, 0.0)
            rank = rank + jnp.where(row > score, 1.0, jnp.where(row == score, earlier, 0.0))
        dropped = jnp.where(rank < float(SEL_TOPK), 0.0, 1.0)
        pieces = []
        for u in range(nsub):
            tr_scr[hk * nsub + u, 0:n_sel, :] = dropped[:, u * LANES:(u + 1) * LANES]
            tr_scr[hk * nsub + u, n_sel:, :] = jnp.zeros((LANES - n_sel, LANES), F32)
            pieces.append(tr_scr[hk * nsub + u].T)
        notsel = jnp.concatenate(pieces, axis=0).astype(BF16)
        return q4, jnp.concatenate([q4, stack4(notsel)], axis=1), o_cmp

    pre = [prepare(hk) for hk in range(nkv)]

    def sel_step(j, carries, bias4):
        rows = pl.ds(pl.multiple_of(j * tk, tk), tk)
        out = []
        for hk in range(nkv):
            s2 = _dot_nt(pre[hk][1], ksaug_scr[rows, 2 * hk * HEAD_DIM:(2 * hk + 2) * HEAD_DIM]) * c2
            if bias4 is not None:
                s2 = s2 + bias4
            out.append(_online_step(carries[hk], s2, vs_ref[rows, head(hk)]))
        return tuple(out)

    n_full = q0 // tk
    carries = lax.fori_loop(0, n_full, lambda j, c: sel_step(j, c, None),
                            tuple(_online_init(rows4) for _ in range(nkv)))
    rk = lax.broadcasted_iota(jnp.int32, (tq, tk), 0)
    ck = lax.broadcasted_iota(jnp.int32, (tq, tk), 1)
    causal4 = stack4(jnp.where(n_full * tk + ck <= q0 + rk, 0.0, NEG_INF))
    carries = sel_step(n_full, carries, causal4)

    w0 = pl.multiple_of(jnp.maximum(q0 - WINDOW, 0), tq)
    rw = lax.broadcasted_iota(jnp.int32, (tq, WIN_KEYS), 0)
    cw = lax.broadcasted_iota(jnp.int32, (tq, WIN_KEYS), 1)
    dist = (q0 - w0) + rw - cw
    wbias4 = stack4(jnp.where(jnp.abs(2 * dist - (WINDOW - 1)) <= (WINDOW - 1), 0.0, NEG_INF))
    wrows = pl.ds(w0, WIN_KEYS)

    for hk in range(nkv):
        q4, _, o_cmp = pre[hk]
        _, l, acc = carries[hk]
        o_sel = acc / l
        s2 = _dot_nt(q4, kw_ref[wrows, head(hk)]) * c2 + wbias4
        _, lw, accw = _online_step(_online_init(rows4), s2, vw_ref[wrows, head(hk)])
        o_win = accw / lw
        gate = lambda c: jnp.concatenate(
            [_lane_pick(gl, ZS_GATE0 + 3 * (hk * grp + g) + c) for g in range(grp)], axis=0)
        o4 = gate(0) * o_cmp + gate(1) * o_sel + gate(2) * o_win
        for g in range(grp):
            o_ref[:, head(hk * grp + g)] = o4[g * tq:(g + 1) * tq, :].astype(o_ref.dtype)


def _nsa_attention(zq, zc, zs, posk, posv, w1k, w1v, w2k, w2v, emask, batch, seq):
    t = zq.shape[0]
    tq, nkv = NSA_TQ, NSA_KV_HEADS
    nq = seq // tq
    qw = NSA_HEADS * HEAD_DIM
    kvw = nkv * HEAD_DIM
    kv = lambda col: pl.BlockSpec((seq, kvw), lambda b, i: (b, col // nkv))
    const = lambda shape: pl.BlockSpec(shape, lambda b, i: (0,) * len(shape))
    return pl.pallas_call(
        _nsa_kernel,
        grid=(batch, nq),
        in_specs=[
            pl.BlockSpec((tq, qw), lambda b, i: (b * nq + i, ZQ_NQ // NSA_HEADS)),
            kv(ZQ_KS), kv(ZQ_KW), kv(ZQ_VS), kv(ZQ_VW),
            *[pl.BlockSpec((seq, HEAD_DIM), functools.partial(lambda b, i, h: (b, h), h=h)) for h in range(ZC_HEADS)],
            pl.BlockSpec((tq, LANES), lambda b, i: (b * nq + i, 0)),
            const((CMP_BLOCK, HEAD_DIM)), const((CMP_BLOCK, HEAD_DIM)),
            const((CMP_BLOCK * HEAD_DIM, HEAD_DIM)), const((CMP_BLOCK * HEAD_DIM, HEAD_DIM)),
            const((HEAD_DIM, HEAD_DIM)), const((HEAD_DIM, HEAD_DIM)),
            const((seq, LANES)),
        ],
        out_specs=pl.BlockSpec((tq, qw), lambda b, i: (b * nq + i, 0)),
        out_shape=jax.ShapeDtypeStruct((t, qw), BF16),
        scratch_shapes=[
            pltpu.VMEM((seq, 2 * kvw), BF16),
            pltpu.VMEM((LANES, kvw), BF16),
            pltpu.VMEM((LANES, kvw), BF16),
            pltpu.VMEM((LANES, CMP_BLOCK * HEAD_DIM), BF16),
            pltpu.VMEM((nkv * (tq // LANES), LANES, LANES), F32),
        ],
        compiler_params=_cparams(("arbitrary", "arbitrary")),
        name="nsa_attention",
    )(zq, zq, zq, zq, zq, zc, zc, zc, zc, zs, posk, posv, w1k, w1v, w2k, w2v, emask)


def _out_proj_kernel(of_ref, on_ref, wf_ref, wn_ref, x_ref, mod_ref, o_ref):
    n = o_ref.shape[1]
    for c0 in range(0, n, COL_CHUNK):
        cols = slice(c0, c0 + COL_CHUNK)
        acc = _dot(of_ref[...], wf_ref[:, cols]) + _dot(on_ref[...], wn_ref[:, cols])
        o_ref[:, cols] = x_ref[:, cols] + mod_ref[0, 2:3, cols] * acc


def _out_proj(o_fox, o_nsa, w_fox, w_nsa, x2d, mod, seq):
    t, d = x2d.shape
    tm = ROW_TILE
    tpb = seq // tm
    kf, kn = o_fox.shape[1], o_nsa.shape[1]
    return pl.pallas_call(
        _out_proj_kernel,
        grid=(t // tm,),
        in_specs=[
            pl.BlockSpec((tm, kf), lambda i: (i, 0)),
            pl.BlockSpec((tm, kn), lambda i: (i, 0)),
            pl.BlockSpec((kf, d), lambda i: (0, 0), pipeline_mode=pl.Buffered(1)),
            pl.BlockSpec((kn, d), lambda i: (0, 0), pipeline_mode=pl.Buffered(1)),
            pl.BlockSpec((tm, d), lambda i: (i, 0)),
            pl.BlockSpec((1, 6, d), lambda i: (i // tpb, 0, 0)),
        ],
        out_specs=pl.BlockSpec((tm, d), lambda i: (i, 0)),
        out_shape=jax.ShapeDtypeStruct((t, d), F32),
        compiler_params=_cparams(("arbitrary",)),
        name="out_proj",
    )(o_fox, o_nsa, w_fox, w_nsa, x2d, mod)


def _conv_rows(u, cw_ref, cb_ref):
    y = cb_ref[...] + cw_ref[0:1, :] * pltpu.roll(u, 2, axis=0)
    y = y + cw_ref[1:2, :] * pltpu.roll(u, 1, axis=0)
    return y + cw_ref[2:3, :] * u


def _ffn_kernel(x_ref, xh_ref, mod_ref, g_ref, wg_ref, wv_ref, cwg_ref, cwv_ref, cbg_ref, cbv_ref,
                wd_ref, fg_ref, o_ref, h_scr, acc_scr, *, tiles_per_batch, final_norm):
    i = pl.program_id(0)
    c = pl.program_id(1)
    tm = x_ref.shape[0]

    @pl.when(c == 0)
    def _():
        g, shift, scale = g_ref[...], mod_ref[0, 3:4, :], mod_ref[0, 4:5, :]
        halo = _norm_mod_rows(xh_ref[...], g, shift, scale)
        first = (i % tiles_per_batch) == 0
        h_scr[0:HALO, :] = jnp.where(first, 0.0, halo).astype(BF16)
        _norm_mod_to_scratch(x_ref, h_scr, HALO, tm, g, shift, scale)
        acc_scr[...] = jnp.zeros_like(acc_scr)

    h = h_scr[...]
    yg = _conv_rows(_dot(h, wg_ref[...]), cwg_ref, cbg_ref)[HALO:, :]
    yv = _conv_rows(_dot(h, wv_ref[...]), cwv_ref, cbv_ref)[HALO:, :]
    act = (yg * jax.nn.sigmoid(yg)) * yv
    acc_scr[...] += _dot(act.astype(BF16), wd_ref[...])

    @pl.when(c == pl.num_programs(1) - 1)
    def _():
        def body(r, _):
            rows = pl.ds(pl.multiple_of(r * NORM_ROWS, NORM_ROWS), NORM_ROWS)
            y = x_ref[rows, :] + mod_ref[0, 5:6, :] * acc_scr[rows, :]
            if final_norm:
                ms = jnp.mean(y * y, axis=-1, keepdims=True)
                y = (y * lax.rsqrt(ms + EPS)) * fg_ref[...]
            o_ref[rows, :] = y
            return 0
        lax.fori_loop(0, tm // NORM_ROWS, body, 0)


def _ffn(x2d, mod, g, w_up, conv_w, conv_b, w_down, final_g, seq, final_norm):
    t, d = x2d.shape
    dff = w_down.shape[0]
    tm, tf = FFN_TILE, FF_CHUNK
    tpb = seq // tm
    nc = dff // tf
    hb = tm // HALO
    kern = functools.partial(_ffn_kernel, tiles_per_batch=tpb, final_norm=final_norm)
    return pl.pallas_call(
        kern,
        grid=(t // tm, nc),
        in_specs=[
            pl.BlockSpec((tm, d), lambda i, c: (i, 0), pipeline_mode=pl.Buffered(1)),
            pl.BlockSpec((HALO, d), lambda i, c: (jnp.maximum(i * hb - 1, 0), 0)),
            pl.BlockSpec((1, 6, d), lambda i, c: (i // tpb, 0, 0)),
            pl.BlockSpec((1, d), lambda i, c: (0, 0)),
            pl.BlockSpec((d, tf), lambda i, c: (0, c)),
            pl.BlockSpec((d, tf), lambda i, c: (0, nc + c)),
            pl.BlockSpec((CONV_WIDTH, tf), lambda i, c: (0, c)),
            pl.BlockSpec((CONV_WIDTH, tf), lambda i, c: (0, nc + c)),
            pl.BlockSpec((1, tf), lambda i, c: (0, c)),
            pl.BlockSpec((1, tf), lambda i, c: (0, nc + c)),
            pl.BlockSpec((tf, d), lambda i, c: (c, 0)),
            pl.BlockSpec((1, d), lambda i, c: (0, 0)),
        ],
        out_specs=pl.BlockSpec((tm, d), lambda i, c: (i, 0), pipeline_mode=pl.Buffered(1)),
        out_shape=jax.ShapeDtypeStruct((t, d), F32),
        scratch_shapes=[pltpu.VMEM((HALO + tm, d), BF16), pltpu.VMEM((tm, d), F32)],
        compiler_params=_cparams(("arbitrary", "arbitrary")),
        name="conv_ffn",
    )(x2d, x2d, mod, g, w_up, w_up, conv_w, conv_w, conv_b, conv_b, w_down, final_g)


def _rope_tables(seq):
    inv = ROPE_THETA ** (-jnp.arange(0, HEAD_DIM, 2, dtype=F32) / HEAD_DIM)
    ang = jnp.arange(seq, dtype=F32)[:, None] * inv[None, :]
    cos, sin = jnp.cos(ang), jnp.sin(ang)
    return jnp.concatenate([cos, cos], axis=-1), jnp.concatenate([-sin, sin], axis=-1)


def _select_mask_table(seq):
    key_blk = (np.arange(seq) // SEL_BLOCK).reshape(seq, 1)
    return jnp.asarray(np.where(np.arange(LANES).reshape(1, LANES) == key_blk, NEG_INF, 0.0), dtype=BF16)


def _split_w_in(w):
    hd = HEAD_DIM
    sizes = [FOX_HEADS * hd] * 3 + [FOX_HEADS] + [NSA_HEADS * hd] + [NSA_KV_HEADS * hd] * 6 + [3 * NSA_HEADS]
    w = w.astype(BF16)
    fq, fk, fv, ff, nq, kc, vc, ks, vs, kw, vw, ng = jnp.split(w, [int(o) for o in np.cumsum(sizes)[:-1]], axis=-1)
    pad = jnp.zeros((w.shape[0], LANES - FOX_HEADS - 3 * NSA_HEADS), w.dtype)
    return jnp.concatenate([fq, fk, fv, nq, ks, kw, vs, vw, kc, vc, ff, ng, pad], axis=-1)


def kernel(x, c, attn_norm_g, ffn_norm_g, w_ada, b_ada, w_in, b_fgate, cmp_pos_k, cmp_pos_v,
           w_cmp1_k, w_cmp2_k, w_cmp1_v, w_cmp2_v, w_out, w_up, conv_w, conv_b, w_down, final_norm_g):
    batch, seq, d = x.shape
    depth = w_ada.shape[0]
    assert seq % ROW_TILE == 0 and seq % FFN_TILE == 0 and seq % FOX_TQ == 0 and seq % NSA_TK == 0 and seq // SEL_BLOCK == LANES // 4
    assert seq >= WIN_KEYS and w_down.shape[1] % FF_CHUNK == 0 and d % COL_CHUNK == 0

    cosf, sinf = _rope_tables(seq)
    emask = _select_mask_table(seq)
    tri = jnp.asarray(np.tril(np.ones((LANES, LANES))), dtype=BF16)
    mod_all = _ada_mod(c, w_ada, b_ada)

    x2d = x.reshape(batch * seq, d)
    n_fox = FOX_HEADS * HEAD_DIM
    for l in range(depth):
        mod = mod_all[l].reshape(batch, 6, d)
        zq, zc, zs = _in_proj(x2d, mod, attn_norm_g[l].reshape(1, d), _split_w_in(w_in[l]), cosf, sinf, seq)
        bias_row = jnp.zeros((1, LANES), F32).at[0, :FOX_HEADS].set(b_fgate[l])
        cumc = _fgate(zs, bias_row, tri, batch, seq)
        o_fox = _fox_attention(zq, cumc, batch, seq)
        o_nsa = _nsa_attention(zq, zc, zs, cmp_pos_k[l], cmp_pos_v[l],
                               w_cmp1_k[l].astype(BF16), w_cmp1_v[l].astype(BF16),
                               w_cmp2_k[l].astype(BF16), w_cmp2_v[l].astype(BF16), emask, batch, seq)
        w_o = w_out[l].astype(BF16)
        x2d = _out_proj(o_fox, o_nsa, w_o[:n_fox], w_o[n_fox:], x2d, mod, seq)
        x2d = _ffn(x2d, mod, ffn_norm_g[l].reshape(1, d), w_up[l].astype(BF16), conv_w[l],
                   conv_b[l].reshape(1, -1), w_down[l].astype(BF16), final_norm_g.reshape(1, d), seq,
                   final_norm=(l == depth - 1))
    return x2d.reshape(batch, seq, d)
```

```python
import functools
import math

import numpy as np
import jax
import jax.numpy as jnp
from jax import lax
from jax.experimental import pallas as pl
from jax.experimental.pallas import tpu as pltpu

HEAD_DIM = 128
FOX_HEADS = 8
NSA_HEADS = 8
NSA_KV_HEADS = 2
NSA_GROUP = NSA_HEADS // NSA_KV_HEADS
CMP_BLOCK = 32
CMP_STRIDE = 16
SEL_BLOCK = 64
SEL_TOPK = 16
WINDOW = 512
CONV_WIDTH = 3
ROPE_THETA = 10000.0
FORCE_BONUS = 1000.0
NEG_INF = -1e30
TINY = 1e-30
EPS = 1e-6
LOG2E = math.log2(math.e)

LANES = 128
SUBLANES = 8
VMEM_LIMIT_BYTES = 56 * 1024 * 1024

ROW_TILE = 512
FFN_TILE = 1024
COL_CHUNK = 512
NORM_ROWS = 128
FOX_TQ = 512
FOX_GROUP = 2
NSA_TQ = 256
NSA_TK = 512
WIN_KEYS = WINDOW + NSA_TQ
HALO = 2 * SUBLANES
FF_CHUNK = 512

BF16 = jnp.bfloat16
F32 = jnp.float32

ZQ_FQ, ZQ_FK, ZQ_FV, ZQ_NQ, ZQ_KS, ZQ_KW, ZQ_VS, ZQ_VW = 0, 8, 16, 24, 32, 34, 36, 38
ZQ_HEADS = 40
ZQ_ROPED = tuple(range(ZQ_NQ, ZQ_KW + NSA_KV_HEADS))
ZC_HEADS = 4
ZS_GATE0 = FOX_HEADS


def _cparams(sem):
    return pltpu.CompilerParams(dimension_semantics=sem, vmem_limit_bytes=VMEM_LIMIT_BYTES)


def _dot(a, b):
    return jnp.dot(a, b, preferred_element_type=F32)


def _dot_nt(a, b):
    return lax.dot_general(a, b, (((1,), (1,)), ((), ())), preferred_element_type=F32)


def _ada_kernel(c_ref, w_ref, b_ref, o_ref):
    c = c_ref[...]
    ca = (c * jax.nn.sigmoid(c)).astype(BF16)
    o_ref[0] = _dot(ca, w_ref[0].astype(BF16)) + b_ref[0]


def _ada_mod(c, w_ada, b_ada):
    depth, d, n = w_ada.shape
    b = c.shape[0]
    tn = 1024
    return pl.pallas_call(
        _ada_kernel,
        grid=(depth, n // tn),
        in_specs=[
            pl.BlockSpec((b, d), lambda l, j: (0, 0)),
            pl.BlockSpec((1, d, tn), lambda l, j: (l, 0, j)),
            pl.BlockSpec((1, 1, tn), lambda l, j: (l, 0, j)),
        ],
        out_specs=pl.BlockSpec((1, b, tn), lambda l, j: (l, 0, j)),
        out_shape=jax.ShapeDtypeStruct((depth, b, n), F32),
        compiler_params=_cparams(("arbitrary", "arbitrary")),
        name="ada_mod",
    )(c, w_ada, b_ada.reshape(depth, 1, n))


def _norm_mod_rows(x, g, shift, scale):
    ms = jnp.mean(x * x, axis=-1, keepdims=True)
    y = (x * lax.rsqrt(ms + EPS)) * g
    return y * (1.0 + scale) + shift


def _norm_mod_to_scratch(x_ref, h_scr, row0, n_rows, g, shift, scale):
    def body(r, _):
        rows = pl.ds(pl.multiple_of(r * NORM_ROWS, NORM_ROWS), NORM_ROWS)
        dst = pl.ds(pl.multiple_of(row0 + r * NORM_ROWS, SUBLANES), NORM_ROWS)
        h_scr[dst, :] = _norm_mod_rows(x_ref[rows, :], g, shift, scale).astype(BF16)
        return 0
    lax.fori_loop(0, n_rows // NORM_ROWS, body, 0)


def _rope_tile(t, cosf, sinf):
    return t * cosf + pltpu.roll(t, HEAD_DIM // 2, axis=1) * sinf


def _in_proj_kernel(x_ref, mod_ref, g_ref, w_ref, cos_ref, sin_ref, zq_ref, zc_ref, zs_ref, h_scr):
    tm = x_ref.shape[0]
    _norm_mod_to_scratch(x_ref, h_scr, 0, tm, g_ref[...], mod_ref[0, 0:1, :], mod_ref[0, 1:2, :])
    heads_per_chunk = COL_CHUNK // HEAD_DIM
    for c0 in range(0, ZQ_HEADS, heads_per_chunk):
        acc = _dot(h_scr[...], w_ref[:, c0 * HEAD_DIM:(c0 + heads_per_chunk) * HEAD_DIM])
        for hh in range(heads_per_chunk):
            t = acc[:, hh * HEAD_DIM:(hh + 1) * HEAD_DIM]
            if c0 + hh in ZQ_ROPED:
                t = _rope_tile(t, cos_ref[...], sin_ref[...])
            zq_ref[:, (c0 + hh) * HEAD_DIM:(c0 + hh + 1) * HEAD_DIM] = t.astype(BF16)
    base = ZQ_HEADS * HEAD_DIM
    acc = _dot(h_scr[...], w_ref[:, base:base + ZC_HEADS * HEAD_DIM])
    for hh in range(ZC_HEADS):
        t = acc[:, hh * HEAD_DIM:(hh + 1) * HEAD_DIM]
        if hh < NSA_KV_HEADS:
            t = _rope_tile(t, cos_ref[...], sin_ref[...])
        zc_ref[:, hh * HEAD_DIM:(hh + 1) * HEAD_DIM] = t
    base += ZC_HEADS * HEAD_DIM
    zs_ref[...] = _dot(h_scr[...], w_ref[:, base:base + LANES])


def _in_proj(x2d, mod, g, wcat, cosf, sinf, seq):
    t, d = x2d.shape
    tm = ROW_TILE
    tpb = seq // tm
    nw = wcat.shape[1]
    return pl.pallas_call(
        _in_proj_kernel,
        grid=(t // tm,),
        in_specs=[
            pl.BlockSpec((tm, d), lambda i: (i, 0)),
            pl.BlockSpec((1, 6, d), lambda i: (i // tpb, 0, 0)),
            pl.BlockSpec((1, d), lambda i: (0, 0)),
            pl.BlockSpec((d, nw), lambda i: (0, 0), pipeline_mode=pl.Buffered(1)),
            pl.BlockSpec((tm, HEAD_DIM), lambda i: (i % tpb, 0)),
            pl.BlockSpec((tm, HEAD_DIM), lambda i: (i % tpb, 0)),
        ],
        out_specs=[
            pl.BlockSpec((tm, ZQ_HEADS * HEAD_DIM), lambda i: (i, 0)),
            pl.BlockSpec((tm, ZC_HEADS * HEAD_DIM), lambda i: (i, 0)),
            pl.BlockSpec((tm, LANES), lambda i: (i, 0)),
        ],
        out_shape=[
            jax.ShapeDtypeStruct((t, ZQ_HEADS * HEAD_DIM), BF16),
            jax.ShapeDtypeStruct((t, ZC_HEADS * HEAD_DIM), F32),
            jax.ShapeDtypeStruct((t, LANES), F32),
        ],
        scratch_shapes=[pltpu.VMEM((tm, d), BF16)],
        compiler_params=_cparams(("arbitrary",)),
        name="in_proj",
    )(x2d, mod, g, wcat, cosf, sinf)


def _split3(v):
    hi = v.astype(BF16)
    r1 = v - hi.astype(F32)
    mid = r1.astype(BF16)
    lo = (r1 - mid.astype(F32)).astype(BF16)
    return hi, mid, lo


def _fgate_kernel(zs_ref, bias_ref, tri_ref, cumc_ref):
    seq = zs_ref.shape[0]
    tri = tri_ref[...]
    carry = jnp.zeros((1, LANES), F32)
    for c in range(seq // LANES):
        zf = zs_ref[c * LANES:(c + 1) * LANES, :] + bias_ref[...]
        lf = jnp.minimum(zf, 0.0) - jnp.log1p(jnp.exp(-jnp.abs(zf)))
        hi, mid, lo = _split3(lf)
        cs = (_dot(tri, lo) + _dot(tri, mid)) + _dot(tri, hi) + carry
        carry = cs[LANES - 1:LANES, :]
        cumc_ref[c * LANES:(c + 1) * LANES, :] = cs


def _fgate(zs, bias_row, tri, batch, seq):
    t = zs.shape[0]
    return pl.pallas_call(
        _fgate_kernel,
        grid=(batch,),
        in_specs=[
            pl.BlockSpec((seq, LANES), lambda b: (b, 0)),
            pl.BlockSpec((1, LANES), lambda b: (0, 0)),
            pl.BlockSpec((LANES, LANES), lambda b: (0, 0)),
        ],
        out_specs=pl.BlockSpec((seq, LANES), lambda b: (b, 0)),
        out_shape=jax.ShapeDtypeStruct((t, LANES), F32),
        compiler_params=_cparams(("arbitrary",)),
        name="fgate_cumsum",
    )(zs, bias_row, tri)


def _lane_pick(tile, lane_idx):
    lane = lax.broadcasted_iota(jnp.int32, tile.shape, 1)
    return jnp.sum(jnp.where(lane == lane_idx, tile, 0.0), axis=1, keepdims=True)


def _online_step(carry, s2, v):
    m, l, acc = carry
    m_new = jnp.maximum(m, jnp.max(s2, axis=1, keepdims=True))
    alpha = jnp.exp2(m - m_new)
    p = jnp.exp2(s2 - m_new)
    l = alpha * l + jnp.sum(p, axis=1, keepdims=True)
    acc = alpha * acc + _dot(p.astype(BF16), v)
    return m_new, l, acc


def _online_init(rows):
    return (jnp.full((rows, 1), NEG_INF, F32), jnp.zeros((rows, 1), F32), jnp.zeros((rows, HEAD_DIM), F32))


def _placed_pieces(pieces, head, first_lane, sign, const_lanes):
    row = lax.broadcasted_iota(jnp.int32, (LANES, LANES), 0)
    col = lax.broadcasted_iota(jnp.int32, (LANES, LANES), 1)
    out = None
    for p, piece in enumerate(pieces):
        place = jnp.where(row == head, jnp.where(col == first_lane + p, sign, 0.0), 0.0).astype(BF16)
        term = _dot(piece, place)
        out = term if out is None else out + term
    lane = lax.broadcasted_iota(jnp.int32, (1, LANES), 1)
    ones = jnp.where(lane >= const_lanes[0], jnp.where(lane < const_lanes[1], 1.0, 0.0), 0.0)
    return (out + ones).astype(BF16)


def _fox_kernel(q_ref, k_ref, v_ref, cum_ref, o_ref, qaug_scr, kaug_scr):
    hb = pl.program_id(1)
    tile, grp = FOX_TQ, FOX_GROUP
    seq = q_ref.shape[0]
    scale = 1.0 / math.sqrt(HEAD_DIM)
    c2 = scale * LOG2E
    npc = 3
    head = lambda h: slice(h * HEAD_DIM, (h + 1) * HEAD_DIM)

    pieces = _split3(cum_ref[...] * (1.0 / scale))
    for g in range(grp):
        qaug_scr[:, head(2 * g)] = q_ref[:, head(g)]
        qaug_scr[:, head(2 * g + 1)] = _placed_pieces(pieces, hb * grp + g, 0, 1.0, (npc, 2 * npc))
        kaug_scr[:, head(2 * g)] = k_ref[:, head(g)]
        kaug_scr[:, head(2 * g + 1)] = _placed_pieces(pieces, hb * grp + g, npc, -1.0, (0, npc))

    causal = (lax.broadcasted_iota(jnp.int32, (tile, tile), 1) <= lax.broadcasted_iota(jnp.int32, (tile, tile), 0))
    for i in range(seq // tile):
        qrows = slice(i * tile, (i + 1) * tile)
        for g in range(grp):
            carry = _online_init(tile)
            for j in range(i + 1):
                krows = slice(j * tile, (j + 1) * tile)
                s2 = _dot_nt(qaug_scr[qrows, 2 * g * HEAD_DIM:(2 * g + 2) * HEAD_DIM],
                             kaug_scr[krows, 2 * g * HEAD_DIM:(2 * g + 2) * HEAD_DIM]) * c2
                if j == i:
                    s2 = jnp.where(causal, s2, NEG_INF)
                carry = _online_step(carry, s2, v_ref[krows, head(g)])
            _, l, acc = carry
            o_ref[qrows, head(g)] = (acc / l).astype(o_ref.dtype)


def _fox_attention(zq, cumc, batch, seq):
    t = zq.shape[0]
    grp = FOX_GROUP
    gw = grp * HEAD_DIM
    return pl.pallas_call(
        _fox_kernel,
        grid=(batch, FOX_HEADS // grp),
        in_specs=[
            pl.BlockSpec((seq, gw), lambda b, h: (b, ZQ_FQ // grp + h)),
            pl.BlockSpec((seq, gw), lambda b, h: (b, ZQ_FK // grp + h)),
            pl.BlockSpec((seq, gw), lambda b, h: (b, ZQ_FV // grp + h)),
            pl.BlockSpec((seq, LANES), lambda b, h: (b, 0)),
        ],
        out_specs=pl.BlockSpec((seq, gw), lambda b, h: (b, h)),
        out_shape=jax.ShapeDtypeStruct((t, FOX_HEADS * HEAD_DIM), BF16),
        scratch_shapes=[pltpu.VMEM((seq, 2 * gw), BF16), pltpu.VMEM((seq, 2 * gw), BF16)],
        compiler_params=_cparams(("arbitrary", "arbitrary")),
        name="fox_attention",
    )(zq, zq, zq, cumc)


def _compress_to(src_ref, hk, pos_ref, w1_ref, w2_ref, flat_scr, dst_scr):
    half = CMP_BLOCK // 2
    n_rows = dst_scr.shape[0]
    cols = slice(hk * HEAD_DIM, (hk + 1) * HEAD_DIM)
    for jp in range(half):
        rows = src_ref[pl.ds(jp, n_rows, stride=CMP_STRIDE), :]
        flat_scr[:, jp * HEAD_DIM:(jp + 1) * HEAD_DIM] = (rows + pos_ref[jp:jp + 1, :]).astype(BF16)
        nxt = pltpu.roll(rows, n_rows - 1, axis=0)
        flat_scr[:, (half + jp) * HEAD_DIM:(half + jp + 1) * HEAD_DIM] = (
            nxt + pos_ref[half + jp:half + jp + 1, :]).astype(BF16)
    pre = _dot(flat_scr[...], w1_ref[...])
    dst_scr[:, cols] = _dot(jax.nn.gelu(pre).astype(BF16), w2_ref[...]).astype(BF16)


def _nsa_kernel(q_ref, ks_ref, kw_ref, vs_ref, vw_ref, kcf0_ref, kcf1_ref, vcf0_ref, vcf1_ref, zs_ref,
                posk_ref, posv_ref, w1k_ref, w1v_ref, w2k_ref, w2v_ref, emask_ref,
                o_ref, ksaug_scr, kcmp_scr, vcmp_scr, flat_scr, tr_scr):
    i = pl.program_id(1)
    tq, tk, grp, nkv = NSA_TQ, NSA_TK, NSA_GROUP, NSA_KV_HEADS
    rows4 = grp * tq
    n_selb = SEL_BLOCK // CMP_STRIDE
    n_sel = LANES // n_selb
    c2 = LOG2E / math.sqrt(HEAD_DIM)
    q0 = i * tq
    head = lambda h: slice(h * HEAD_DIM, (h + 1) * HEAD_DIM)

    @pl.when(i == 0)
    def _():
        for hk in range(nkv):
            ksaug_scr[:, head(2 * hk)] = ks_ref[:, head(hk)]
            ksaug_scr[:, head(2 * hk + 1)] = emask_ref[...]
            _compress_to((kcf0_ref, kcf1_ref)[hk], hk, posk_ref, w1k_ref, w2k_ref, flat_scr, kcmp_scr)
            _compress_to((vcf0_ref, vcf1_ref)[hk], hk, posv_ref, w1v_ref, w2v_ref, flat_scr, vcmp_scr)

    def stack4(tile):
        return jnp.concatenate([tile] * grp, axis=0)

    rloc = lax.broadcasted_iota(jnp.int32, (tq, LANES), 0)
    lane = lax.broadcasted_iota(jnp.int32, (tq, LANES), 1)
    cvalid4 = stack4(jnp.where((lane * CMP_STRIDE + (CMP_BLOCK - 1)) <= q0 + rloc, 1.0, 0.0)) > 0.5
    blk = lax.broadcasted_iota(jnp.int32, (n_sel, tq), 0)
    cur = lax.shift_right_logical(q0 + lax.broadcasted_iota(jnp.int32, (n_sel, tq), 1), int(math.log2(SEL_BLOCK)))
    forced = jnp.where(blk == 0, 1.0, jnp.where(blk == cur, 1.0, jnp.where(blk == cur - 1, 1.0, 0.0)))
    gl = jax.nn.sigmoid(zs_ref[...])

    def prepare(hk):
        q4 = jnp.concatenate([q_ref[:, head(hk * grp + g)] for g in range(grp)], axis=0)
        sc = jnp.where(cvalid4, _dot_nt(q4, kcmp_scr[:, head(hk)]) * c2, NEG_INF)
        pc = jnp.where(cvalid4, jnp.exp2(sc - jnp.max(sc, axis=1, keepdims=True)), 0.0)
        pc = pc / jnp.maximum(jnp.sum(pc, axis=1, keepdims=True), TINY)
        o_cmp = _dot(pc.astype(BF16), vcmp_scr[:, head(hk)])
        psum = (pc[0:tq] + pc[tq:2 * tq]) + (pc[2 * tq:3 * tq] + pc[3 * tq:4 * tq])
        nsub = tq // LANES
        for u in range(nsub):
            tr_scr[hk * nsub + u] = psum[u * LANES:(u + 1) * LANES, :].T
        pooled = lambda t: ((tr_scr[t, pl.ds(0, n_sel, stride=n_selb), :] + tr_scr[t, pl.ds(1, n_sel, stride=n_selb), :])
                            + (tr_scr[t, pl.ds(2, n_sel, stride=n_selb), :] + tr_scr[t, pl.ds(3, n_sel, stride=n_selb), :]))
        imp = jnp.concatenate([pooled(hk * nsub + u) for u in range(nsub)], axis=1)
        score = jnp.where(blk <= cur, imp + forced * FORCE_BONUS, NEG_INF)
        rank = jnp.zeros((n_sel, tq), F32)
        for kk in range(n_sel):
            row = score[kk:kk + 1, :]
            earlier = jnp.where(blk > kk, 1.0, 0.0)
            rank = rank + jnp.where(row > score, 1.0, jnp.where(row == score, earlier, 0.0))
        dropped = jnp.where(rank < float(SEL_TOPK), 0.0, 1.0)
        pieces = []
        for u in range(nsub):
            tr_scr[hk * nsub + u, 0:n_sel, :] = dropped[:, u * LANES:(u + 1) * LANES]
            tr_scr[hk * nsub + u, n_sel:, :] = jnp.zeros((LANES - n_sel, LANES), F32)
            pieces.append(tr_scr[hk * nsub + u].T)
        notsel = jnp.concatenate(pieces, axis=0).astype(BF16)
        return q4, jnp.concatenate([q4, stack4(notsel)], axis=1), o_cmp

    pre = [prepare(hk) for hk in range(nkv)]

    def sel_step(j, carries, bias4):
        rows = pl.ds(pl.multiple_of(j * tk, tk), tk)
        out = []
        for hk in range(nkv):
            s2 = _dot_nt(pre[hk][1], ksaug_scr[rows, 2 * hk * HEAD_DIM:(2 * hk + 2) * HEAD_DIM]) * c2
            if bias4 is not None:
                s2 = s2 + bias4
            out.append(_online_step(carries[hk], s2, vs_ref[rows, head(hk)]))
        return tuple(out)

    n_full = q0 // tk
    carries = lax.fori_loop(0, n_full, lambda j, c: sel_step(j, c, None),
                            tuple(_online_init(rows4) for _ in range(nkv)))
    rk = lax.broadcasted_iota(jnp.int32, (tq, tk), 0)
    ck = lax.broadcasted_iota(jnp.int32, (tq, tk), 1)
    causal4 = stack4(jnp.where(n_full * tk + ck <= q0 + rk, 0.0, NEG_INF))
    carries = sel_step(n_full, carries, causal4)

    w0 = pl.multiple_of(jnp.maximum(q0 - WINDOW, 0), tq)
    rw = lax.broadcasted_iota(jnp.int32, (tq, WIN_KEYS), 0)
    cw = lax.broadcasted_iota(jnp.int32, (tq, WIN_KEYS), 1)
    dist = (q0 - w0) + rw - cw
    wbias4 = stack4(jnp.where(jnp.abs(2 * dist - (WINDOW - 1)) <= (WINDOW - 1), 0.0, NEG_INF))
    wrows = pl.ds(w0, WIN_KEYS)

    for hk in range(nkv):
        q4, _, o_cmp = pre[hk]
        _, l, acc = carries[hk]
        o_sel = acc / l
        s2 = _dot_nt(q4, kw_ref[wrows, head(hk)]) * c2 + wbias4
        _, lw, accw = _online_step(_online_init(rows4), s2, vw_ref[wrows, head(hk)])
        o_win = accw / lw
        gate = lambda c: jnp.concatenate(
            [_lane_pick(gl, ZS_GATE0 + 3 * (hk * grp + g) + c) for g in range(grp)], axis=0)
        o4 = gate(0) * o_cmp + gate(1) * o_sel + gate(2) * o_win
        for g in range(grp):
            o_ref[:, head(hk * grp + g)] = o4[g * tq:(g + 1) * tq, :].astype(o_ref.dtype)


def _nsa_attention(zq, zc, zs, posk, posv, w1k, w1v, w2k, w2v, emask, batch, seq):
    t = zq.shape[0]
    tq, nkv = NSA_TQ, NSA_KV_HEADS
    nq = seq // tq
    qw = NSA_HEADS * HEAD_DIM
    kvw = nkv * HEAD_DIM
    kv = lambda col: pl.BlockSpec((seq, kvw), lambda b, i: (b, col // nkv))
    const = lambda shape: pl.BlockSpec(shape, lambda b, i: (0,) * len(shape))
    return pl.pallas_call(
        _nsa_kernel,
        grid=(batch, nq),
        in_specs=[
            pl.BlockSpec((tq, qw), lambda b, i: (b * nq + i, ZQ_NQ // NSA_HEADS)),
            kv(ZQ_KS), kv(ZQ_KW), kv(ZQ_VS), kv(ZQ_VW),
            *[pl.BlockSpec((seq, HEAD_DIM), functools.partial(lambda b, i, h: (b, h), h=h)) for h in range(ZC_HEADS)],
            pl.BlockSpec((tq, LANES), lambda b, i: (b * nq + i, 0)),
            const((CMP_BLOCK, HEAD_DIM)), const((CMP_BLOCK, HEAD_DIM)),
            const((CMP_BLOCK * HEAD_DIM, HEAD_DIM)), const((CMP_BLOCK * HEAD_DIM, HEAD_DIM)),
            const((HEAD_DIM, HEAD_DIM)), const((HEAD_DIM, HEAD_DIM)),
            const((seq, LANES)),
        ],
        out_specs=pl.BlockSpec((tq, qw), lambda b, i: (b * nq + i, 0)),
        out_shape=jax.ShapeDtypeStruct((t, qw), BF16),
        scratch_shapes=[
            pltpu.VMEM((seq, 2 * kvw), BF16),
            pltpu.VMEM((LANES, kvw), BF16),
            pltpu.VMEM((LANES, kvw), BF16),
            pltpu.VMEM((LANES, CMP_BLOCK * HEAD_DIM), BF16),
            pltpu.VMEM((nkv * (tq // LANES), LANES, LANES), F32),
        ],
        compiler_params=_cparams(("arbitrary", "arbitrary")),
        name="nsa_attention",
    )(zq, zq, zq, zq, zq, zc, zc, zc, zc, zs, posk, posv, w1k, w1v, w2k, w2v, emask)


def _out_proj_kernel(of_ref, on_ref, wf_ref, wn_ref, x_ref, mod_ref, o_ref):
    n = o_ref.shape[1]
    for c0 in range(0, n, COL_CHUNK):
        cols = slice(c0, c0 + COL_CHUNK)
        acc = _dot(of_ref[...], wf_ref[:, cols]) + _dot(on_ref[...], wn_ref[:, cols])
        o_ref[:, cols] = x_ref[:, cols] + mod_ref[0, 2:3, cols] * acc


def _out_proj(o_fox, o_nsa, w_fox, w_nsa, x2d, mod, seq):
    t, d = x2d.shape
    tm = ROW_TILE
    tpb = seq // tm
    kf, kn = o_fox.shape[1], o_nsa.shape[1]
    return pl.pallas_call(
        _out_proj_kernel,
        grid=(t // tm,),
        in_specs=[
            pl.BlockSpec((tm, kf), lambda i: (i, 0)),
            pl.BlockSpec((tm, kn), lambda i: (i, 0)),
            pl.BlockSpec((kf, d), lambda i: (0, 0), pipeline_mode=pl.Buffered(1)),
            pl.BlockSpec((kn, d), lambda i: (0, 0), pipeline_mode=pl.Buffered(1)),
            pl.BlockSpec((tm, d), lambda i: (i, 0)),
            pl.BlockSpec((1, 6, d), lambda i: (i // tpb, 0, 0)),
        ],
        out_specs=pl.BlockSpec((tm, d), lambda i: (i, 0)),
        out_shape=jax.ShapeDtypeStruct((t, d), F32),
        compiler_params=_cparams(("arbitrary",)),
        name="out_proj",
    )(o_fox, o_nsa, w_fox, w_nsa, x2d, mod)


def _conv_rows(u, cw_ref, cb_ref):
    y = cb_ref[0] + cw_ref[0, 0:1, :] * pltpu.roll(u, 2, axis=0)
    y = y + cw_ref[0, 1:2, :] * pltpu.roll(u, 1, axis=0)
    return y + cw_ref[0, 2:3, :] * u


def _ffn_kernel(x_ref, xh_ref, mod_ref, g_ref, wu_ref, cw_ref, cb_ref,
                wd_ref, fg_ref, o_ref, h_scr, acc_scr, *, tiles_per_batch, final_norm):
    i = pl.program_id(0)
    c = pl.program_id(1)
    tm = x_ref.shape[0]

    @pl.when(c == 0)
    def _():
        g, shift, scale = g_ref[...], mod_ref[0, 3:4, :], mod_ref[0, 4:5, :]
        halo = _norm_mod_rows(xh_ref[...], g, shift, scale)
        first = (i % tiles_per_batch) == 0
        h_scr[0:HALO, :] = jnp.where(first, 0.0, halo).astype(BF16)
        _norm_mod_to_scratch(x_ref, h_scr, HALO, tm, g, shift, scale)
        acc_scr[...] = jnp.zeros_like(acc_scr)

    tf = wd_ref.shape[0]
    y = _conv_rows(_dot(h_scr[...], wu_ref[0]), cw_ref, cb_ref)
    yg, yv = y[HALO:, :tf], y[HALO:, tf:]
    act = (yg * jax.nn.sigmoid(yg)) * yv
    acc_scr[...] += _dot(act.astype(BF16), wd_ref[...])

    @pl.when(c == pl.num_programs(1) - 1)
    def _():
        def body(r, _):
            rows = pl.ds(pl.multiple_of(r * NORM_ROWS, NORM_ROWS), NORM_ROWS)
            y = x_ref[rows, :] + mod_ref[0, 5:6, :] * acc_scr[rows, :]
            if final_norm:
                ms = jnp.mean(y * y, axis=-1, keepdims=True)
                y = (y * lax.rsqrt(ms + EPS)) * fg_ref[...]
            o_ref[rows, :] = y
            return 0
        lax.fori_loop(0, tm // NORM_ROWS, body, 0)


def _ffn(x2d, mod, g, w_up, conv_w, conv_b, w_down, final_g, seq, final_norm):
    t, d = x2d.shape
    dff = w_down.shape[0]
    tm, tf = FFN_TILE, FF_CHUNK
    tpb = seq // tm
    nc = dff // tf
    hb = tm // HALO
    kern = functools.partial(_ffn_kernel, tiles_per_batch=tpb, final_norm=final_norm)
    return pl.pallas_call(
        kern,
        grid=(t // tm, nc),
        in_specs=[
            pl.BlockSpec((tm, d), lambda i, c: (i, 0), pipeline_mode=pl.Buffered(1)),
            pl.BlockSpec((HALO, d), lambda i, c: (jnp.maximum(i * hb - 1, 0), 0)),
            pl.BlockSpec((1, 6, d), lambda i, c: (i // tpb, 0, 0)),
            pl.BlockSpec((1, d), lambda i, c: (0, 0)),
            pl.BlockSpec((1, d, 2 * tf), lambda i, c: (c, 0, 0)),
            pl.BlockSpec((1, CONV_WIDTH, 2 * tf), lambda i, c: (c, 0, 0)),
            pl.BlockSpec((1, 1, 2 * tf), lambda i, c: (c, 0, 0)),
            pl.BlockSpec((tf, d), lambda i, c: (c, 0)),
            pl.BlockSpec((1, d), lambda i, c: (0, 0)),
        ],
        out_specs=pl.BlockSpec((tm, d), lambda i, c: (i, 0), pipeline_mode=pl.Buffered(1)),
        out_shape=jax.ShapeDtypeStruct((t, d), F32),
        scratch_shapes=[pltpu.VMEM((HALO + tm, d), BF16), pltpu.VMEM((tm, d), F32)],
        compiler_params=_cparams(("arbitrary", "arbitrary")),
        name="conv_ffn",
    )(x2d, x2d, mod, g, _chunk_major(w_up, nc), _chunk_major(conv_w, nc), _chunk_major(conv_b, nc), w_down, final_g)


def _chunk_major(a, nc):
    r, two_dff = a.shape
    tf = two_dff // (2 * nc)
    return a.reshape(r, 2, nc, tf).transpose(2, 0, 1, 3).reshape(nc, r, 2 * tf)


def _rope_tables(seq):
    inv = ROPE_THETA ** (-jnp.arange(0, HEAD_DIM, 2, dtype=F32) / HEAD_DIM)
    ang = jnp.arange(seq, dtype=F32)[:, None] * inv[None, :]
    cos, sin = jnp.cos(ang), jnp.sin(ang)
    return jnp.concatenate([cos, cos], axis=-1), jnp.concatenate([-sin, sin], axis=-1)


def _select_mask_table(seq):
    key_blk = (np.arange(seq) // SEL_BLOCK).reshape(seq, 1)
    return jnp.asarray(np.where(np.arange(LANES).reshape(1, LANES) == key_blk, NEG_INF, 0.0), dtype=BF16)


def _split_w_in(w):
    hd = HEAD_DIM
    sizes = [FOX_HEADS * hd] * 3 + [FOX_HEADS] + [NSA_HEADS * hd] + [NSA_KV_HEADS * hd] * 6 + [3 * NSA_HEADS]
    w = w.astype(BF16)
    fq, fk, fv, ff, nq, kc, vc, ks, vs, kw, vw, ng = jnp.split(w, [int(o) for o in np.cumsum(sizes)[:-1]], axis=-1)
    pad = jnp.zeros((w.shape[0], LANES - FOX_HEADS - 3 * NSA_HEADS), w.dtype)
    return jnp.concatenate([fq, fk, fv, nq, ks, kw, vs, vw, kc, vc, ff, ng, pad], axis=-1)


def kernel(x, c, attn_norm_g, ffn_norm_g, w_ada, b_ada, w_in, b_fgate, cmp_pos_k, cmp_pos_v,
           w_cmp1_k, w_cmp2_k, w_cmp1_v, w_cmp2_v, w_out, w_up, conv_w, conv_b, w_down, final_norm_g):
    batch, seq, d = x.shape
    depth = w_ada.shape[0]
    assert seq % ROW_TILE == 0 and seq % FFN_TILE == 0 and seq % FOX_TQ == 0 and seq % NSA_TK == 0 and seq // SEL_BLOCK == LANES // 4
    assert seq >= WIN_KEYS and w_down.shape[1] % FF_CHUNK == 0 and d % COL_CHUNK == 0

    cosf, sinf = _rope_tables(seq)
    emask = _select_mask_table(seq)
    tri = jnp.asarray(np.tril(np.ones((LANES, LANES))), dtype=BF16)
    mod_all = _ada_mod(c, w_ada, b_ada)

    x2d = x.reshape(batch * seq, d)
    n_fox = FOX_HEADS * HEAD_DIM
    for l in range(depth):
        mod = mod_all[l].reshape(batch, 6, d)
        zq, zc, zs = _in_proj(x2d, mod, attn_norm_g[l].reshape(1, d), _split_w_in(w_in[l]), cosf, sinf, seq)
        bias_row = jnp.zeros((1, LANES), F32).at[0, :FOX_HEADS].set(b_fgate[l])
        cumc = _fgate(zs, bias_row, tri, batch, seq)
        o_fox = _fox_attention(zq, cumc, batch, seq)
        o_nsa = _nsa_attention(zq, zc, zs, cmp_pos_k[l], cmp_pos_v[l],
                               w_cmp1_k[l].astype(BF16), w_cmp1_v[l].astype(BF16),
                               w_cmp2_k[l].astype(BF16), w_cmp2_v[l].astype(BF16), emask, batch, seq)
        w_o = w_out[l].astype(BF16)
        x2d = _out_proj(o_fox, o_nsa, w_o[:n_fox], w_o[n_fox:], x2d, mod, seq)
        x2d = _ffn(x2d, mod, ffn_norm_g[l].reshape(1, d), w_up[l].astype(BF16), conv_w[l],
                   conv_b[l].reshape(1, -1), w_down[l].astype(BF16), final_norm_g.reshape(1, d), seq,
                   final_norm=(l == depth - 1))
    return x2d.reshape(batch, seq, d)
```

```python
import functools
import math

import numpy as np
import jax
import jax.numpy as jnp
from jax import lax
from jax.experimental import pallas as pl
from jax.experimental.pallas import tpu as pltpu

HEAD_DIM = 128
FOX_HEADS = 8
NSA_HEADS = 8
NSA_KV_HEADS = 2
NSA_GROUP = NSA_HEADS // NSA_KV_HEADS
CMP_BLOCK = 32
CMP_STRIDE = 16
SEL_BLOCK = 64
SEL_TOPK = 16
WINDOW = 512
CONV_WIDTH = 3
ROPE_THETA = 10000.0
FORCE_BONUS = 1000.0
NEG_INF = -1e30
TINY = 1e-30
EPS = 1e-6
LOG2E = math.log2(math.e)

LANES = 128
SUBLANES = 8
VMEM_LIMIT_BYTES = 56 * 1024 * 1024

ROW_TILE = 512
FFN_TILE = 1024
COL_CHUNK = 512
NORM_ROWS = 128
NORM_SUBROWS = 16
NORM_UNROLL = 8
FOX_TQ = 512
FOX_GROUP = 2
NSA_TQ = 256
NSA_TK = 512
WIN_KEYS = WINDOW + NSA_TQ
HALO = 2 * SUBLANES
FF_CHUNK = 512

BF16 = jnp.bfloat16
F32 = jnp.float32

ZQ_FQ, ZQ_FK, ZQ_FV, ZQ_NQ, ZQ_KS, ZQ_KW, ZQ_VS, ZQ_VW = 0, 8, 16, 24, 32, 34, 36, 38
ZQ_HEADS = 40
ZQ_ROPED = tuple(range(ZQ_NQ, ZQ_KW + NSA_KV_HEADS))
ZC_HEADS = 4
ZS_GATE0 = FOX_HEADS


def _cparams(sem):
    return pltpu.CompilerParams(dimension_semantics=sem, vmem_limit_bytes=VMEM_LIMIT_BYTES)


def _dot(a, b):
    return jnp.dot(a, b, preferred_element_type=F32)


def _dot_nt(a, b):
    return lax.dot_general(a, b, (((1,), (1,)), ((), ())), preferred_element_type=F32)


def _ada_kernel(c_ref, w_ref, b_ref, o_ref):
    c = c_ref[...]
    ca = (c * jax.nn.sigmoid(c)).astype(BF16)
    o_ref[0] = _dot(ca, w_ref[0].astype(BF16)) + b_ref[0]


def _ada_mod(c, w_ada, b_ada):
    depth, d, n = w_ada.shape
    b = c.shape[0]
    tn = 1024
    return pl.pallas_call(
        _ada_kernel,
        grid=(depth, n // tn),
        in_specs=[
            pl.BlockSpec((b, d), lambda l, j: (0, 0)),
            pl.BlockSpec((1, d, tn), lambda l, j: (l, 0, j)),
            pl.BlockSpec((1, 1, tn), lambda l, j: (l, 0, j)),
        ],
        out_specs=pl.BlockSpec((1, b, tn), lambda l, j: (l, 0, j)),
        out_shape=jax.ShapeDtypeStruct((depth, b, n), F32),
        compiler_params=_cparams(("arbitrary", "arbitrary")),
        name="ada_mod",
    )(c, w_ada, b_ada.reshape(depth, 1, n))


def _norm_mod_rows(x, gain, shift):
    ms = jnp.mean(x * x, axis=-1, keepdims=True)
    return (x * lax.rsqrt(ms + EPS)) * gain + shift


def _norm_mod_to_scratch(x_ref, h_scr, row0, n_rows, gain, shift):
    nr = NORM_SUBROWS
    def body(r, _):
        rows = pl.ds(pl.multiple_of(r * nr, nr), nr)
        dst = pl.ds(pl.multiple_of(row0 + r * nr, nr), nr)
        h_scr[dst, :] = _norm_mod_rows(x_ref[rows, :], gain, shift).astype(BF16)
        return 0
    lax.fori_loop(0, n_rows // nr, body, 0, unroll=NORM_UNROLL)


def _rope_tile(t, cosf, sinf):
    return t * cosf + pltpu.roll(t, HEAD_DIM // 2, axis=1) * sinf


def _in_proj_kernel(x_ref, mod_ref, g_ref, w_ref, cos_ref, sin_ref, zq_ref, zc_ref, zs_ref, h_scr):
    tm = x_ref.shape[0]
    _norm_mod_to_scratch(x_ref, h_scr, 0, tm, g_ref[...] * (1.0 + mod_ref[0, 1:2, :]), mod_ref[0, 0:1, :])
    heads_per_chunk = COL_CHUNK // HEAD_DIM
    for c0 in range(0, ZQ_HEADS, heads_per_chunk):
        acc = _dot(h_scr[...], w_ref[:, c0 * HEAD_DIM:(c0 + heads_per_chunk) * HEAD_DIM])
        for hh in range(heads_per_chunk):
            t = acc[:, hh * HEAD_DIM:(hh + 1) * HEAD_DIM]
            if c0 + hh in ZQ_ROPED:
                t = _rope_tile(t, cos_ref[...], sin_ref[...])
            zq_ref[:, (c0 + hh) * HEAD_DIM:(c0 + hh + 1) * HEAD_DIM] = t.astype(BF16)
    base = ZQ_HEADS * HEAD_DIM
    acc = _dot(h_scr[...], w_ref[:, base:base + ZC_HEADS * HEAD_DIM])
    for hh in range(ZC_HEADS):
        t = acc[:, hh * HEAD_DIM:(hh + 1) * HEAD_DIM]
        if hh < NSA_KV_HEADS:
            t = _rope_tile(t, cos_ref[...], sin_ref[...])
        zc_ref[:, hh * HEAD_DIM:(hh + 1) * HEAD_DIM] = t
    base += ZC_HEADS * HEAD_DIM
    zs_ref[...] = _dot(h_scr[...], w_ref[:, base:base + LANES])


def _in_proj(x2d, mod, g, wcat, cosf, sinf, seq):
    t, d = x2d.shape
    tm = ROW_TILE
    tpb = seq // tm
    nw = wcat.shape[1]
    return pl.pallas_call(
        _in_proj_kernel,
        grid=(t // tm,),
        in_specs=[
            pl.BlockSpec((tm, d), lambda i: (i, 0)),
            pl.BlockSpec((1, 6, d), lambda i: (i // tpb, 0, 0)),
            pl.BlockSpec((1, d), lambda i: (0, 0)),
            pl.BlockSpec((d, nw), lambda i: (0, 0), pipeline_mode=pl.Buffered(1)),
            pl.BlockSpec((tm, HEAD_DIM), lambda i: (i % tpb, 0)),
            pl.BlockSpec((tm, HEAD_DIM), lambda i: (i % tpb, 0)),
        ],
        out_specs=[
            pl.BlockSpec((tm, ZQ_HEADS * HEAD_DIM), lambda i: (i, 0)),
            pl.BlockSpec((tm, ZC_HEADS * HEAD_DIM), lambda i: (i, 0)),
            pl.BlockSpec((tm, LANES), lambda i: (i, 0)),
        ],
        out_shape=[
            jax.ShapeDtypeStruct((t, ZQ_HEADS * HEAD_DIM), BF16),
            jax.ShapeDtypeStruct((t, ZC_HEADS * HEAD_DIM), F32),
            jax.ShapeDtypeStruct((t, LANES), F32),
        ],
        scratch_shapes=[pltpu.VMEM((tm, d), BF16)],
        compiler_params=_cparams(("arbitrary",)),
        name="in_proj",
    )(x2d, mod, g, wcat, cosf, sinf)


def _split3(v):
    hi = v.astype(BF16)
    r1 = v - hi.astype(F32)
    mid = r1.astype(BF16)
    lo = (r1 - mid.astype(F32)).astype(BF16)
    return hi, mid, lo


def _fgate_kernel(zs_ref, bias_ref, tri_ref, cumc_ref):
    seq = zs_ref.shape[0]
    tri = tri_ref[...]
    carry = jnp.zeros((1, LANES), F32)
    for c in range(seq // LANES):
        zf = zs_ref[c * LANES:(c + 1) * LANES, :] + bias_ref[...]
        lf = jnp.minimum(zf, 0.0) - jnp.log1p(jnp.exp(-jnp.abs(zf)))
        hi, mid, lo = _split3(lf)
        cs = (_dot(tri, lo) + _dot(tri, mid)) + _dot(tri, hi) + carry
        carry = cs[LANES - 1:LANES, :]
        cumc_ref[c * LANES:(c + 1) * LANES, :] = cs


def _fgate(zs, bias_row, tri, batch, seq):
    t = zs.shape[0]
    return pl.pallas_call(
        _fgate_kernel,
        grid=(batch,),
        in_specs=[
            pl.BlockSpec((seq, LANES), lambda b: (b, 0)),
            pl.BlockSpec((1, LANES), lambda b: (0, 0)),
            pl.BlockSpec((LANES, LANES), lambda b: (0, 0)),
        ],
        out_specs=pl.BlockSpec((seq, LANES), lambda b: (b, 0)),
        out_shape=jax.ShapeDtypeStruct((t, LANES), F32),
        compiler_params=_cparams(("arbitrary",)),
        name="fgate_cumsum",
    )(zs, bias_row, tri)


def _lane_pick(tile, lane_idx):
    lane = lax.broadcasted_iota(jnp.int32, tile.shape, 1)
    return jnp.sum(jnp.where(lane == lane_idx, tile, 0.0), axis=1, keepdims=True)


def _online_step(carry, s2, v_ones):
    m, acc = carry
    m_new = jnp.maximum(m, jnp.max(s2, axis=1, keepdims=True))
    p = jnp.exp2(s2 - m_new)
    acc = jnp.exp2(m - m_new) * acc + _dot(p.astype(BF16), v_ones)
    return m_new, acc


def _online_init(rows):
    return jnp.full((rows, 1), NEG_INF, F32), jnp.zeros((rows, 2 * HEAD_DIM), F32)


def _online_finish(carry):
    _, acc = carry
    return acc[:, :HEAD_DIM] / acc[:, HEAD_DIM:HEAD_DIM + 1]


def _online_step_sum(carry, s2, v):
    m, l, acc = carry
    m_new = jnp.maximum(m, jnp.max(s2, axis=1, keepdims=True))
    alpha = jnp.exp2(m - m_new)
    p = jnp.exp2(s2 - m_new)
    l = alpha * l + jnp.sum(p, axis=1, keepdims=True)
    acc = alpha * acc + _dot(p.astype(BF16), v)
    return m_new, l, acc


def _online_init_sum(rows):
    return (jnp.full((rows, 1), NEG_INF, F32), jnp.zeros((rows, 1), F32), jnp.zeros((rows, HEAD_DIM), F32))


def _online_finish_sum(carry):
    _, l, acc = carry
    return acc / l


def _ones_lane0(rows):
    return jnp.where(lax.broadcasted_iota(jnp.int32, (rows, LANES), 1) == 0, 1.0, 0.0).astype(BF16)


def _bias_lanes(pieces, head):
    npc = len(pieces)

    def place(n_in):
        row = lax.broadcasted_iota(jnp.int32, (n_in * LANES, 2 * LANES), 0)
        col = lax.broadcasted_iota(jnp.int32, (n_in * LANES, 2 * LANES), 1)
        return row, col
    hi_mid = jnp.concatenate(pieces[:2], axis=1)
    row, col = place(2)
    pc = jnp.where(row >= LANES, 1, 0)
    hit = row - pc * LANES == head
    w_a = jnp.where(hit, jnp.where(col == pc, 1.0, jnp.where(col == LANES + npc + pc, -1.0, 0.0)), 0.0).astype(BF16)
    row, col = place(1)
    w_b = jnp.where(row == head,
                    jnp.where(col == npc - 1, 1.0, jnp.where(col == LANES + 2 * npc - 1, -1.0, 0.0)), 0.0).astype(BF16)
    lane = lax.broadcasted_iota(jnp.int32, (1, 2 * LANES), 1)
    in_range = lambda lo, hi: jnp.where(lane >= lo, jnp.where(lane < hi, 1.0, 0.0), 0.0)
    ones = in_range(npc, 2 * npc) + in_range(LANES, LANES + npc)
    return (_dot(hi_mid, w_a) + _dot(pieces[2], w_b) + ones).astype(BF16)


def _fox_kernel(q_ref, k_ref, v_ref, cum_ref, o_ref, qaug_scr, kaug_scr, vaug_scr):
    hb = pl.program_id(1)
    tile, grp = FOX_TQ, FOX_GROUP
    seq = q_ref.shape[0]
    scale = 1.0 / math.sqrt(HEAD_DIM)
    c2 = scale * LOG2E
    head = lambda h: slice(h * HEAD_DIM, (h + 1) * HEAD_DIM)

    pieces = _split3(cum_ref[...] * (1.0 / scale))
    ones = _ones_lane0(seq)
    for g in range(grp):
        bias = _bias_lanes(pieces, hb * grp + g)
        qaug_scr[:, head(2 * g)] = q_ref[:, head(g)]
        qaug_scr[:, head(2 * g + 1)] = bias[:, :LANES]
        kaug_scr[:, head(2 * g)] = k_ref[:, head(g)]
        kaug_scr[:, head(2 * g + 1)] = bias[:, LANES:]
        vaug_scr[:, head(2 * g)] = v_ref[:, head(g)]
        vaug_scr[:, head(2 * g + 1)] = ones

    causal = (lax.broadcasted_iota(jnp.int32, (tile, tile), 1) <= lax.broadcasted_iota(jnp.int32, (tile, tile), 0))
    for i in range(seq // tile):
        qrows = slice(i * tile, (i + 1) * tile)
        for g in range(grp):
            carry = _online_init(tile)
            for j in range(i + 1):
                krows = slice(j * tile, (j + 1) * tile)
                s2 = _dot_nt(qaug_scr[qrows, 2 * g * HEAD_DIM:(2 * g + 2) * HEAD_DIM],
                             kaug_scr[krows, 2 * g * HEAD_DIM:(2 * g + 2) * HEAD_DIM]) * c2
                if j == i:
                    s2 = jnp.where(causal, s2, NEG_INF)
                carry = _online_step(carry, s2, vaug_scr[krows, 2 * g * HEAD_DIM:(2 * g + 2) * HEAD_DIM])
            o_ref[qrows, head(g)] = _online_finish(carry).astype(o_ref.dtype)


def _fox_attention(zq, cumc, batch, seq):
    t = zq.shape[0]
    grp = FOX_GROUP
    gw = grp * HEAD_DIM
    return pl.pallas_call(
        _fox_kernel,
        grid=(batch, FOX_HEADS // grp),
        in_specs=[
            pl.BlockSpec((seq, gw), lambda b, h: (b, ZQ_FQ // grp + h)),
            pl.BlockSpec((seq, gw), lambda b, h: (b, ZQ_FK // grp + h)),
            pl.BlockSpec((seq, gw), lambda b, h: (b, ZQ_FV // grp + h)),
            pl.BlockSpec((seq, LANES), lambda b, h: (b, 0)),
        ],
        out_specs=pl.BlockSpec((seq, gw), lambda b, h: (b, h)),
        out_shape=jax.ShapeDtypeStruct((t, FOX_HEADS * HEAD_DIM), BF16),
        scratch_shapes=[pltpu.VMEM((seq, 2 * gw), BF16)] * 3,
        compiler_params=_cparams(("arbitrary", "arbitrary")),
        name="fox_attention",
    )(zq, zq, zq, cumc)


def _compress_to(src_ref, hk, pos_ref, w1_ref, w2_ref, flat_scr, dst_scr):
    half = CMP_BLOCK // 2
    n_rows = dst_scr.shape[0]
    cols = slice(hk * HEAD_DIM, (hk + 1) * HEAD_DIM)
    for jp in range(half):
        rows = src_ref[pl.ds(jp, n_rows, stride=CMP_STRIDE), :]
        flat_scr[:, jp * HEAD_DIM:(jp + 1) * HEAD_DIM] = (rows + pos_ref[jp:jp + 1, :]).astype(BF16)
        nxt = pltpu.roll(rows, n_rows - 1, axis=0)
        flat_scr[:, (half + jp) * HEAD_DIM:(half + jp + 1) * HEAD_DIM] = (
            nxt + pos_ref[half + jp:half + jp + 1, :]).astype(BF16)
    pre = _dot(flat_scr[...], w1_ref[...])
    dst_scr[:, cols] = _dot(jax.nn.gelu(pre).astype(BF16), w2_ref[...]).astype(BF16)


def _nsa_kernel(q_ref, ks_ref, kw_ref, vs_ref, vw_ref, kcf0_ref, kcf1_ref, vcf0_ref, vcf1_ref, zs_ref,
                posk_ref, posv_ref, w1k_ref, w1v_ref, w2k_ref, w2v_ref, emask_ref,
                o_ref, ksaug_scr, kcmp_scr, vcmp_scr, flat_scr, tr_scr):
    i = pl.program_id(1)
    tq, tk, grp, nkv = NSA_TQ, NSA_TK, NSA_GROUP, NSA_KV_HEADS
    rows4 = grp * tq
    n_selb = SEL_BLOCK // CMP_STRIDE
    n_sel = LANES // n_selb
    c2 = LOG2E / math.sqrt(HEAD_DIM)
    q0 = i * tq
    head = lambda h: slice(h * HEAD_DIM, (h + 1) * HEAD_DIM)

    @pl.when(i == 0)
    def _():
        for hk in range(nkv):
            ksaug_scr[:, head(2 * hk)] = ks_ref[:, head(hk)]
            ksaug_scr[:, head(2 * hk + 1)] = emask_ref[...]
            _compress_to((kcf0_ref, kcf1_ref)[hk], hk, posk_ref, w1k_ref, w2k_ref, flat_scr, kcmp_scr)
            _compress_to((vcf0_ref, vcf1_ref)[hk], hk, posv_ref, w1v_ref, w2v_ref, flat_scr, vcmp_scr)

    for n_full in range(ks_ref.shape[0] // tk):
        pl.when(q0 // tk == n_full)(functools.partial(
            _nsa_tile, n_full, q0, q_ref, kw_ref, vs_ref, vw_ref, zs_ref, o_ref, ksaug_scr, kcmp_scr, vcmp_scr, tr_scr))


def _nsa_tile(n_full, q0, q_ref, kw_ref, vs_ref, vw_ref, zs_ref, o_ref, ksaug_scr, kcmp_scr, vcmp_scr, tr_scr):
    tq, tk, grp, nkv = NSA_TQ, NSA_TK, NSA_GROUP, NSA_KV_HEADS
    rows4 = grp * tq
    n_selb = SEL_BLOCK // CMP_STRIDE
    n_sel = LANES // n_selb
    c2 = LOG2E / math.sqrt(HEAD_DIM)
    head = lambda h: slice(h * HEAD_DIM, (h + 1) * HEAD_DIM)

    def stack4(tile):
        return jnp.concatenate([tile] * grp, axis=0)

    rloc = lax.broadcasted_iota(jnp.int32, (tq, LANES), 0)
    lane = lax.broadcasted_iota(jnp.int32, (tq, LANES), 1)
    cvalid4 = stack4(jnp.where((lane * CMP_STRIDE + (CMP_BLOCK - 1)) <= q0 + rloc, 1.0, 0.0)) > 0.5
    blk = lax.broadcasted_iota(jnp.int32, (n_sel, tq), 0)
    cur = lax.shift_right_logical(q0 + lax.broadcasted_iota(jnp.int32, (n_sel, tq), 1), int(math.log2(SEL_BLOCK)))
    forced = jnp.where(blk == 0, 1.0, jnp.where(blk == cur, 1.0, jnp.where(blk == cur - 1, 1.0, 0.0)))
    gl = jax.nn.sigmoid(zs_ref[...])

    def prepare(hk):
        q4 = jnp.concatenate([q_ref[:, head(hk * grp + g)] for g in range(grp)], axis=0)
        sc = jnp.where(cvalid4, _dot_nt(q4, kcmp_scr[:, head(hk)]) * c2, NEG_INF)
        pc = jnp.where(cvalid4, jnp.exp2(sc - jnp.max(sc, axis=1, keepdims=True)), 0.0)
        pc = pc / jnp.maximum(jnp.sum(pc, axis=1, keepdims=True), TINY)
        o_cmp = _dot(pc.astype(BF16), vcmp_scr[:, head(hk)])
        psum = (pc[0:tq] + pc[tq:2 * tq]) + (pc[2 * tq:3 * tq] + pc[3 * tq:4 * tq])
        nsub = tq // LANES
        for u in range(nsub):
            tr_scr[hk * nsub + u] = psum[u * LANES:(u + 1) * LANES, :].T
        pooled = lambda t: ((tr_scr[t, pl.ds(0, n_sel, stride=n_selb), :] + tr_scr[t, pl.ds(1, n_sel, stride=n_selb), :])
                            + (tr_scr[t, pl.ds(2, n_sel, stride=n_selb), :] + tr_scr[t, pl.ds(3, n_sel, stride=n_selb), :]))
        imp = jnp.concatenate([pooled(hk * nsub + u) for u in range(nsub)], axis=1)
        score = jnp.where(blk <= cur, imp + forced * FORCE_BONUS, NEG_INF)
        rank = jnp.zeros((n_sel, tq), F32)
        for kk in range(n_sel):
            row = score[kk:kk + 1, :]
            earlier = jnp.where(blk > kk, 1.0, 0.0)
            rank = rank + jnp.where(row > score, 1.0, jnp.where(row == score, earlier, 0.0))
        dropped = jnp.where(rank < float(SEL_TOPK), 0.0, 1.0)
        pieces = []
        for u in range(nsub):
            tr_scr[hk * nsub + u, 0:n_sel, :] = dropped[:, u * LANES:(u + 1) * LANES]
            tr_scr[hk * nsub + u, n_sel:, :] = jnp.zeros((LANES - n_sel, LANES), F32)
            pieces.append(tr_scr[hk * nsub + u].T)
        notsel = jnp.concatenate(pieces, axis=0).astype(BF16)
        return q4, jnp.concatenate([q4, stack4(notsel)], axis=1), o_cmp

    pre = [prepare(hk) for hk in range(nkv)]

    def sel_step(j, carries, bias4):
        rows = slice(j * tk, (j + 1) * tk)
        out = []
        for hk in range(nkv):
            s2 = _dot_nt(pre[hk][1], ksaug_scr[rows, 2 * hk * HEAD_DIM:(2 * hk + 2) * HEAD_DIM]) * c2
            if bias4 is not None:
                s2 = s2 + bias4
            out.append(_online_step_sum(carries[hk], s2, vs_ref[rows, head(hk)]))
        return tuple(out)

    carries = tuple(_online_init_sum(rows4) for _ in range(nkv))
    for j in range(n_full):
        carries = sel_step(j, carries, None)
    rk = lax.broadcasted_iota(jnp.int32, (tq, tk), 0)
    ck = lax.broadcasted_iota(jnp.int32, (tq, tk), 1)
    causal4 = stack4(jnp.where(n_full * tk + ck <= q0 + rk, 0.0, NEG_INF))
    carries = sel_step(n_full, carries, causal4)

    w0 = pl.multiple_of(jnp.maximum(q0 - WINDOW, 0), tq)
    rw = lax.broadcasted_iota(jnp.int32, (tq, WIN_KEYS), 0)
    cw = lax.broadcasted_iota(jnp.int32, (tq, WIN_KEYS), 1)
    dist = (q0 - w0) + rw - cw
    wbias4 = stack4(jnp.where(jnp.abs(2 * dist - (WINDOW - 1)) <= (WINDOW - 1), 0.0, NEG_INF))
    wrows = pl.ds(w0, WIN_KEYS)

    for hk in range(nkv):
        q4, _, o_cmp = pre[hk]
        o_sel = _online_finish_sum(carries[hk])
        s2 = _dot_nt(q4, kw_ref[wrows, head(hk)]) * c2 + wbias4
        o_win = _online_finish_sum(_online_step_sum(_online_init_sum(rows4), s2, vw_ref[wrows, head(hk)]))
        gate = lambda c: jnp.concatenate(
            [_lane_pick(gl, ZS_GATE0 + 3 * (hk * grp + g) + c) for g in range(grp)], axis=0)
        o4 = gate(0) * o_cmp + gate(1) * o_sel + gate(2) * o_win
        for g in range(grp):
            o_ref[:, head(hk * grp + g)] = o4[g * tq:(g + 1) * tq, :].astype(o_ref.dtype)


def _nsa_attention(zq, zc, zs, posk, posv, w1k, w1v, w2k, w2v, emask, batch, seq):
    t = zq.shape[0]
    tq, nkv = NSA_TQ, NSA_KV_HEADS
    nq = seq // tq
    qw = NSA_HEADS * HEAD_DIM
    kvw = nkv * HEAD_DIM
    kv = lambda col: pl.BlockSpec((seq, kvw), lambda b, i: (b, col // nkv))
    const = lambda shape: pl.BlockSpec(shape, lambda b, i: (0,) * len(shape))
    return pl.pallas_call(
        _nsa_kernel,
        grid=(batch, nq),
        in_specs=[
            pl.BlockSpec((tq, qw), lambda b, i: (b * nq + i, ZQ_NQ // NSA_HEADS)),
            kv(ZQ_KS), kv(ZQ_KW), kv(ZQ_VS), kv(ZQ_VW),
            *[pl.BlockSpec((seq, HEAD_DIM), functools.partial(lambda b, i, h: (b, h), h=h)) for h in range(ZC_HEADS)],
            pl.BlockSpec((tq, LANES), lambda b, i: (b * nq + i, 0)),
            const((CMP_BLOCK, HEAD_DIM)), const((CMP_BLOCK, HEAD_DIM)),
            const((CMP_BLOCK * HEAD_DIM, HEAD_DIM)), const((CMP_BLOCK * HEAD_DIM, HEAD_DIM)),
            const((HEAD_DIM, HEAD_DIM)), const((HEAD_DIM, HEAD_DIM)),
            const((seq, LANES)),
        ],
        out_specs=pl.BlockSpec((tq, qw), lambda b, i: (b * nq + i, 0)),
        out_shape=jax.ShapeDtypeStruct((t, qw), BF16),
        scratch_shapes=[
            pltpu.VMEM((seq, 2 * kvw), BF16),
            pltpu.VMEM((LANES, kvw), BF16),
            pltpu.VMEM((LANES, kvw), BF16),
            pltpu.VMEM((LANES, CMP_BLOCK * HEAD_DIM), BF16),
            pltpu.VMEM((nkv * (tq // LANES), LANES, LANES), F32),
        ],
        compiler_params=_cparams(("arbitrary", "arbitrary")),
        name="nsa_attention",
    )(zq, zq, zq, zq, zq, zc, zc, zc, zc, zs, posk, posv, w1k, w1v, w2k, w2v, emask)


def _out_proj_kernel(of_ref, on_ref, wf_ref, wn_ref, x_ref, mod_ref, o_ref):
    n = o_ref.shape[1]
    for c0 in range(0, n, COL_CHUNK):
        cols = slice(c0, c0 + COL_CHUNK)
        acc = _dot(of_ref[...], wf_ref[:, cols]) + _dot(on_ref[...], wn_ref[:, cols])
        o_ref[:, cols] = x_ref[:, cols] + mod_ref[0, 2:3, cols] * acc


def _out_proj(o_fox, o_nsa, w_fox, w_nsa, x2d, mod, seq):
    t, d = x2d.shape
    tm = ROW_TILE
    tpb = seq // tm
    kf, kn = o_fox.shape[1], o_nsa.shape[1]
    return pl.pallas_call(
        _out_proj_kernel,
        grid=(t // tm,),
        in_specs=[
            pl.BlockSpec((tm, kf), lambda i: (i, 0)),
            pl.BlockSpec((tm, kn), lambda i: (i, 0)),
            pl.BlockSpec((kf, d), lambda i: (0, 0), pipeline_mode=pl.Buffered(1)),
            pl.BlockSpec((kn, d), lambda i: (0, 0), pipeline_mode=pl.Buffered(1)),
            pl.BlockSpec((tm, d), lambda i: (i, 0)),
            pl.BlockSpec((1, 6, d), lambda i: (i // tpb, 0, 0)),
        ],
        out_specs=pl.BlockSpec((tm, d), lambda i: (i, 0)),
        out_shape=jax.ShapeDtypeStruct((t, d), F32),
        compiler_params=_cparams(("arbitrary",)),
        name="out_proj",
    )(o_fox, o_nsa, w_fox, w_nsa, x2d, mod)


def _conv_rows(u, cw_ref, cb_ref):
    y = cb_ref[...] + cw_ref[0:1, :] * pltpu.roll(u, 2, axis=0)
    y = y + cw_ref[1:2, :] * pltpu.roll(u, 1, axis=0)
    return y + cw_ref[2:3, :] * u


def _ffn_kernel(x_ref, xh_ref, mod_ref, g_ref, wg_ref, wv_ref, cwg_ref, cwv_ref, cbg_ref, cbv_ref,
                wd_ref, fg_ref, o_ref, h_scr, acc_scr, *, tiles_per_batch, final_norm):
    i = pl.program_id(0)
    c = pl.program_id(1)
    tm = x_ref.shape[0]

    @pl.when(c == 0)
    def _():
        gain, shift = g_ref[...] * (1.0 + mod_ref[0, 4:5, :]), mod_ref[0, 3:4, :]
        halo = _norm_mod_rows(xh_ref[...], gain, shift)
        first = (i % tiles_per_batch) == 0
        h_scr[0:HALO, :] = jnp.where(first, 0.0, halo).astype(BF16)
        _norm_mod_to_scratch(x_ref, h_scr, HALO, tm, gain, shift)
        acc_scr[...] = jnp.zeros_like(acc_scr)

    h = h_scr[...]
    yg = _conv_rows(_dot(h, wg_ref[...]), cwg_ref, cbg_ref)[HALO:, :]
    yv = _conv_rows(_dot(h, wv_ref[...]), cwv_ref, cbv_ref)[HALO:, :]
    act = (yg * jax.nn.sigmoid(yg)) * yv
    acc_scr[...] += _dot(act.astype(BF16), wd_ref[...])

    @pl.when(c == pl.num_programs(1) - 1)
    def _():
        def body(r, _):
            rows = pl.ds(pl.multiple_of(r * NORM_ROWS, NORM_ROWS), NORM_ROWS)
            y = x_ref[rows, :] + mod_ref[0, 5:6, :] * acc_scr[rows, :]
            if final_norm:
                ms = jnp.mean(y * y, axis=-1, keepdims=True)
                y = (y * lax.rsqrt(ms + EPS)) * fg_ref[...]
            o_ref[rows, :] = y
            return 0
        lax.fori_loop(0, tm // NORM_ROWS, body, 0)


def _ffn(x2d, mod, g, w_up, conv_w, conv_b, w_down, final_g, seq, final_norm):
    t, d = x2d.shape
    dff = w_down.shape[0]
    tm, tf = FFN_TILE, FF_CHUNK
    tpb = seq // tm
    nc = dff // tf
    hb = tm // HALO
    kern = functools.partial(_ffn_kernel, tiles_per_batch=tpb, final_norm=final_norm)
    return pl.pallas_call(
        kern,
        grid=(t // tm, nc),
        in_specs=[
            pl.BlockSpec((tm, d), lambda i, c: (i, 0)),
            pl.BlockSpec((HALO, d), lambda i, c: (jnp.maximum(i * hb - 1, 0), 0)),
            pl.BlockSpec((1, 6, d), lambda i, c: (i // tpb, 0, 0)),
            pl.BlockSpec((1, d), lambda i, c: (0, 0)),
            pl.BlockSpec((d, tf), lambda i, c: (0, c)),
            pl.BlockSpec((d, tf), lambda i, c: (0, nc + c)),
            pl.BlockSpec((CONV_WIDTH, tf), lambda i, c: (0, c)),
            pl.BlockSpec((CONV_WIDTH, tf), lambda i, c: (0, nc + c)),
            pl.BlockSpec((1, tf), lambda i, c: (0, c)),
            pl.BlockSpec((1, tf), lambda i, c: (0, nc + c)),
            pl.BlockSpec((tf, d), lambda i, c: (c, 0)),
            pl.BlockSpec((1, d), lambda i, c: (0, 0)),
        ],
        out_specs=pl.BlockSpec((tm, d), lambda i, c: (i, 0), pipeline_mode=pl.Buffered(1)),
        out_shape=jax.ShapeDtypeStruct((t, d), F32),
        scratch_shapes=[pltpu.VMEM((HALO + tm, d), BF16), pltpu.VMEM((tm, d), F32)],
        compiler_params=_cparams(("arbitrary", "arbitrary")),
        name="conv_ffn",
    )(x2d, x2d, mod, g, w_up, w_up, conv_w, conv_w, conv_b, conv_b, w_down, final_g)


def _rope_tables(seq):
    inv = ROPE_THETA ** (-jnp.arange(0, HEAD_DIM, 2, dtype=F32) / HEAD_DIM)
    ang = jnp.arange(seq, dtype=F32)[:, None] * inv[None, :]
    cos, sin = jnp.cos(ang), jnp.sin(ang)
    return jnp.concatenate([cos, cos], axis=-1), jnp.concatenate([-sin, sin], axis=-1)


def _select_mask_table(seq):
    key_blk = (np.arange(seq) // SEL_BLOCK).reshape(seq, 1)
    return jnp.asarray(np.where(np.arange(LANES).reshape(1, LANES) == key_blk, NEG_INF, 0.0), dtype=BF16)


def _split_w_in(w):
    hd = HEAD_DIM
    sizes = [FOX_HEADS * hd] * 3 + [FOX_HEADS] + [NSA_HEADS * hd] + [NSA_KV_HEADS * hd] * 6 + [3 * NSA_HEADS]
    w = w.astype(BF16)
    fq, fk, fv, ff, nq, kc, vc, ks, vs, kw, vw, ng = jnp.split(w, [int(o) for o in np.cumsum(sizes)[:-1]], axis=-1)
    pad = jnp.zeros((w.shape[0], LANES - FOX_HEADS - 3 * NSA_HEADS), w.dtype)
    return jnp.concatenate([fq, fk, fv, nq, ks, kw, vs, vw, kc, vc, ff, ng, pad], axis=-1)


def kernel(x, c, attn_norm_g, ffn_norm_g, w_ada, b_ada, w_in, b_fgate, cmp_pos_k, cmp_pos_v,
           w_cmp1_k, w_cmp2_k, w_cmp1_v, w_cmp2_v, w_out, w_up, conv_w, conv_b, w_down, final_norm_g):
    batch, seq, d = x.shape
    depth = w_ada.shape[0]
    assert seq % ROW_TILE == 0 and seq % FFN_TILE == 0 and seq % FOX_TQ == 0 and seq % NSA_TK == 0 and seq // SEL_BLOCK == LANES // 4
    assert seq >= WIN_KEYS and w_down.shape[1] % FF_CHUNK == 0 and d % COL_CHUNK == 0

    cosf, sinf = _rope_tables(seq)
    emask = _select_mask_table(seq)
    tri = jnp.asarray(np.tril(np.ones((LANES, LANES))), dtype=BF16)
    mod_all = _ada_mod(c, w_ada, b_ada)

    x2d = x.reshape(batch * seq, d)
    n_fox = FOX_HEADS * HEAD_DIM
    for l in range(depth):
        mod = mod_all[l].reshape(batch, 6, d)
        zq, zc, zs = _in_proj(x2d, mod, attn_norm_g[l].reshape(1, d), _split_w_in(w_in[l]), cosf, sinf, seq)
        bias_row = jnp.zeros((1, LANES), F32).at[0, :FOX_HEADS].set(b_fgate[l])
        cumc = _fgate(zs, bias_row, tri, batch, seq)
        o_fox = _fox_attention(zq, cumc, batch, seq)
        o_nsa = _nsa_attention(zq, zc, zs, cmp_pos_k[l], cmp_pos_v[l],
                               w_cmp1_k[l].astype(BF16), w_cmp1_v[l].astype(BF16),
                               w_cmp2_k[l].astype(BF16), w_cmp2_v[l].astype(BF16), emask, batch, seq)
        w_o = w_out[l].astype(BF16)
        x2d = _out_proj(o_fox, o_nsa, w_o[:n_fox], w_o[n_fox:], x2d, mod, seq)
        x2d = _ffn(x2d, mod, ffn_norm_g[l].reshape(1, d), w_up[l].astype(BF16), conv_w[l],
                   conv_b[l].reshape(1, -1), w_down[l].astype(BF16), final_norm_g.reshape(1, d), seq,
                   final_norm=(l == depth - 1))
    return x2d.reshape(batch, seq, d)
```

```python
import functools
import math

import numpy as np
import jax
import jax.numpy as jnp
from jax import lax
from jax.experimental import pallas as pl
from jax.experimental.pallas import tpu as pltpu

HEAD_DIM = 128
FOX_HEADS = 8
NSA_HEADS = 8
NSA_KV_HEADS = 2
NSA_GROUP = NSA_HEADS // NSA_KV_HEADS
CMP_BLOCK = 32
CMP_STRIDE = 16
SEL_BLOCK = 64
SEL_TOPK = 16
WINDOW = 512
CONV_WIDTH = 3
ROPE_THETA = 10000.0
FORCE_BONUS = 1000.0
NEG_INF = -1e30
TINY = 1e-30
EPS = 1e-6
LOG2E = math.log2(math.e)

LANES = 128
SUBLANES = 8
VMEM_LIMIT_BYTES = 56 * 1024 * 1024

ROW_TILE = 512
FFN_TILE = 1024
COL_CHUNK = 512
NORM_ROWS = 128
NORM_SUBROWS = 16
NORM_UNROLL = 8
FOX_TQ = 512
FOX_GROUP = 2
NSA_TQ = 256
NSA_TK = 512
WIN_KEYS = WINDOW + NSA_TQ
HALO = 2 * SUBLANES
FF_CHUNK = 512

BF16 = jnp.bfloat16
F32 = jnp.float32

ZQ_FQ, ZQ_FK, ZQ_FV, ZQ_NQ, ZQ_KS, ZQ_KW, ZQ_VS, ZQ_VW = 0, 8, 16, 24, 32, 34, 36, 38
ZQ_HEADS = 40
ZQ_ROPED = tuple(range(ZQ_NQ, ZQ_KW + NSA_KV_HEADS))
ZC_HEADS = 4
ZS_GATE0 = FOX_HEADS


def _cparams(sem):
    return pltpu.CompilerParams(dimension_semantics=sem, vmem_limit_bytes=VMEM_LIMIT_BYTES)


def _dot(a, b):
    return jnp.dot(a, b, preferred_element_type=F32)


def _dot_nt(a, b):
    return lax.dot_general(a, b, (((1,), (1,)), ((), ())), preferred_element_type=F32)


def _ada_kernel(c_ref, w_ref, b_ref, o_ref):
    c = c_ref[...]
    ca = (c * jax.nn.sigmoid(c)).astype(BF16)
    o_ref[0] = _dot(ca, w_ref[0].astype(BF16)) + b_ref[0]


def _ada_mod(c, w_ada, b_ada):
    depth, d, n = w_ada.shape
    b = c.shape[0]
    tn = 1024
    return pl.pallas_call(
        _ada_kernel,
        grid=(depth, n // tn),
        in_specs=[
            pl.BlockSpec((b, d), lambda l, j: (0, 0)),
            pl.BlockSpec((1, d, tn), lambda l, j: (l, 0, j)),
            pl.BlockSpec((1, 1, tn), lambda l, j: (l, 0, j)),
        ],
        out_specs=pl.BlockSpec((1, b, tn), lambda l, j: (l, 0, j)),
        out_shape=jax.ShapeDtypeStruct((depth, b, n), F32),
        compiler_params=_cparams(("arbitrary", "arbitrary")),
        name="ada_mod",
    )(c, w_ada, b_ada.reshape(depth, 1, n))


def _norm_mod_rows(x, gain, shift):
    ms = jnp.mean(x * x, axis=-1, keepdims=True)
    return (x * lax.rsqrt(ms + EPS)) * gain + shift


def _norm_mod_to_scratch(x_ref, h_scr, row0, n_rows, gain, shift):
    nr = NORM_SUBROWS
    def body(r, _):
        rows = pl.ds(pl.multiple_of(r * nr, nr), nr)
        dst = pl.ds(pl.multiple_of(row0 + r * nr, nr), nr)
        h_scr[dst, :] = _norm_mod_rows(x_ref[rows, :], gain, shift).astype(BF16)
        return 0
    lax.fori_loop(0, n_rows // nr, body, 0, unroll=NORM_UNROLL)


def _rope_tile(t, cosf, sinf):
    return t * cosf + pltpu.roll(t, HEAD_DIM // 2, axis=1) * sinf


def _in_proj_kernel(x_ref, mod_ref, g_ref, w_ref, cos_ref, sin_ref, zq_ref, zc_ref, zs_ref, h_scr):
    tm = x_ref.shape[0]
    _norm_mod_to_scratch(x_ref, h_scr, 0, tm, g_ref[...] * (1.0 + mod_ref[0, 1:2, :]), mod_ref[0, 0:1, :])
    heads_per_chunk = COL_CHUNK // HEAD_DIM
    for c0 in range(0, ZQ_HEADS, heads_per_chunk):
        acc = _dot(h_scr[...], w_ref[:, c0 * HEAD_DIM:(c0 + heads_per_chunk) * HEAD_DIM])
        for hh in range(heads_per_chunk):
            t = acc[:, hh * HEAD_DIM:(hh + 1) * HEAD_DIM]
            if c0 + hh in ZQ_ROPED:
                t = _rope_tile(t, cos_ref[...], sin_ref[...])
            zq_ref[:, (c0 + hh) * HEAD_DIM:(c0 + hh + 1) * HEAD_DIM] = t.astype(BF16)
    base = ZQ_HEADS * HEAD_DIM
    acc = _dot(h_scr[...], w_ref[:, base:base + ZC_HEADS * HEAD_DIM])
    for hh in range(ZC_HEADS):
        t = acc[:, hh * HEAD_DIM:(hh + 1) * HEAD_DIM]
        if hh < NSA_KV_HEADS:
            t = _rope_tile(t, cos_ref[...], sin_ref[...])
        zc_ref[:, hh * HEAD_DIM:(hh + 1) * HEAD_DIM] = t
    base += ZC_HEADS * HEAD_DIM
    zs_ref[...] = _dot(h_scr[...], w_ref[:, base:base + LANES])


def _in_proj(x2d, mod, g, wcat, cosf, sinf, seq):
    t, d = x2d.shape
    tm = ROW_TILE
    tpb = seq // tm
    nw = wcat.shape[1]
    return pl.pallas_call(
        _in_proj_kernel,
        grid=(t // tm,),
        in_specs=[
            pl.BlockSpec((tm, d), lambda i: (i, 0)),
            pl.BlockSpec((1, 6, d), lambda i: (i // tpb, 0, 0)),
            pl.BlockSpec((1, d), lambda i: (0, 0)),
            pl.BlockSpec((d, nw), lambda i: (0, 0), pipeline_mode=pl.Buffered(1)),
            pl.BlockSpec((tm, HEAD_DIM), lambda i: (i % tpb, 0)),
            pl.BlockSpec((tm, HEAD_DIM), lambda i: (i % tpb, 0)),
        ],
        out_specs=[
            pl.BlockSpec((tm, ZQ_HEADS * HEAD_DIM), lambda i: (i, 0)),
            pl.BlockSpec((tm, ZC_HEADS * HEAD_DIM), lambda i: (i, 0)),
            pl.BlockSpec((tm, LANES), lambda i: (i, 0)),
        ],
        out_shape=[
            jax.ShapeDtypeStruct((t, ZQ_HEADS * HEAD_DIM), BF16),
            jax.ShapeDtypeStruct((t, ZC_HEADS * HEAD_DIM), F32),
            jax.ShapeDtypeStruct((t, LANES), F32),
        ],
        scratch_shapes=[pltpu.VMEM((tm, d), BF16)],
        compiler_params=_cparams(("arbitrary",)),
        name="in_proj",
    )(x2d, mod, g, wcat, cosf, sinf)


def _split3(v):
    hi = v.astype(BF16)
    r1 = v - hi.astype(F32)
    mid = r1.astype(BF16)
    lo = (r1 - mid.astype(F32)).astype(BF16)
    return hi, mid, lo


def _fgate_kernel(zs_ref, bias_ref, tri_ref, cumc_ref):
    seq = zs_ref.shape[0]
    tri = tri_ref[...]
    carry = jnp.zeros((1, LANES), F32)
    for c in range(seq // LANES):
        zf = zs_ref[c * LANES:(c + 1) * LANES, :] + bias_ref[...]
        lf = jnp.minimum(zf, 0.0) - jnp.log1p(jnp.exp(-jnp.abs(zf)))
        hi, mid, lo = _split3(lf)
        cs = (_dot(tri, lo) + _dot(tri, mid)) + _dot(tri, hi) + carry
        carry = cs[LANES - 1:LANES, :]
        cumc_ref[c * LANES:(c + 1) * LANES, :] = cs


def _fgate(zs, bias_row, tri, batch, seq):
    t = zs.shape[0]
    return pl.pallas_call(
        _fgate_kernel,
        grid=(batch,),
        in_specs=[
            pl.BlockSpec((seq, LANES), lambda b: (b, 0)),
            pl.BlockSpec((1, LANES), lambda b: (0, 0)),
            pl.BlockSpec((LANES, LANES), lambda b: (0, 0)),
        ],
        out_specs=pl.BlockSpec((seq, LANES), lambda b: (b, 0)),
        out_shape=jax.ShapeDtypeStruct((t, LANES), F32),
        compiler_params=_cparams(("arbitrary",)),
        name="fgate_cumsum",
    )(zs, bias_row, tri)


def _lane_pick(tile, lane_idx):
    lane = lax.broadcasted_iota(jnp.int32, tile.shape, 1)
    return jnp.sum(jnp.where(lane == lane_idx, tile, 0.0), axis=1, keepdims=True)


def _online_step(carry, s2, v_ones):
    m, acc = carry
    m_new = jnp.maximum(m, jnp.max(s2, axis=1, keepdims=True))
    p = jnp.exp2(s2 - m_new)
    acc = jnp.exp2(m - m_new) * acc + _dot(p.astype(BF16), v_ones)
    return m_new, acc


def _online_init(rows):
    return jnp.full((rows, 1), NEG_INF, F32), jnp.zeros((rows, 2 * HEAD_DIM), F32)


def _online_finish(carry):
    _, acc = carry
    return acc[:, :HEAD_DIM] / acc[:, HEAD_DIM:HEAD_DIM + 1]


def _online_step_sum(carry, s2, v):
    m, l, acc = carry
    m_new = jnp.maximum(m, jnp.max(s2, axis=1, keepdims=True))
    alpha = jnp.exp2(m - m_new)
    p = jnp.exp2(s2 - m_new)
    l = alpha * l + jnp.sum(p, axis=1, keepdims=True)
    acc = alpha * acc + _dot(p.astype(BF16), v)
    return m_new, l, acc


def _online_init_sum(rows):
    return (jnp.full((rows, 1), NEG_INF, F32), jnp.zeros((rows, 1), F32), jnp.zeros((rows, HEAD_DIM), F32))


def _online_finish_sum(carry):
    _, l, acc = carry
    return acc / l


def _ones_lane0(rows):
    return jnp.where(lax.broadcasted_iota(jnp.int32, (rows, LANES), 1) == 0, 1.0, 0.0).astype(BF16)


def _bias_lanes(pieces, head):
    npc = len(pieces)

    def place(n_in):
        row = lax.broadcasted_iota(jnp.int32, (n_in * LANES, 2 * LANES), 0)
        col = lax.broadcasted_iota(jnp.int32, (n_in * LANES, 2 * LANES), 1)
        return row, col
    hi_mid = jnp.concatenate(pieces[:2], axis=1)
    row, col = place(2)
    pc = jnp.where(row >= LANES, 1, 0)
    hit = row - pc * LANES == head
    w_a = jnp.where(hit, jnp.where(col == pc, 1.0, jnp.where(col == LANES + npc + pc, -1.0, 0.0)), 0.0).astype(BF16)
    row, col = place(1)
    w_b = jnp.where(row == head,
                    jnp.where(col == npc - 1, 1.0, jnp.where(col == LANES + 2 * npc - 1, -1.0, 0.0)), 0.0).astype(BF16)
    lane = lax.broadcasted_iota(jnp.int32, (1, 2 * LANES), 1)
    in_range = lambda lo, hi: jnp.where(lane >= lo, jnp.where(lane < hi, 1.0, 0.0), 0.0)
    ones = in_range(npc, 2 * npc) + in_range(LANES, LANES + npc)
    return (_dot(hi_mid, w_a) + _dot(pieces[2], w_b) + ones).astype(BF16)


def _fox_kernel(q_ref, k_ref, v_ref, cum_ref, o_ref, qaug_scr, kaug_scr, vaug_scr):
    hb = pl.program_id(1)
    tile, grp = FOX_TQ, FOX_GROUP
    seq = q_ref.shape[0]
    scale = 1.0 / math.sqrt(HEAD_DIM)
    c2 = scale * LOG2E
    head = lambda h: slice(h * HEAD_DIM, (h + 1) * HEAD_DIM)

    pieces = _split3(cum_ref[...] * (1.0 / scale))
    ones = _ones_lane0(seq)
    for g in range(grp):
        bias = _bias_lanes(pieces, hb * grp + g)
        qaug_scr[:, head(2 * g)] = q_ref[:, head(g)]
        qaug_scr[:, head(2 * g + 1)] = bias[:, :LANES]
        kaug_scr[:, head(2 * g)] = k_ref[:, head(g)]
        kaug_scr[:, head(2 * g + 1)] = bias[:, LANES:]
        vaug_scr[:, head(2 * g)] = v_ref[:, head(g)]
        vaug_scr[:, head(2 * g + 1)] = ones

    causal = (lax.broadcasted_iota(jnp.int32, (tile, tile), 1) <= lax.broadcasted_iota(jnp.int32, (tile, tile), 0))
    for i in range(seq // tile):
        qrows = slice(i * tile, (i + 1) * tile)
        for g in range(grp):
            carry = _online_init(tile)
            for j in range(i + 1):
                krows = slice(j * tile, (j + 1) * tile)
                s2 = _dot_nt(qaug_scr[qrows, 2 * g * HEAD_DIM:(2 * g + 2) * HEAD_DIM],
                             kaug_scr[krows, 2 * g * HEAD_DIM:(2 * g + 2) * HEAD_DIM]) * c2
                if j == i:
                    s2 = jnp.where(causal, s2, NEG_INF)
                carry = _online_step(carry, s2, vaug_scr[krows, 2 * g * HEAD_DIM:(2 * g + 2) * HEAD_DIM])
            o_ref[qrows, head(g)] = _online_finish(carry).astype(o_ref.dtype)


def _fox_attention(zq, cumc, batch, seq):
    t = zq.shape[0]
    grp = FOX_GROUP
    gw = grp * HEAD_DIM
    return pl.pallas_call(
        _fox_kernel,
        grid=(batch, FOX_HEADS // grp),
        in_specs=[
            pl.BlockSpec((seq, gw), lambda b, h: (b, ZQ_FQ // grp + h)),
            pl.BlockSpec((seq, gw), lambda b, h: (b, ZQ_FK // grp + h)),
            pl.BlockSpec((seq, gw), lambda b, h: (b, ZQ_FV // grp + h)),
            pl.BlockSpec((seq, LANES), lambda b, h: (b, 0)),
        ],
        out_specs=pl.BlockSpec((seq, gw), lambda b, h: (b, h)),
        out_shape=jax.ShapeDtypeStruct((t, FOX_HEADS * HEAD_DIM), BF16),
        scratch_shapes=[pltpu.VMEM((seq, 2 * gw), BF16)] * 3,
        compiler_params=_cparams(("arbitrary", "arbitrary")),
        name="fox_attention",
    )(zq, zq, zq, cumc)


def _compress_to(src_ref, hk, pos_ref, w1_ref, w2_ref, flat_scr, dst_scr):
    half = CMP_BLOCK // 2
    n_rows = dst_scr.shape[0]
    cols = slice(hk * HEAD_DIM, (hk + 1) * HEAD_DIM)
    for jp in range(half):
        rows = src_ref[pl.ds(jp, n_rows, stride=CMP_STRIDE), :]
        flat_scr[:, jp * HEAD_DIM:(jp + 1) * HEAD_DIM] = (rows + pos_ref[jp:jp + 1, :]).astype(BF16)
        nxt = pltpu.roll(rows, n_rows - 1, axis=0)
        flat_scr[:, (half + jp) * HEAD_DIM:(half + jp + 1) * HEAD_DIM] = (
            nxt + pos_ref[half + jp:half + jp + 1, :]).astype(BF16)
    pre = _dot(flat_scr[...], w1_ref[...])
    dst_scr[:, cols] = _dot(jax.nn.gelu(pre).astype(BF16), w2_ref[...]).astype(BF16)


def _nsa_kernel(q_ref, ks_ref, kw_ref, vs_ref, vw_ref, kcf0_ref, kcf1_ref, vcf0_ref, vcf1_ref, zs_ref,
                posk_ref, posv_ref, w1k_ref, w1v_ref, w2k_ref, w2v_ref, emask_ref,
                o_ref, ksaug_scr, kcmp_scr, vcmp_scr, flat_scr, tr_scr):
    i = pl.program_id(1)
    tq, tk, grp, nkv = NSA_TQ, NSA_TK, NSA_GROUP, NSA_KV_HEADS
    rows4 = grp * tq
    n_selb = SEL_BLOCK // CMP_STRIDE
    n_sel = LANES // n_selb
    c2 = LOG2E / math.sqrt(HEAD_DIM)
    q0 = i * tq
    head = lambda h: slice(h * HEAD_DIM, (h + 1) * HEAD_DIM)

    @pl.when(i == 0)
    def _():
        for hk in range(nkv):
            ksaug_scr[:, head(2 * hk)] = ks_ref[:, head(hk)]
            ksaug_scr[:, head(2 * hk + 1)] = emask_ref[...]
            _compress_to((kcf0_ref, kcf1_ref)[hk], hk, posk_ref, w1k_ref, w2k_ref, flat_scr, kcmp_scr)
            _compress_to((vcf0_ref, vcf1_ref)[hk], hk, posv_ref, w1v_ref, w2v_ref, flat_scr, vcmp_scr)

    _nsa_tile(q0 // tk, q0, q_ref, kw_ref, vs_ref, vw_ref, zs_ref, o_ref, ksaug_scr, kcmp_scr, vcmp_scr, tr_scr)


def _nsa_tile(n_full, q0, q_ref, kw_ref, vs_ref, vw_ref, zs_ref, o_ref, ksaug_scr, kcmp_scr, vcmp_scr, tr_scr):
    tq, tk, grp, nkv = NSA_TQ, NSA_TK, NSA_GROUP, NSA_KV_HEADS
    rows4 = grp * tq
    n_selb = SEL_BLOCK // CMP_STRIDE
    n_sel = LANES // n_selb
    c2 = LOG2E / math.sqrt(HEAD_DIM)
    head = lambda h: slice(h * HEAD_DIM, (h + 1) * HEAD_DIM)

    def stack4(tile):
        return jnp.concatenate([tile] * grp, axis=0)

    rloc = lax.broadcasted_iota(jnp.int32, (tq, LANES), 0)
    lane = lax.broadcasted_iota(jnp.int32, (tq, LANES), 1)
    cvalid4 = stack4(jnp.where((lane * CMP_STRIDE + (CMP_BLOCK - 1)) <= q0 + rloc, 1.0, 0.0)) > 0.5
    blk = lax.broadcasted_iota(jnp.int32, (n_sel, tq), 0)
    cur = lax.shift_right_logical(q0 + lax.broadcasted_iota(jnp.int32, (n_sel, tq), 1), int(math.log2(SEL_BLOCK)))
    forced = jnp.where(blk == 0, 1.0, jnp.where(blk == cur, 1.0, jnp.where(blk == cur - 1, 1.0, 0.0)))
    gl = jax.nn.sigmoid(zs_ref[...])

    def prepare(hk):
        q4 = jnp.concatenate([q_ref[:, head(hk * grp + g)] for g in range(grp)], axis=0)
        sc = jnp.where(cvalid4, _dot_nt(q4, kcmp_scr[:, head(hk)]) * c2, NEG_INF)
        pc = jnp.where(cvalid4, jnp.exp2(sc - jnp.max(sc, axis=1, keepdims=True)), 0.0)
        pc = pc / jnp.maximum(jnp.sum(pc, axis=1, keepdims=True), TINY)
        o_cmp = _dot(pc.astype(BF16), vcmp_scr[:, head(hk)])
        psum = (pc[0:tq] + pc[tq:2 * tq]) + (pc[2 * tq:3 * tq] + pc[3 * tq:4 * tq])
        nsub = tq // LANES
        for u in range(nsub):
            tr_scr[hk * nsub + u] = psum[u * LANES:(u + 1) * LANES, :].T
        pooled = lambda t: ((tr_scr[t, pl.ds(0, n_sel, stride=n_selb), :] + tr_scr[t, pl.ds(1, n_sel, stride=n_selb), :])
                            + (tr_scr[t, pl.ds(2, n_sel, stride=n_selb), :] + tr_scr[t, pl.ds(3, n_sel, stride=n_selb), :]))
        imp = jnp.concatenate([pooled(hk * nsub + u) for u in range(nsub)], axis=1)
        score = jnp.where(blk <= cur, imp + forced * FORCE_BONUS, NEG_INF)
        rank = jnp.zeros((n_sel, tq), F32)
        for kk in range(n_sel):
            row = score[kk:kk + 1, :]
            earlier = jnp.where(blk > kk, 1.0, 0.0)
            rank = rank + jnp.where(row > score, 1.0, jnp.where(row == score, earlier, 0.0))
        dropped = jnp.where(rank < float(SEL_TOPK), 0.0, 1.0)
        pieces = []
        for u in range(nsub):
            tr_scr[hk * nsub + u, 0:n_sel, :] = dropped[:, u * LANES:(u + 1) * LANES]
            tr_scr[hk * nsub + u, n_sel:, :] = jnp.zeros((LANES - n_sel, LANES), F32)
            pieces.append(tr_scr[hk * nsub + u].T)
        notsel = jnp.concatenate(pieces, axis=0).astype(BF16)
        return q4, jnp.concatenate([q4, stack4(notsel)], axis=1), o_cmp

    pre = [prepare(hk) for hk in range(nkv)]

    def sel_step(j, carries, bias4):
        rows = pl.ds(pl.multiple_of(j * tk, tk), tk)
        out = []
        for hk in range(nkv):
            s2 = _dot_nt(pre[hk][1], ksaug_scr[rows, 2 * hk * HEAD_DIM:(2 * hk + 2) * HEAD_DIM]) * c2
            if bias4 is not None:
                s2 = s2 + bias4
            out.append(_online_step_sum(carries[hk], s2, vs_ref[rows, head(hk)]))
        return tuple(out)

    carries = lax.fori_loop(0, n_full, lambda j, c: sel_step(j, c, None),
                            tuple(_online_init_sum(rows4) for _ in range(nkv)))
    rk = lax.broadcasted_iota(jnp.int32, (tq, tk), 0)
    ck = lax.broadcasted_iota(jnp.int32, (tq, tk), 1)
    causal4 = stack4(jnp.where(n_full * tk + ck <= q0 + rk, 0.0, NEG_INF))
    carries = sel_step(n_full, carries, causal4)

    w0 = pl.multiple_of(jnp.maximum(q0 - WINDOW, 0), tq)
    rw = lax.broadcasted_iota(jnp.int32, (tq, WIN_KEYS), 0)
    cw = lax.broadcasted_iota(jnp.int32, (tq, WIN_KEYS), 1)
    dist = (q0 - w0) + rw - cw
    wbias4 = stack4(jnp.where(jnp.abs(2 * dist - (WINDOW - 1)) <= (WINDOW - 1), 0.0, NEG_INF))
    wrows = pl.ds(w0, WIN_KEYS)

    for hk in range(nkv):
        q4, _, o_cmp = pre[hk]
        o_sel = _online_finish_sum(carries[hk])
        s2 = _dot_nt(q4, kw_ref[wrows, head(hk)]) * c2 + wbias4
        o_win = _online_finish_sum(_online_step_sum(_online_init_sum(rows4), s2, vw_ref[wrows, head(hk)]))
        gate = lambda c: jnp.concatenate(
            [_lane_pick(gl, ZS_GATE0 + 3 * (hk * grp + g) + c) for g in range(grp)], axis=0)
        o4 = gate(0) * o_cmp + gate(1) * o_sel + gate(2) * o_win
        for g in range(grp):
            o_ref[:, head(hk * grp + g)] = o4[g * tq:(g + 1) * tq, :].astype(o_ref.dtype)


def _nsa_attention(zq, zc, zs, posk, posv, w1k, w1v, w2k, w2v, emask, batch, seq):
    t = zq.shape[0]
    tq, nkv = NSA_TQ, NSA_KV_HEADS
    nq = seq // tq
    qw = NSA_HEADS * HEAD_DIM
    kvw = nkv * HEAD_DIM
    kv = lambda col: pl.BlockSpec((seq, kvw), lambda b, i: (b, col // nkv))
    const = lambda shape: pl.BlockSpec(shape, lambda b, i: (0,) * len(shape))
    return pl.pallas_call(
        _nsa_kernel,
        grid=(batch, nq),
        in_specs=[
            pl.BlockSpec((tq, qw), lambda b, i: (b * nq + i, ZQ_NQ // NSA_HEADS)),
            kv(ZQ_KS), kv(ZQ_KW), kv(ZQ_VS), kv(ZQ_VW),
            *[pl.BlockSpec((seq, HEAD_DIM), functools.partial(lambda b, i, h: (b, h), h=h)) for h in range(ZC_HEADS)],
            pl.BlockSpec((tq, LANES), lambda b, i: (b * nq + i, 0)),
            const((CMP_BLOCK, HEAD_DIM)), const((CMP_BLOCK, HEAD_DIM)),
            const((CMP_BLOCK * HEAD_DIM, HEAD_DIM)), const((CMP_BLOCK * HEAD_DIM, HEAD_DIM)),
            const((HEAD_DIM, HEAD_DIM)), const((HEAD_DIM, HEAD_DIM)),
            const((seq, LANES)),
        ],
        out_specs=pl.BlockSpec((tq, qw), lambda b, i: (b * nq + i, 0)),
        out_shape=jax.ShapeDtypeStruct((t, qw), BF16),
        scratch_shapes=[
            pltpu.VMEM((seq, 2 * kvw), BF16),
            pltpu.VMEM((LANES, kvw), BF16),
            pltpu.VMEM((LANES, kvw), BF16),
            pltpu.VMEM((LANES, CMP_BLOCK * HEAD_DIM), BF16),
            pltpu.VMEM((nkv * (tq // LANES), LANES, LANES), F32),
        ],
        compiler_params=_cparams(("arbitrary", "arbitrary")),
        name="nsa_attention",
    )(zq, zq, zq, zq, zq, zc, zc, zc, zc, zs, posk, posv, w1k, w1v, w2k, w2v, emask)


def _out_proj_kernel(of_ref, on_ref, wf_ref, wn_ref, x_ref, mod_ref, o_ref):
    n = o_ref.shape[1]
    for c0 in range(0, n, COL_CHUNK):
        cols = slice(c0, c0 + COL_CHUNK)
        acc = _dot(of_ref[...], wf_ref[:, cols]) + _dot(on_ref[...], wn_ref[:, cols])
        o_ref[:, cols] = x_ref[:, cols] + mod_ref[0, 2:3, cols] * acc


def _out_proj(o_fox, o_nsa, w_fox, w_nsa, x2d, mod, seq):
    t, d = x2d.shape
    tm = ROW_TILE
    tpb = seq // tm
    kf, kn = o_fox.shape[1], o_nsa.shape[1]
    return pl.pallas_call(
        _out_proj_kernel,
        grid=(t // tm,),
        in_specs=[
            pl.BlockSpec((tm, kf), lambda i: (i, 0)),
            pl.BlockSpec((tm, kn), lambda i: (i, 0)),
            pl.BlockSpec((kf, d), lambda i: (0, 0), pipeline_mode=pl.Buffered(1)),
            pl.BlockSpec((kn, d), lambda i: (0, 0), pipeline_mode=pl.Buffered(1)),
            pl.BlockSpec((tm, d), lambda i: (i, 0)),
            pl.BlockSpec((1, 6, d), lambda i: (i // tpb, 0, 0)),
        ],
        out_specs=pl.BlockSpec((tm, d), lambda i: (i, 0)),
        out_shape=jax.ShapeDtypeStruct((t, d), F32),
        compiler_params=_cparams(("arbitrary",)),
        name="out_proj",
    )(o_fox, o_nsa, w_fox, w_nsa, x2d, mod)


def _conv_rows(u, cw_ref, cb_ref):
    y = cb_ref[...] + cw_ref[0:1, :] * pltpu.roll(u, 2, axis=0)
    y = y + cw_ref[1:2, :] * pltpu.roll(u, 1, axis=0)
    return y + cw_ref[2:3, :] * u


def _ffn_kernel(x_ref, xh_ref, mod_ref, g_ref, wg_ref, wv_ref, cwg_ref, cwv_ref, cbg_ref, cbv_ref,
                wd_ref, fg_ref, o_ref, h_scr, acc_scr, *, tiles_per_batch, final_norm):
    i = pl.program_id(0)
    c = pl.program_id(1)
    tm = x_ref.shape[0]

    @pl.when(c == 0)
    def _():
        gain, shift = g_ref[...] * (1.0 + mod_ref[0, 4:5, :]), mod_ref[0, 3:4, :]
        halo = _norm_mod_rows(xh_ref[...], gain, shift)
        first = (i % tiles_per_batch) == 0
        h_scr[0:HALO, :] = jnp.where(first, 0.0, halo).astype(BF16)
        _norm_mod_to_scratch(x_ref, h_scr, HALO, tm, gain, shift)
        acc_scr[...] = jnp.zeros_like(acc_scr)

    h = h_scr[...]
    yg = _conv_rows(_dot(h, wg_ref[...]), cwg_ref, cbg_ref)[HALO:, :]
    yv = _conv_rows(_dot(h, wv_ref[...]), cwv_ref, cbv_ref)[HALO:, :]
    act = (yg * jax.nn.sigmoid(yg)) * yv
    acc_scr[...] += _dot(act.astype(BF16), wd_ref[...])

    @pl.when(c == pl.num_programs(1) - 1)
    def _():
        def body(r, _):
            rows = pl.ds(pl.multiple_of(r * NORM_ROWS, NORM_ROWS), NORM_ROWS)
            y = x_ref[rows, :] + mod_ref[0, 5:6, :] * acc_scr[rows, :]
            if final_norm:
                ms = jnp.mean(y * y, axis=-1, keepdims=True)
                y = (y * lax.rsqrt(ms + EPS)) * fg_ref[...]
            o_ref[rows, :] = y
            return 0
        lax.fori_loop(0, tm // NORM_ROWS, body, 0)


def _ffn(x2d, mod, g, w_up, conv_w, conv_b, w_down, final_g, seq, final_norm):
    t, d = x2d.shape
    dff = w_down.shape[0]
    tm, tf = FFN_TILE, FF_CHUNK
    tpb = seq // tm
    nc = dff // tf
    hb = tm // HALO
    kern = functools.partial(_ffn_kernel, tiles_per_batch=tpb, final_norm=final_norm)
    return pl.pallas_call(
        kern,
        grid=(t // tm, nc),
        in_specs=[
            pl.BlockSpec((tm, d), lambda i, c: (i, 0)),
            pl.BlockSpec((HALO, d), lambda i, c: (jnp.maximum(i * hb - 1, 0), 0)),
            pl.BlockSpec((1, 6, d), lambda i, c: (i // tpb, 0, 0)),
            pl.BlockSpec((1, d), lambda i, c: (0, 0)),
            pl.BlockSpec((d, tf), lambda i, c: (0, c)),
            pl.BlockSpec((d, tf), lambda i, c: (0, nc + c)),
            pl.BlockSpec((CONV_WIDTH, tf), lambda i, c: (0, c)),
            pl.BlockSpec((CONV_WIDTH, tf), lambda i, c: (0, nc + c)),
            pl.BlockSpec((1, tf), lambda i, c: (0, c)),
            pl.BlockSpec((1, tf), lambda i, c: (0, nc + c)),
            pl.BlockSpec((tf, d), lambda i, c: (c, 0)),
            pl.BlockSpec((1, d), lambda i, c: (0, 0)),
        ],
        out_specs=pl.BlockSpec((tm, d), lambda i, c: (i, 0), pipeline_mode=pl.Buffered(1)),
        out_shape=jax.ShapeDtypeStruct((t, d), F32),
        scratch_shapes=[pltpu.VMEM((HALO + tm, d), BF16), pltpu.VMEM((tm, d), F32)],
        compiler_params=_cparams(("arbitrary", "arbitrary")),
        name="conv_ffn",
    )(x2d, x2d, mod, g, w_up, w_up, conv_w, conv_w, conv_b, conv_b, w_down, final_g)


def _rope_tables(seq):
    inv = ROPE_THETA ** (-jnp.arange(0, HEAD_DIM, 2, dtype=F32) / HEAD_DIM)
    ang = jnp.arange(seq, dtype=F32)[:, None] * inv[None, :]
    cos, sin = jnp.cos(ang), jnp.sin(ang)
    return jnp.concatenate([cos, cos], axis=-1), jnp.concatenate([-sin, sin], axis=-1)


def _select_mask_table(seq):
    key_blk = (np.arange(seq) // SEL_BLOCK).reshape(seq, 1)
    return jnp.asarray(np.where(np.arange(LANES).reshape(1, LANES) == key_blk, NEG_INF, 0.0), dtype=BF16)


def _split_w_in(w):
    hd = HEAD_DIM
    sizes = [FOX_HEADS * hd] * 3 + [FOX_HEADS] + [NSA_HEADS * hd] + [NSA_KV_HEADS * hd] * 6 + [3 * NSA_HEADS]
    w = w.astype(BF16)
    fq, fk, fv, ff, nq, kc, vc, ks, vs, kw, vw, ng = jnp.split(w, [int(o) for o in np.cumsum(sizes)[:-1]], axis=-1)
    pad = jnp.zeros((w.shape[0], LANES - FOX_HEADS - 3 * NSA_HEADS), w.dtype)
    return jnp.concatenate([fq, fk, fv, nq, ks, kw, vs, vw, kc, vc, ff, ng, pad], axis=-1)


def kernel(x, c, attn_norm_g, ffn_norm_g, w_ada, b_ada, w_in, b_fgate, cmp_pos_k, cmp_pos_v,
           w_cmp1_k, w_cmp2_k, w_cmp1_v, w_cmp2_v, w_out, w_up, conv_w, conv_b, w_down, final_norm_g):
    batch, seq, d = x.shape
    depth = w_ada.shape[0]
    assert seq % ROW_TILE == 0 and seq % FFN_TILE == 0 and seq % FOX_TQ == 0 and seq % NSA_TK == 0 and seq // SEL_BLOCK == LANES // 4
    assert seq >= WIN_KEYS and w_down.shape[1] % FF_CHUNK == 0 and d % COL_CHUNK == 0

    cosf, sinf = _rope_tables(seq)
    emask = _select_mask_table(seq)
    tri = jnp.asarray(np.tril(np.ones((LANES, LANES))), dtype=BF16)
    mod_all = _ada_mod(c, w_ada, b_ada)

    x2d = x.reshape(batch * seq, d)
    n_fox = FOX_HEADS * HEAD_DIM
    for l in range(depth):
        mod = mod_all[l].reshape(batch, 6, d)
        zq, zc, zs = _in_proj(x2d, mod, attn_norm_g[l].reshape(1, d), _split_w_in(w_in[l]), cosf, sinf, seq)
        bias_row = jnp.zeros((1, LANES), F32).at[0, :FOX_HEADS].set(b_fgate[l])
        cumc = _fgate(zs, bias_row, tri, batch, seq)
        o_fox = _fox_attention(zq, cumc, batch, seq)
        o_nsa = _nsa_attention(zq, zc, zs, cmp_pos_k[l], cmp_pos_v[l],
                               w_cmp1_k[l].astype(BF16), w_cmp1_v[l].astype(BF16),
                               w_cmp2_k[l].astype(BF16), w_cmp2_v[l].astype(BF16), emask, batch, seq)
        w_o = w_out[l].astype(BF16)
        x2d = _out_proj(o_fox, o_nsa, w_o[:n_fox], w_o[n_fox:], x2d, mod, seq)
        x2d = _ffn(x2d, mod, ffn_norm_g[l].reshape(1, d), w_up[l].astype(BF16), conv_w[l],
                   conv_b[l].reshape(1, -1), w_down[l].astype(BF16), final_norm_g.reshape(1, d), seq,
                   final_norm=(l == depth - 1))
    return x2d.reshape(batch, seq, d)
```

```python
import functools
import math

import numpy as np
import jax
import jax.numpy as jnp
from jax import lax
from jax.experimental import pallas as pl
from jax.experimental.pallas import tpu as pltpu

HEAD_DIM = 128
FOX_HEADS = 8
NSA_HEADS = 8
NSA_KV_HEADS = 2
NSA_GROUP = NSA_HEADS // NSA_KV_HEADS
CMP_BLOCK = 32
CMP_STRIDE = 16
SEL_BLOCK = 64
SEL_TOPK = 16
WINDOW = 512
CONV_WIDTH = 3
ROPE_THETA = 10000.0
FORCE_BONUS = 1000.0
NEG_INF = -1e30
TINY = 1e-30
EPS = 1e-6
LOG2E = math.log2(math.e)

LANES = 128
SUBLANES = 8
VMEM_LIMIT_BYTES = 56 * 1024 * 1024

ROW_TILE = 512
FFN_TILE = 1024
COL_CHUNK = 512
NORM_ROWS = 128
NORM_SUBROWS = 16
NORM_UNROLL = 8
FOX_TQ = 512
FOX_GROUP = 2
NSA_TQ = 128
NSA_KV_PER_STEP = 2
NSA_STATIC_SWEEP = True
NSA_TK = 512
WIN_KEYS = WINDOW + NSA_TQ
HALO = 2 * SUBLANES
FF_CHUNK = 512

BF16 = jnp.bfloat16
F32 = jnp.float32

ZQ_FQ, ZQ_FK, ZQ_FV, ZQ_NQ, ZQ_KS, ZQ_KW, ZQ_VS, ZQ_VW = 0, 8, 16, 24, 32, 34, 36, 38
ZQ_HEADS = 40
ZQ_ROPED = tuple(range(ZQ_NQ, ZQ_KW + NSA_KV_HEADS))
ZC_HEADS = 4
ZS_GATE0 = FOX_HEADS


def _cparams(sem):
    return pltpu.CompilerParams(dimension_semantics=sem, vmem_limit_bytes=VMEM_LIMIT_BYTES)


def _dot(a, b):
    return jnp.dot(a, b, preferred_element_type=F32)


def _dot_nt(a, b):
    return lax.dot_general(a, b, (((1,), (1,)), ((), ())), preferred_element_type=F32)


def _ada_kernel(c_ref, w_ref, b_ref, o_ref):
    c = c_ref[...]
    ca = (c * jax.nn.sigmoid(c)).astype(BF16)
    o_ref[0] = _dot(ca, w_ref[0].astype(BF16)) + b_ref[0]


def _ada_mod(c, w_ada, b_ada):
    depth, d, n = w_ada.shape
    b = c.shape[0]
    tn = 1024
    return pl.pallas_call(
        _ada_kernel,
        grid=(depth, n // tn),
        in_specs=[
            pl.BlockSpec((b, d), lambda l, j: (0, 0)),
            pl.BlockSpec((1, d, tn), lambda l, j: (l, 0, j)),
            pl.BlockSpec((1, 1, tn), lambda l, j: (l, 0, j)),
        ],
        out_specs=pl.BlockSpec((1, b, tn), lambda l, j: (l, 0, j)),
        out_shape=jax.ShapeDtypeStruct((depth, b, n), F32),
        compiler_params=_cparams(("arbitrary", "arbitrary")),
        name="ada_mod",
    )(c, w_ada, b_ada.reshape(depth, 1, n))


def _norm_mod_rows(x, gain, shift):
    ms = jnp.mean(x * x, axis=-1, keepdims=True)
    return (x * lax.rsqrt(ms + EPS)) * gain + shift


def _norm_mod_to_scratch(x_ref, h_scr, row0, n_rows, gain, shift):
    nr = NORM_SUBROWS
    def body(r, _):
        rows = pl.ds(pl.multiple_of(r * nr, nr), nr)
        dst = pl.ds(pl.multiple_of(row0 + r * nr, nr), nr)
        h_scr[dst, :] = _norm_mod_rows(x_ref[rows, :], gain, shift).astype(BF16)
        return 0
    lax.fori_loop(0, n_rows // nr, body, 0, unroll=NORM_UNROLL)


def _rope_tile(t, cosf, sinf):
    return t * cosf + pltpu.roll(t, HEAD_DIM // 2, axis=1) * sinf


def _in_proj_kernel(x_ref, mod_ref, g_ref, w_ref, cos_ref, sin_ref, zq_ref, zc_ref, zs_ref, h_scr):
    tm = x_ref.shape[0]
    _norm_mod_to_scratch(x_ref, h_scr, 0, tm, g_ref[...] * (1.0 + mod_ref[0, 1:2, :]), mod_ref[0, 0:1, :])
    heads_per_chunk = COL_CHUNK // HEAD_DIM
    for c0 in range(0, ZQ_HEADS, heads_per_chunk):
        acc = _dot(h_scr[...], w_ref[:, c0 * HEAD_DIM:(c0 + heads_per_chunk) * HEAD_DIM])
        for hh in range(heads_per_chunk):
            t = acc[:, hh * HEAD_DIM:(hh + 1) * HEAD_DIM]
            if c0 + hh in ZQ_ROPED:
                t = _rope_tile(t, cos_ref[...], sin_ref[...])
            zq_ref[:, (c0 + hh) * HEAD_DIM:(c0 + hh + 1) * HEAD_DIM] = t.astype(BF16)
    base = ZQ_HEADS * HEAD_DIM
    acc = _dot(h_scr[...], w_ref[:, base:base + ZC_HEADS * HEAD_DIM])
    for hh in range(ZC_HEADS):
        t = acc[:, hh * HEAD_DIM:(hh + 1) * HEAD_DIM]
        if hh < NSA_KV_HEADS:
            t = _rope_tile(t, cos_ref[...], sin_ref[...])
        zc_ref[:, hh * HEAD_DIM:(hh + 1) * HEAD_DIM] = t
    base += ZC_HEADS * HEAD_DIM
    zs_ref[...] = _dot(h_scr[...], w_ref[:, base:base + LANES])


def _in_proj(x2d, mod, g, wcat, cosf, sinf, seq):
    t, d = x2d.shape
    tm = ROW_TILE
    tpb = seq // tm
    nw = wcat.shape[1]
    return pl.pallas_call(
        _in_proj_kernel,
        grid=(t // tm,),
        in_specs=[
            pl.BlockSpec((tm, d), lambda i: (i, 0)),
            pl.BlockSpec((1, 6, d), lambda i: (i // tpb, 0, 0)),
            pl.BlockSpec((1, d), lambda i: (0, 0)),
            pl.BlockSpec((d, nw), lambda i: (0, 0), pipeline_mode=pl.Buffered(1)),
            pl.BlockSpec((tm, HEAD_DIM), lambda i: (i % tpb, 0)),
            pl.BlockSpec((tm, HEAD_DIM), lambda i: (i % tpb, 0)),
        ],
        out_specs=[
            pl.BlockSpec((tm, ZQ_HEADS * HEAD_DIM), lambda i: (i, 0)),
            pl.BlockSpec((tm, ZC_HEADS * HEAD_DIM), lambda i: (i, 0)),
            pl.BlockSpec((tm, LANES), lambda i: (i, 0)),
        ],
        out_shape=[
            jax.ShapeDtypeStruct((t, ZQ_HEADS * HEAD_DIM), BF16),
            jax.ShapeDtypeStruct((t, ZC_HEADS * HEAD_DIM), F32),
            jax.ShapeDtypeStruct((t, LANES), F32),
        ],
        scratch_shapes=[pltpu.VMEM((tm, d), BF16)],
        compiler_params=_cparams(("arbitrary",)),
        name="in_proj",
    )(x2d, mod, g, wcat, cosf, sinf)


def _split3(v):
    hi = v.astype(BF16)
    r1 = v - hi.astype(F32)
    mid = r1.astype(BF16)
    lo = (r1 - mid.astype(F32)).astype(BF16)
    return hi, mid, lo


def _fgate_kernel(zs_ref, bias_ref, tri_ref, cumc_ref):
    seq = zs_ref.shape[0]
    tri = tri_ref[...]
    carry = jnp.zeros((1, LANES), F32)
    for c in range(seq // LANES):
        zf = zs_ref[c * LANES:(c + 1) * LANES, :] + bias_ref[...]
        lf = jnp.minimum(zf, 0.0) - jnp.log1p(jnp.exp(-jnp.abs(zf)))
        hi, mid, lo = _split3(lf)
        cs = (_dot(tri, lo) + _dot(tri, mid)) + _dot(tri, hi) + carry
        carry = cs[LANES - 1:LANES, :]
        cumc_ref[c * LANES:(c + 1) * LANES, :] = cs


def _fgate(zs, bias_row, tri, batch, seq):
    t = zs.shape[0]
    return pl.pallas_call(
        _fgate_kernel,
        grid=(batch,),
        in_specs=[
            pl.BlockSpec((seq, LANES), lambda b: (b, 0)),
            pl.BlockSpec((1, LANES), lambda b: (0, 0)),
            pl.BlockSpec((LANES, LANES), lambda b: (0, 0)),
        ],
        out_specs=pl.BlockSpec((seq, LANES), lambda b: (b, 0)),
        out_shape=jax.ShapeDtypeStruct((t, LANES), F32),
        compiler_params=_cparams(("arbitrary",)),
        name="fgate_cumsum",
    )(zs, bias_row, tri)


def _lane_pick(tile, lane_idx):
    lane = lax.broadcasted_iota(jnp.int32, tile.shape, 1)
    return jnp.sum(jnp.where(lane == lane_idx, tile, 0.0), axis=1, keepdims=True)


def _online_step(carry, s2, v_ones):
    m, acc = carry
    m_new = jnp.maximum(m, jnp.max(s2, axis=1, keepdims=True))
    p = jnp.exp2(s2 - m_new)
    acc = jnp.exp2(m - m_new) * acc + _dot(p.astype(BF16), v_ones)
    return m_new, acc


def _online_init(rows):
    return jnp.full((rows, 1), NEG_INF, F32), jnp.zeros((rows, 2 * HEAD_DIM), F32)


def _online_finish(carry):
    _, acc = carry
    return acc[:, :HEAD_DIM] / acc[:, HEAD_DIM:HEAD_DIM + 1]


def _online_step_sum(carry, s2, v):
    m, l, acc = carry
    m_new = jnp.maximum(m, jnp.max(s2, axis=1, keepdims=True))
    alpha = jnp.exp2(m - m_new)
    p = jnp.exp2(s2 - m_new)
    l = alpha * l + jnp.sum(p, axis=1, keepdims=True)
    acc = alpha * acc + _dot(p.astype(BF16), v)
    return m_new, l, acc


def _online_init_sum(rows):
    return (jnp.full((rows, 1), NEG_INF, F32), jnp.zeros((rows, 1), F32), jnp.zeros((rows, HEAD_DIM), F32))


def _online_finish_sum(carry):
    _, l, acc = carry
    return acc / l


def _ones_lane0(rows):
    return jnp.where(lax.broadcasted_iota(jnp.int32, (rows, LANES), 1) == 0, 1.0, 0.0).astype(BF16)


def _bias_lanes(pieces, head):
    npc = len(pieces)

    def place(n_in):
        row = lax.broadcasted_iota(jnp.int32, (n_in * LANES, 2 * LANES), 0)
        col = lax.broadcasted_iota(jnp.int32, (n_in * LANES, 2 * LANES), 1)
        return row, col
    hi_mid = jnp.concatenate(pieces[:2], axis=1)
    row, col = place(2)
    pc = jnp.where(row >= LANES, 1, 0)
    hit = row - pc * LANES == head
    w_a = jnp.where(hit, jnp.where(col == pc, 1.0, jnp.where(col == LANES + npc + pc, -1.0, 0.0)), 0.0).astype(BF16)
    row, col = place(1)
    w_b = jnp.where(row == head,
                    jnp.where(col == npc - 1, 1.0, jnp.where(col == LANES + 2 * npc - 1, -1.0, 0.0)), 0.0).astype(BF16)
    lane = lax.broadcasted_iota(jnp.int32, (1, 2 * LANES), 1)
    in_range = lambda lo, hi: jnp.where(lane >= lo, jnp.where(lane < hi, 1.0, 0.0), 0.0)
    ones = in_range(npc, 2 * npc) + in_range(LANES, LANES + npc)
    return (_dot(hi_mid, w_a) + _dot(pieces[2], w_b) + ones).astype(BF16)


def _fox_kernel(q_ref, k_ref, v_ref, cum_ref, o_ref, qaug_scr, kaug_scr, vaug_scr):
    hb = pl.program_id(1)
    tile, grp = FOX_TQ, FOX_GROUP
    seq = q_ref.shape[0]
    scale = 1.0 / math.sqrt(HEAD_DIM)
    c2 = scale * LOG2E
    head = lambda h: slice(h * HEAD_DIM, (h + 1) * HEAD_DIM)

    pieces = _split3(cum_ref[...] * (1.0 / scale))
    ones = _ones_lane0(seq)
    for g in range(grp):
        bias = _bias_lanes(pieces, hb * grp + g)
        qaug_scr[:, head(2 * g)] = q_ref[:, head(g)]
        qaug_scr[:, head(2 * g + 1)] = bias[:, :LANES]
        kaug_scr[:, head(2 * g)] = k_ref[:, head(g)]
        kaug_scr[:, head(2 * g + 1)] = bias[:, LANES:]
        vaug_scr[:, head(2 * g)] = v_ref[:, head(g)]
        vaug_scr[:, head(2 * g + 1)] = ones

    causal = (lax.broadcasted_iota(jnp.int32, (tile, tile), 1) <= lax.broadcasted_iota(jnp.int32, (tile, tile), 0))
    for i in range(seq // tile):
        qrows = slice(i * tile, (i + 1) * tile)
        for g in range(grp):
            carry = _online_init(tile)
            for j in range(i + 1):
                krows = slice(j * tile, (j + 1) * tile)
                s2 = _dot_nt(qaug_scr[qrows, 2 * g * HEAD_DIM:(2 * g + 2) * HEAD_DIM],
                             kaug_scr[krows, 2 * g * HEAD_DIM:(2 * g + 2) * HEAD_DIM]) * c2
                if j == i:
                    s2 = jnp.where(causal, s2, NEG_INF)
                carry = _online_step(carry, s2, vaug_scr[krows, 2 * g * HEAD_DIM:(2 * g + 2) * HEAD_DIM])
            o_ref[qrows, head(g)] = _online_finish(carry).astype(o_ref.dtype)


def _fox_attention(zq, cumc, batch, seq):
    t = zq.shape[0]
    grp = FOX_GROUP
    gw = grp * HEAD_DIM
    return pl.pallas_call(
        _fox_kernel,
        grid=(batch, FOX_HEADS // grp),
        in_specs=[
            pl.BlockSpec((seq, gw), lambda b, h: (b, ZQ_FQ // grp + h)),
            pl.BlockSpec((seq, gw), lambda b, h: (b, ZQ_FK // grp + h)),
            pl.BlockSpec((seq, gw), lambda b, h: (b, ZQ_FV // grp + h)),
            pl.BlockSpec((seq, LANES), lambda b, h: (b, 0)),
        ],
        out_specs=pl.BlockSpec((seq, gw), lambda b, h: (b, h)),
        out_shape=jax.ShapeDtypeStruct((t, FOX_HEADS * HEAD_DIM), BF16),
        scratch_shapes=[pltpu.VMEM((seq, 2 * gw), BF16)] * 3,
        compiler_params=_cparams(("arbitrary", "arbitrary")),
        name="fox_attention",
    )(zq, zq, zq, cumc)


def _compress_to(src_ref, hk, pos_ref, w1_ref, w2_ref, flat_scr, dst_scr):
    half = CMP_BLOCK // 2
    n_rows = dst_scr.shape[0]
    cols = slice(hk * HEAD_DIM, (hk + 1) * HEAD_DIM)
    for jp in range(half):
        rows = src_ref[pl.ds(jp, n_rows, stride=CMP_STRIDE), :]
        flat_scr[:, jp * HEAD_DIM:(jp + 1) * HEAD_DIM] = (rows + pos_ref[jp:jp + 1, :]).astype(BF16)
        nxt = pltpu.roll(rows, n_rows - 1, axis=0)
        flat_scr[:, (half + jp) * HEAD_DIM:(half + jp + 1) * HEAD_DIM] = (
            nxt + pos_ref[half + jp:half + jp + 1, :]).astype(BF16)
    pre = _dot(flat_scr[...], w1_ref[...])
    dst_scr[:, cols] = _dot(jax.nn.gelu(pre).astype(BF16), w2_ref[...]).astype(BF16)


def _nsa_kernel(q_ref, ks_ref, kw_ref, vs_ref, vw_ref, *rest):
    nkv = NSA_KV_PER_STEP
    kcf_refs, vcf_refs = rest[:nkv], rest[nkv:2 * nkv]
    (zs_ref, posk_ref, posv_ref, w1k_ref, w1v_ref, w2k_ref, w2v_ref, emask_ref,
     o_ref, ksaug_scr, kcmp_scr, vcmp_scr, flat_scr, tr_scr) = rest[2 * nkv:]
    hk0 = pl.program_id(1) * nkv
    i = pl.program_id(2)
    tq, tk = NSA_TQ, NSA_TK
    q0 = i * tq
    head = lambda h: slice(h * HEAD_DIM, (h + 1) * HEAD_DIM)

    @pl.when(i == 0)
    def _():
        for hk in range(nkv):
            ksaug_scr[:, head(2 * hk)] = ks_ref[:, head(hk)]
            ksaug_scr[:, head(2 * hk + 1)] = emask_ref[...]
            _compress_to(kcf_refs[hk], hk, posk_ref, w1k_ref, w2k_ref, flat_scr, kcmp_scr)
            _compress_to(vcf_refs[hk], hk, posv_ref, w1v_ref, w2v_ref, flat_scr, vcmp_scr)

    tile = functools.partial(_nsa_tile, q0=q0, hk0=hk0, q_ref=q_ref, kw_ref=kw_ref, vs_ref=vs_ref, vw_ref=vw_ref,
                             zs_ref=zs_ref, o_ref=o_ref, ksaug_scr=ksaug_scr, kcmp_scr=kcmp_scr, vcmp_scr=vcmp_scr,
                             tr_scr=tr_scr)
    if NSA_STATIC_SWEEP:
        for n_full in range(ks_ref.shape[0] // tk):
            pl.when(q0 // tk == n_full)(functools.partial(tile, n_full))
    else:
        tile(q0 // tk)


def _nsa_tile(n_full, *, q0, hk0, q_ref, kw_ref, vs_ref, vw_ref, zs_ref, o_ref, ksaug_scr, kcmp_scr, vcmp_scr, tr_scr):
    tq, tk, grp, nkv = NSA_TQ, NSA_TK, NSA_GROUP, NSA_KV_PER_STEP
    rows4 = grp * tq
    n_selb = SEL_BLOCK // CMP_STRIDE
    n_sel = LANES // n_selb
    c2 = LOG2E / math.sqrt(HEAD_DIM)
    head = lambda h: slice(h * HEAD_DIM, (h + 1) * HEAD_DIM)

    def stack4(tile):
        return jnp.concatenate([tile] * grp, axis=0)

    rloc = lax.broadcasted_iota(jnp.int32, (tq, LANES), 0)
    lane = lax.broadcasted_iota(jnp.int32, (tq, LANES), 1)
    cvalid4 = stack4(jnp.where((lane * CMP_STRIDE + (CMP_BLOCK - 1)) <= q0 + rloc, 1.0, 0.0)) > 0.5
    blk = lax.broadcasted_iota(jnp.int32, (n_sel, tq), 0)
    cur = lax.shift_right_logical(q0 + lax.broadcasted_iota(jnp.int32, (n_sel, tq), 1), int(math.log2(SEL_BLOCK)))
    forced = jnp.where(blk == 0, 1.0, jnp.where(blk == cur, 1.0, jnp.where(blk == cur - 1, 1.0, 0.0)))
    gl = jax.nn.sigmoid(zs_ref[...])

    def prepare(hk):
        q4 = jnp.concatenate([q_ref[:, head(hk * grp + g)] for g in range(grp)], axis=0)
        sc = jnp.where(cvalid4, _dot_nt(q4, kcmp_scr[:, head(hk)]) * c2, NEG_INF)
        pc = jnp.where(cvalid4, jnp.exp2(sc - jnp.max(sc, axis=1, keepdims=True)), 0.0)
        pc = pc / jnp.maximum(jnp.sum(pc, axis=1, keepdims=True), TINY)
        o_cmp = _dot(pc.astype(BF16), vcmp_scr[:, head(hk)])
        psum = (pc[0:tq] + pc[tq:2 * tq]) + (pc[2 * tq:3 * tq] + pc[3 * tq:4 * tq])
        nsub = tq // LANES
        for u in range(nsub):
            tr_scr[hk * nsub + u] = psum[u * LANES:(u + 1) * LANES, :].T
        pooled = lambda t: ((tr_scr[t, pl.ds(0, n_sel, stride=n_selb), :] + tr_scr[t, pl.ds(1, n_sel, stride=n_selb), :])
                            + (tr_scr[t, pl.ds(2, n_sel, stride=n_selb), :] + tr_scr[t, pl.ds(3, n_sel, stride=n_selb), :]))
        imp = jnp.concatenate([pooled(hk * nsub + u) for u in range(nsub)], axis=1)
        score = jnp.where(blk <= cur, imp + forced * FORCE_BONUS, NEG_INF)
        rank = jnp.zeros((n_sel, tq), F32)
        for kk in range(n_sel):
            row = score[kk:kk + 1, :]
            earlier = jnp.where(blk > kk, 1.0, 0.0)
            rank = rank + jnp.where(row > score, 1.0, jnp.where(row == score, earlier, 0.0))
        dropped = jnp.where(rank < float(SEL_TOPK), 0.0, 1.0)
        pieces = []
        for u in range(nsub):
            tr_scr[hk * nsub + u, 0:n_sel, :] = dropped[:, u * LANES:(u + 1) * LANES]
            tr_scr[hk * nsub + u, n_sel:, :] = jnp.zeros((LANES - n_sel, LANES), F32)
            pieces.append(tr_scr[hk * nsub + u].T)
        notsel = jnp.concatenate(pieces, axis=0).astype(BF16)
        return q4, jnp.concatenate([q4, stack4(notsel)], axis=1), o_cmp

    pre = [prepare(hk) for hk in range(nkv)]

    def sel_step(j, carries, bias4):
        rows = slice(j * tk, (j + 1) * tk) if isinstance(j, int) else pl.ds(pl.multiple_of(j * tk, tk), tk)
        out = []
        for hk in range(nkv):
            s2 = _dot_nt(pre[hk][1], ksaug_scr[rows, 2 * hk * HEAD_DIM:(2 * hk + 2) * HEAD_DIM]) * c2
            if bias4 is not None:
                s2 = s2 + bias4
            out.append(_online_step_sum(carries[hk], s2, vs_ref[rows, head(hk)]))
        return tuple(out)

    carries = tuple(_online_init_sum(rows4) for _ in range(nkv))
    if isinstance(n_full, int):
        for j in range(n_full):
            carries = sel_step(j, carries, None)
    else:
        carries = lax.fori_loop(0, n_full, lambda j, c: sel_step(j, c, None), carries)
    rk = lax.broadcasted_iota(jnp.int32, (tq, tk), 0)
    ck = lax.broadcasted_iota(jnp.int32, (tq, tk), 1)
    causal4 = stack4(jnp.where(n_full * tk + ck <= q0 + rk, 0.0, NEG_INF))
    carries = sel_step(n_full, carries, causal4)

    w0 = pl.multiple_of(jnp.maximum(q0 - WINDOW, 0), tq)
    rw = lax.broadcasted_iota(jnp.int32, (tq, WIN_KEYS), 0)
    cw = lax.broadcasted_iota(jnp.int32, (tq, WIN_KEYS), 1)
    dist = (q0 - w0) + rw - cw
    wbias4 = stack4(jnp.where(jnp.abs(2 * dist - (WINDOW - 1)) <= (WINDOW - 1), 0.0, NEG_INF))
    wrows = pl.ds(w0, WIN_KEYS)

    for hk in range(nkv):
        q4, _, o_cmp = pre[hk]
        o_sel = _online_finish_sum(carries[hk])
        s2 = _dot_nt(q4, kw_ref[wrows, head(hk)]) * c2 + wbias4
        o_win = _online_finish_sum(_online_step_sum(_online_init_sum(rows4), s2, vw_ref[wrows, head(hk)]))
        gate = lambda c: jnp.concatenate(
            [_lane_pick(gl, ZS_GATE0 + 3 * ((hk0 + hk) * grp + g) + c) for g in range(grp)], axis=0)
        o4 = gate(0) * o_cmp + gate(1) * o_sel + gate(2) * o_win
        for g in range(grp):
            o_ref[:, head(hk * grp + g)] = o4[g * tq:(g + 1) * tq, :].astype(o_ref.dtype)


def _nsa_attention(zq, zc, zs, posk, posv, w1k, w1v, w2k, w2v, emask, batch, seq):
    t = zq.shape[0]
    tq, nkv = NSA_TQ, NSA_KV_PER_STEP
    nq = seq // tq
    qw = nkv * NSA_GROUP * HEAD_DIM
    kvw = nkv * HEAD_DIM
    kv = lambda col: pl.BlockSpec((seq, kvw), lambda b, h, i: (b, col // nkv + h))
    zc_head = lambda first, u: pl.BlockSpec((seq, HEAD_DIM), lambda b, h, i: (b, first + h * nkv + u))
    const = lambda shape: pl.BlockSpec(shape, lambda b, h, i: (0,) * len(shape))
    return pl.pallas_call(
        _nsa_kernel,
        grid=(batch, NSA_KV_HEADS // nkv, nq),
        in_specs=[
            pl.BlockSpec((tq, qw), lambda b, h, i: (b * nq + i, ZQ_NQ * HEAD_DIM // qw + h)),
            kv(ZQ_KS), kv(ZQ_KW), kv(ZQ_VS), kv(ZQ_VW),
            *[zc_head(0, u) for u in range(nkv)], *[zc_head(NSA_KV_HEADS, u) for u in range(nkv)],
            pl.BlockSpec((tq, LANES), lambda b, h, i: (b * nq + i, 0)),
            const((CMP_BLOCK, HEAD_DIM)), const((CMP_BLOCK, HEAD_DIM)),
            const((CMP_BLOCK * HEAD_DIM, HEAD_DIM)), const((CMP_BLOCK * HEAD_DIM, HEAD_DIM)),
            const((HEAD_DIM, HEAD_DIM)), const((HEAD_DIM, HEAD_DIM)),
            const((seq, LANES)),
        ],
        out_specs=pl.BlockSpec((tq, qw), lambda b, h, i: (b * nq + i, h)),
        out_shape=jax.ShapeDtypeStruct((t, NSA_HEADS * HEAD_DIM), BF16),
        scratch_shapes=[
            pltpu.VMEM((seq, 2 * kvw), BF16),
            pltpu.VMEM((LANES, kvw), BF16),
            pltpu.VMEM((LANES, kvw), BF16),
            pltpu.VMEM((LANES, CMP_BLOCK * HEAD_DIM), BF16),
            pltpu.VMEM((nkv * (tq // LANES), LANES, LANES), F32),
        ],
        compiler_params=_cparams(("arbitrary", "arbitrary", "arbitrary")),
        name="nsa_attention",
    )(zq, zq, zq, zq, zq, *([zc] * (2 * nkv)), zs, posk, posv, w1k, w1v, w2k, w2v, emask)


def _out_proj_kernel(of_ref, on_ref, wf_ref, wn_ref, x_ref, mod_ref, o_ref):
    n = o_ref.shape[1]
    for c0 in range(0, n, COL_CHUNK):
        cols = slice(c0, c0 + COL_CHUNK)
        acc = _dot(of_ref[...], wf_ref[:, cols]) + _dot(on_ref[...], wn_ref[:, cols])
        o_ref[:, cols] = x_ref[:, cols] + mod_ref[0, 2:3, cols] * acc


def _out_proj(o_fox, o_nsa, w_fox, w_nsa, x2d, mod, seq):
    t, d = x2d.shape
    tm = ROW_TILE
    tpb = seq // tm
    kf, kn = o_fox.shape[1], o_nsa.shape[1]
    return pl.pallas_call(
        _out_proj_kernel,
        grid=(t // tm,),
        in_specs=[
            pl.BlockSpec((tm, kf), lambda i: (i, 0)),
            pl.BlockSpec((tm, kn), lambda i: (i, 0)),
            pl.BlockSpec((kf, d), lambda i: (0, 0), pipeline_mode=pl.Buffered(1)),
            pl.BlockSpec((kn, d), lambda i: (0, 0), pipeline_mode=pl.Buffered(1)),
            pl.BlockSpec((tm, d), lambda i: (i, 0)),
            pl.BlockSpec((1, 6, d), lambda i: (i // tpb, 0, 0)),
        ],
        out_specs=pl.BlockSpec((tm, d), lambda i: (i, 0)),
        out_shape=jax.ShapeDtypeStruct((t, d), F32),
        compiler_params=_cparams(("arbitrary",)),
        name="out_proj",
    )(o_fox, o_nsa, w_fox, w_nsa, x2d, mod)


def _conv_rows(u, cw_ref, cb_ref):
    y = cb_ref[...] + cw_ref[0:1, :] * pltpu.roll(u, 2, axis=0)
    y = y + cw_ref[1:2, :] * pltpu.roll(u, 1, axis=0)
    return y + cw_ref[2:3, :] * u


def _ffn_kernel(x_ref, xh_ref, mod_ref, g_ref, wg_ref, wv_ref, cwg_ref, cwv_ref, cbg_ref, cbv_ref,
                wd_ref, fg_ref, o_ref, h_scr, acc_scr, *, tiles_per_batch, final_norm):
    i = pl.program_id(0)
    c = pl.program_id(1)
    tm = x_ref.shape[0]

    @pl.when(c == 0)
    def _():
        gain, shift = g_ref[...] * (1.0 + mod_ref[0, 4:5, :]), mod_ref[0, 3:4, :]
        halo = _norm_mod_rows(xh_ref[...], gain, shift)
        first = (i % tiles_per_batch) == 0
        h_scr[0:HALO, :] = jnp.where(first, 0.0, halo).astype(BF16)
        _norm_mod_to_scratch(x_ref, h_scr, HALO, tm, gain, shift)
        acc_scr[...] = jnp.zeros_like(acc_scr)

    h = h_scr[...]
    yg = _conv_rows(_dot(h, wg_ref[...]), cwg_ref, cbg_ref)[HALO:, :]
    yv = _conv_rows(_dot(h, wv_ref[...]), cwv_ref, cbv_ref)[HALO:, :]
    act = (yg * jax.nn.sigmoid(yg)) * yv
    acc_scr[...] += _dot(act.astype(BF16), wd_ref[...])

    @pl.when(c == pl.num_programs(1) - 1)
    def _():
        def body(r, _):
            rows = pl.ds(pl.multiple_of(r * NORM_ROWS, NORM_ROWS), NORM_ROWS)
            y = x_ref[rows, :] + mod_ref[0, 5:6, :] * acc_scr[rows, :]
            if final_norm:
                ms = jnp.mean(y * y, axis=-1, keepdims=True)
                y = (y * lax.rsqrt(ms + EPS)) * fg_ref[...]
            o_ref[rows, :] = y
            return 0
        lax.fori_loop(0, tm // NORM_ROWS, body, 0)


def _ffn(x2d, mod, g, w_up, conv_w, conv_b, w_down, final_g, seq, final_norm):
    t, d = x2d.shape
    dff = w_down.shape[0]
    tm, tf = FFN_TILE, FF_CHUNK
    tpb = seq // tm
    nc = dff // tf
    hb = tm // HALO
    kern = functools.partial(_ffn_kernel, tiles_per_batch=tpb, final_norm=final_norm)
    return pl.pallas_call(
        kern,
        grid=(t // tm, nc),
        in_specs=[
            pl.BlockSpec((tm, d), lambda i, c: (i, 0)),
            pl.BlockSpec((HALO, d), lambda i, c: (jnp.maximum(i * hb - 1, 0), 0)),
            pl.BlockSpec((1, 6, d), lambda i, c: (i // tpb, 0, 0)),
            pl.BlockSpec((1, d), lambda i, c: (0, 0)),
            pl.BlockSpec((d, tf), lambda i, c: (0, c)),
            pl.BlockSpec((d, tf), lambda i, c: (0, nc + c)),
            pl.BlockSpec((CONV_WIDTH, tf), lambda i, c: (0, c)),
            pl.BlockSpec((CONV_WIDTH, tf), lambda i, c: (0, nc + c)),
            pl.BlockSpec((1, tf), lambda i, c: (0, c)),
            pl.BlockSpec((1, tf), lambda i, c: (0, nc + c)),
            pl.BlockSpec((tf, d), lambda i, c: (c, 0)),
            pl.BlockSpec((1, d), lambda i, c: (0, 0)),
        ],
        out_specs=pl.BlockSpec((tm, d), lambda i, c: (i, 0), pipeline_mode=pl.Buffered(1)),
        out_shape=jax.ShapeDtypeStruct((t, d), F32),
        scratch_shapes=[pltpu.VMEM((HALO + tm, d), BF16), pltpu.VMEM((tm, d), F32)],
        compiler_params=_cparams(("arbitrary", "arbitrary")),
        name="conv_ffn",
    )(x2d, x2d, mod, g, w_up, w_up, conv_w, conv_w, conv_b, conv_b, w_down, final_g)


def _rope_tables(seq):
    inv = ROPE_THETA ** (-jnp.arange(0, HEAD_DIM, 2, dtype=F32) / HEAD_DIM)
    ang = jnp.arange(seq, dtype=F32)[:, None] * inv[None, :]
    cos, sin = jnp.cos(ang), jnp.sin(ang)
    return jnp.concatenate([cos, cos], axis=-1), jnp.concatenate([-sin, sin], axis=-1)


def _select_mask_table(seq):
    key_blk = (np.arange(seq) // SEL_BLOCK).reshape(seq, 1)
    return jnp.asarray(np.where(np.arange(LANES).reshape(1, LANES) == key_blk, NEG_INF, 0.0), dtype=BF16)


def _split_w_in(w):
    hd = HEAD_DIM
    sizes = [FOX_HEADS * hd] * 3 + [FOX_HEADS] + [NSA_HEADS * hd] + [NSA_KV_HEADS * hd] * 6 + [3 * NSA_HEADS]
    w = w.astype(BF16)
    fq, fk, fv, ff, nq, kc, vc, ks, vs, kw, vw, ng = jnp.split(w, [int(o) for o in np.cumsum(sizes)[:-1]], axis=-1)
    pad = jnp.zeros((w.shape[0], LANES - FOX_HEADS - 3 * NSA_HEADS), w.dtype)
    return jnp.concatenate([fq, fk, fv, nq, ks, kw, vs, vw, kc, vc, ff, ng, pad], axis=-1)


def kernel(x, c, attn_norm_g, ffn_norm_g, w_ada, b_ada, w_in, b_fgate, cmp_pos_k, cmp_pos_v,
           w_cmp1_k, w_cmp2_k, w_cmp1_v, w_cmp2_v, w_out, w_up, conv_w, conv_b, w_down, final_norm_g):
    batch, seq, d = x.shape
    depth = w_ada.shape[0]
    assert seq % ROW_TILE == 0 and seq % FFN_TILE == 0 and seq % FOX_TQ == 0 and seq % NSA_TK == 0 and seq // SEL_BLOCK == LANES // 4
    assert seq >= WIN_KEYS and w_down.shape[1] % FF_CHUNK == 0 and d % COL_CHUNK == 0

    cosf, sinf = _rope_tables(seq)
    emask = _select_mask_table(seq)
    tri = jnp.asarray(np.tril(np.ones((LANES, LANES))), dtype=BF16)
    mod_all = _ada_mod(c, w_ada, b_ada)

    x2d = x.reshape(batch * seq, d)
    n_fox = FOX_HEADS * HEAD_DIM
    for l in range(depth):
        mod = mod_all[l].reshape(batch, 6, d)
        zq, zc, zs = _in_proj(x2d, mod, attn_norm_g[l].reshape(1, d), _split_w_in(w_in[l]), cosf, sinf, seq)
        bias_row = jnp.zeros((1, LANES), F32).at[0, :FOX_HEADS].set(b_fgate[l])
        cumc = _fgate(zs, bias_row, tri, batch, seq)
        o_fox = _fox_attention(zq, cumc, batch, seq)
        o_nsa = _nsa_attention(zq, zc, zs, cmp_pos_k[l], cmp_pos_v[l],
                               w_cmp1_k[l].astype(BF16), w_cmp1_v[l].astype(BF16),
                               w_cmp2_k[l].astype(BF16), w_cmp2_v[l].astype(BF16), emask, batch, seq)
        w_o = w_out[l].astype(BF16)
        x2d = _out_proj(o_fox, o_nsa, w_o[:n_fox], w_o[n_fox:], x2d, mod, seq)
        x2d = _ffn(x2d, mod, ffn_norm_g[l].reshape(1, d), w_up[l].astype(BF16), conv_w[l],
                   conv_b[l].reshape(1, -1), w_down[l].astype(BF16), final_norm_g.reshape(1, d), seq,
                   final_norm=(l == depth - 1))
    return x2d.reshape(batch, seq, d)
```

```python
import functools
import math

import numpy as np
import jax
import jax.numpy as jnp
from jax import lax
from jax.experimental import pallas as pl
from jax.experimental.pallas import tpu as pltpu

HEAD_DIM = 128
FOX_HEADS = 8
NSA_HEADS = 8
NSA_KV_HEADS = 2
NSA_GROUP = NSA_HEADS // NSA_KV_HEADS
CMP_BLOCK = 32
CMP_STRIDE = 16
SEL_BLOCK = 64
SEL_TOPK = 16
WINDOW = 512
CONV_WIDTH = 3
ROPE_THETA = 10000.0
FORCE_BONUS = 1000.0
NEG_INF = -1e30
TINY = 1e-30
EPS = 1e-6
LOG2E = math.log2(math.e)

LANES = 128
SUBLANES = 8
VMEM_LIMIT_BYTES = 56 * 1024 * 1024

ROW_TILE = 512
FFN_TILE = 1024
COL_CHUNK = 512
NORM_SUBROWS = 16
NORM_UNROLL = 8
FOX_TQ = 512
FOX_GROUP = 2
NSA_TQ = 128
NSA_KV_PER_STEP = 2
NSA_STATIC_SWEEP = True
NSA_TK = 512
WIN_KEYS = WINDOW + NSA_TQ
HALO = 2 * SUBLANES
FF_CHUNK = 512

BF16 = jnp.bfloat16
F32 = jnp.float32

ZQ_FQ, ZQ_FK, ZQ_FV, ZQ_NQ, ZQ_KS, ZQ_KW, ZQ_VS, ZQ_VW = 0, 8, 16, 24, 32, 34, 36, 38
ZQ_HEADS = 40
ZQ_ROPED = tuple(range(ZQ_NQ, ZQ_KW + NSA_KV_HEADS))
ZC_HEADS = 4
ZS_GATE0 = FOX_HEADS


def _cparams(sem):
    return pltpu.CompilerParams(dimension_semantics=sem, vmem_limit_bytes=VMEM_LIMIT_BYTES)


def _dot(a, b):
    return jnp.dot(a, b, preferred_element_type=F32)


def _dot_nt(a, b):
    return lax.dot_general(a, b, (((1,), (1,)), ((), ())), preferred_element_type=F32)


def _ada_kernel(c_ref, w_ref, b_ref, o_ref):
    c = c_ref[...]
    ca = (c * jax.nn.sigmoid(c)).astype(BF16)
    o_ref[0] = _dot(ca, w_ref[0].astype(BF16)) + b_ref[0]


def _ada_mod(c, w_ada, b_ada):
    depth, d, n = w_ada.shape
    b = c.shape[0]
    tn = 1024
    return pl.pallas_call(
        _ada_kernel,
        grid=(depth, n // tn),
        in_specs=[
            pl.BlockSpec((b, d), lambda l, j: (0, 0)),
            pl.BlockSpec((1, d, tn), lambda l, j: (l, 0, j)),
            pl.BlockSpec((1, 1, tn), lambda l, j: (l, 0, j)),
        ],
        out_specs=pl.BlockSpec((1, b, tn), lambda l, j: (l, 0, j)),
        out_shape=jax.ShapeDtypeStruct((depth, b, n), F32),
        compiler_params=_cparams(("arbitrary", "arbitrary")),
        name="ada_mod",
    )(c, w_ada, b_ada.reshape(depth, 1, n))


def _norm_mod_rows(x, gain, shift):
    ms = jnp.mean(x * x, axis=-1, keepdims=True)
    return (x * lax.rsqrt(ms + EPS)) * gain + shift


def _norm_mod_to_scratch(x_ref, h_scr, row0, n_rows, gain, shift):
    nr = NORM_SUBROWS
    def body(r, _):
        rows = pl.ds(pl.multiple_of(r * nr, nr), nr)
        dst = pl.ds(pl.multiple_of(row0 + r * nr, nr), nr)
        h_scr[dst, :] = _norm_mod_rows(x_ref[rows, :], gain, shift).astype(BF16)
        return 0
    lax.fori_loop(0, n_rows // nr, body, 0, unroll=NORM_UNROLL)


def _rope_tile(t, cosf, sinf):
    return t * cosf + pltpu.roll(t, HEAD_DIM // 2, axis=1) * sinf


def _in_proj_kernel(x_ref, mod_ref, g_ref, w_ref, cos_ref, sin_ref, zq_ref, zc_ref, zs_ref, h_scr):
    tm = x_ref.shape[0]
    _norm_mod_to_scratch(x_ref, h_scr, 0, tm, g_ref[...] * (1.0 + mod_ref[0, 1:2, :]), mod_ref[0, 0:1, :])
    heads_per_chunk = COL_CHUNK // HEAD_DIM
    for c0 in range(0, ZQ_HEADS, heads_per_chunk):
        acc = _dot(h_scr[...], w_ref[:, c0 * HEAD_DIM:(c0 + heads_per_chunk) * HEAD_DIM])
        for hh in range(heads_per_chunk):
            t = acc[:, hh * HEAD_DIM:(hh + 1) * HEAD_DIM]
            if c0 + hh in ZQ_ROPED:
                t = _rope_tile(t, cos_ref[...], sin_ref[...])
            zq_ref[:, (c0 + hh) * HEAD_DIM:(c0 + hh + 1) * HEAD_DIM] = t.astype(BF16)
    base = ZQ_HEADS * HEAD_DIM
    acc = _dot(h_scr[...], w_ref[:, base:base + ZC_HEADS * HEAD_DIM])
    for hh in range(ZC_HEADS):
        t = acc[:, hh * HEAD_DIM:(hh + 1) * HEAD_DIM]
        if hh < NSA_KV_HEADS:
            t = _rope_tile(t, cos_ref[...], sin_ref[...])
        zc_ref[:, hh * HEAD_DIM:(hh + 1) * HEAD_DIM] = t
    base += ZC_HEADS * HEAD_DIM
    zs_ref[...] = _dot(h_scr[...], w_ref[:, base:base + LANES])


def _in_proj(x2d, mod, g, wcat, cosf, sinf, seq):
    t, d = x2d.shape
    tm = ROW_TILE
    tpb = seq // tm
    nw = wcat.shape[1]
    return pl.pallas_call(
        _in_proj_kernel,
        grid=(t // tm,),
        in_specs=[
            pl.BlockSpec((tm, d), lambda i: (i, 0)),
            pl.BlockSpec((1, 6, d), lambda i: (i // tpb, 0, 0)),
            pl.BlockSpec((1, d), lambda i: (0, 0)),
            pl.BlockSpec((d, nw), lambda i: (0, 0), pipeline_mode=pl.Buffered(1)),
            pl.BlockSpec((tm, HEAD_DIM), lambda i: (i % tpb, 0)),
            pl.BlockSpec((tm, HEAD_DIM), lambda i: (i % tpb, 0)),
        ],
        out_specs=[
            pl.BlockSpec((tm, ZQ_HEADS * HEAD_DIM), lambda i: (i, 0)),
            pl.BlockSpec((tm, ZC_HEADS * HEAD_DIM), lambda i: (i, 0)),
            pl.BlockSpec((tm, LANES), lambda i: (i, 0)),
        ],
        out_shape=[
            jax.ShapeDtypeStruct((t, ZQ_HEADS * HEAD_DIM), BF16),
            jax.ShapeDtypeStruct((t, ZC_HEADS * HEAD_DIM), F32),
            jax.ShapeDtypeStruct((t, LANES), F32),
        ],
        scratch_shapes=[pltpu.VMEM((tm, d), BF16)],
        compiler_params=_cparams(("arbitrary",)),
        name="in_proj",
    )(x2d, mod, g, wcat, cosf, sinf)


def _split3(v):
    hi = v.astype(BF16)
    r1 = v - hi.astype(F32)
    mid = r1.astype(BF16)
    lo = (r1 - mid.astype(F32)).astype(BF16)
    return hi, mid, lo


def _fgate_kernel(zs_ref, bias_ref, tri_ref, cumc_ref):
    seq = zs_ref.shape[0]
    tri = tri_ref[...]
    carry = jnp.zeros((1, LANES), F32)
    for c in range(seq // LANES):
        zf = zs_ref[c * LANES:(c + 1) * LANES, :] + bias_ref[...]
        lf = jnp.minimum(zf, 0.0) - jnp.log1p(jnp.exp(-jnp.abs(zf)))
        hi, mid, lo = _split3(lf)
        cs = (_dot(tri, lo) + _dot(tri, mid)) + _dot(tri, hi) + carry
        carry = cs[LANES - 1:LANES, :]
        cumc_ref[c * LANES:(c + 1) * LANES, :] = cs


def _fgate(zs, bias_row, tri, batch, seq):
    t = zs.shape[0]
    return pl.pallas_call(
        _fgate_kernel,
        grid=(batch,),
        in_specs=[
            pl.BlockSpec((seq, LANES), lambda b: (b, 0)),
            pl.BlockSpec((1, LANES), lambda b: (0, 0)),
            pl.BlockSpec((LANES, LANES), lambda b: (0, 0)),
        ],
        out_specs=pl.BlockSpec((seq, LANES), lambda b: (b, 0)),
        out_shape=jax.ShapeDtypeStruct((t, LANES), F32),
        compiler_params=_cparams(("arbitrary",)),
        name="fgate_cumsum",
    )(zs, bias_row, tri)


def _lane_pick(tile, lane_idx):
    lane = lax.broadcasted_iota(jnp.int32, tile.shape, 1)
    return jnp.sum(jnp.where(lane == lane_idx, tile, 0.0), axis=1, keepdims=True)


def _online_step(carry, s2, v_ones):
    m, acc = carry
    m_new = jnp.maximum(m, jnp.max(s2, axis=1, keepdims=True))
    p = jnp.exp2(s2 - m_new)
    acc = jnp.exp2(m - m_new) * acc + _dot(p.astype(BF16), v_ones)
    return m_new, acc


def _online_init(rows):
    return jnp.full((rows, 1), NEG_INF, F32), jnp.zeros((rows, 2 * HEAD_DIM), F32)


def _online_finish(carry):
    _, acc = carry
    return acc[:, :HEAD_DIM] / acc[:, HEAD_DIM:HEAD_DIM + 1]


def _online_step_sum(carry, s2, v):
    m, l, acc = carry
    m_new = jnp.maximum(m, jnp.max(s2, axis=1, keepdims=True))
    alpha = jnp.exp2(m - m_new)
    p = jnp.exp2(s2 - m_new)
    l = alpha * l + jnp.sum(p, axis=1, keepdims=True)
    acc = alpha * acc + _dot(p.astype(BF16), v)
    return m_new, l, acc


def _online_init_sum(rows):
    return (jnp.full((rows, 1), NEG_INF, F32), jnp.zeros((rows, 1), F32), jnp.zeros((rows, HEAD_DIM), F32))


def _online_finish_sum(carry):
    _, l, acc = carry
    return acc / l


def _ones_lane0(rows):
    return jnp.where(lax.broadcasted_iota(jnp.int32, (rows, LANES), 1) == 0, 1.0, 0.0).astype(BF16)


def _bias_lanes(pieces, head):
    npc = len(pieces)

    def place(n_in):
        row = lax.broadcasted_iota(jnp.int32, (n_in * LANES, 2 * LANES), 0)
        col = lax.broadcasted_iota(jnp.int32, (n_in * LANES, 2 * LANES), 1)
        return row, col
    hi_mid = jnp.concatenate(pieces[:2], axis=1)
    row, col = place(2)
    pc = jnp.where(row >= LANES, 1, 0)
    hit = row - pc * LANES == head
    w_a = jnp.where(hit, jnp.where(col == pc, 1.0, jnp.where(col == LANES + npc + pc, -1.0, 0.0)), 0.0).astype(BF16)
    row, col = place(1)
    w_b = jnp.where(row == head,
                    jnp.where(col == npc - 1, 1.0, jnp.where(col == LANES + 2 * npc - 1, -1.0, 0.0)), 0.0).astype(BF16)
    lane = lax.broadcasted_iota(jnp.int32, (1, 2 * LANES), 1)
    in_range = lambda lo, hi: jnp.where(lane >= lo, jnp.where(lane < hi, 1.0, 0.0), 0.0)
    ones = in_range(npc, 2 * npc) + in_range(LANES, LANES + npc)
    return (_dot(hi_mid, w_a) + _dot(pieces[2], w_b) + ones).astype(BF16)


def _fox_kernel(q_ref, k_ref, v_ref, cum_ref, o_ref, qaug_scr, kaug_scr, vaug_scr):
    hb = pl.program_id(1)
    tile, grp = FOX_TQ, FOX_GROUP
    seq = q_ref.shape[0]
    scale = 1.0 / math.sqrt(HEAD_DIM)
    c2 = scale * LOG2E
    head = lambda h: slice(h * HEAD_DIM, (h + 1) * HEAD_DIM)

    pieces = _split3(cum_ref[...] * (1.0 / scale))
    ones = _ones_lane0(seq)
    for g in range(grp):
        bias = _bias_lanes(pieces, hb * grp + g)
        qaug_scr[:, head(2 * g)] = q_ref[:, head(g)]
        qaug_scr[:, head(2 * g + 1)] = bias[:, :LANES]
        kaug_scr[:, head(2 * g)] = k_ref[:, head(g)]
        kaug_scr[:, head(2 * g + 1)] = bias[:, LANES:]
        vaug_scr[:, head(2 * g)] = v_ref[:, head(g)]
        vaug_scr[:, head(2 * g + 1)] = ones

    causal = (lax.broadcasted_iota(jnp.int32, (tile, tile), 1) <= lax.broadcasted_iota(jnp.int32, (tile, tile), 0))
    for i in range(seq // tile):
        qrows = slice(i * tile, (i + 1) * tile)
        for g in range(grp):
            carry = _online_init(tile)
            for j in range(i + 1):
                krows = slice(j * tile, (j + 1) * tile)
                s2 = _dot_nt(qaug_scr[qrows, 2 * g * HEAD_DIM:(2 * g + 2) * HEAD_DIM],
                             kaug_scr[krows, 2 * g * HEAD_DIM:(2 * g + 2) * HEAD_DIM]) * c2
                if j == i:
                    s2 = jnp.where(causal, s2, NEG_INF)
                carry = _online_step(carry, s2, vaug_scr[krows, 2 * g * HEAD_DIM:(2 * g + 2) * HEAD_DIM])
            o_ref[qrows, head(g)] = _online_finish(carry).astype(o_ref.dtype)


def _fox_attention(zq, cumc, batch, seq):
    t = zq.shape[0]
    grp = FOX_GROUP
    gw = grp * HEAD_DIM
    return pl.pallas_call(
        _fox_kernel,
        grid=(batch, FOX_HEADS // grp),
        in_specs=[
            pl.BlockSpec((seq, gw), lambda b, h: (b, ZQ_FQ // grp + h)),
            pl.BlockSpec((seq, gw), lambda b, h: (b, ZQ_FK // grp + h)),
            pl.BlockSpec((seq, gw), lambda b, h: (b, ZQ_FV // grp + h)),
            pl.BlockSpec((seq, LANES), lambda b, h: (b, 0)),
        ],
        out_specs=pl.BlockSpec((seq, gw), lambda b, h: (b, h)),
        out_shape=jax.ShapeDtypeStruct((t, FOX_HEADS * HEAD_DIM), BF16),
        scratch_shapes=[pltpu.VMEM((seq, 2 * gw), BF16)] * 3,
        compiler_params=_cparams(("arbitrary", "arbitrary")),
        name="fox_attention",
    )(zq, zq, zq, cumc)


def _compress_to(src_ref, hk, pos_ref, w1_ref, w2_ref, flat_scr, dst_scr):
    half = CMP_BLOCK // 2
    n_rows = dst_scr.shape[0]
    cols = slice(hk * HEAD_DIM, (hk + 1) * HEAD_DIM)
    for jp in range(half):
        rows = src_ref[pl.ds(jp, n_rows, stride=CMP_STRIDE), :]
        flat_scr[:, jp * HEAD_DIM:(jp + 1) * HEAD_DIM] = (rows + pos_ref[jp:jp + 1, :]).astype(BF16)
        nxt = pltpu.roll(rows, n_rows - 1, axis=0)
        flat_scr[:, (half + jp) * HEAD_DIM:(half + jp + 1) * HEAD_DIM] = (
            nxt + pos_ref[half + jp:half + jp + 1, :]).astype(BF16)
    pre = _dot(flat_scr[...], w1_ref[...])
    dst_scr[:, cols] = _dot(jax.nn.gelu(pre).astype(BF16), w2_ref[...]).astype(BF16)


def _nsa_kernel(q_ref, ks_ref, kw_ref, vs_ref, vw_ref, *rest):
    nkv = NSA_KV_PER_STEP
    kcf_refs, vcf_refs = rest[:nkv], rest[nkv:2 * nkv]
    (zs_ref, posk_ref, posv_ref, w1k_ref, w1v_ref, w2k_ref, w2v_ref, emask_ref,
     o_ref, ksaug_scr, kcmp_scr, vcmp_scr, flat_scr, tr_scr) = rest[2 * nkv:]
    hk0 = pl.program_id(1) * nkv
    i = pl.program_id(2)
    tq, tk = NSA_TQ, NSA_TK
    q0 = i * tq
    head = lambda h: slice(h * HEAD_DIM, (h + 1) * HEAD_DIM)

    @pl.when(i == 0)
    def _():
        for hk in range(nkv):
            ksaug_scr[:, head(2 * hk)] = ks_ref[:, head(hk)]
            ksaug_scr[:, head(2 * hk + 1)] = emask_ref[...]
            _compress_to(kcf_refs[hk], hk, posk_ref, w1k_ref, w2k_ref, flat_scr, kcmp_scr)
            _compress_to(vcf_refs[hk], hk, posv_ref, w1v_ref, w2v_ref, flat_scr, vcmp_scr)

    tile = functools.partial(_nsa_tile, q0=q0, hk0=hk0, q_ref=q_ref, kw_ref=kw_ref, vs_ref=vs_ref, vw_ref=vw_ref,
                             zs_ref=zs_ref, o_ref=o_ref, ksaug_scr=ksaug_scr, kcmp_scr=kcmp_scr, vcmp_scr=vcmp_scr,
                             tr_scr=tr_scr)
    if NSA_STATIC_SWEEP:
        for n_full in range(ks_ref.shape[0] // tk):
            pl.when(q0 // tk == n_full)(functools.partial(tile, n_full))
    else:
        tile(q0 // tk)


def _nsa_tile(n_full, *, q0, hk0, q_ref, kw_ref, vs_ref, vw_ref, zs_ref, o_ref, ksaug_scr, kcmp_scr, vcmp_scr, tr_scr):
    tq, tk, grp, nkv = NSA_TQ, NSA_TK, NSA_GROUP, NSA_KV_PER_STEP
    rows4 = grp * tq
    n_selb = SEL_BLOCK // CMP_STRIDE
    n_sel = LANES // n_selb
    c2 = LOG2E / math.sqrt(HEAD_DIM)
    head = lambda h: slice(h * HEAD_DIM, (h + 1) * HEAD_DIM)

    def stack4(tile):
        return jnp.concatenate([tile] * grp, axis=0)

    rloc = lax.broadcasted_iota(jnp.int32, (tq, LANES), 0)
    lane = lax.broadcasted_iota(jnp.int32, (tq, LANES), 1)
    cvalid4 = stack4(jnp.where((lane * CMP_STRIDE + (CMP_BLOCK - 1)) <= q0 + rloc, 1.0, 0.0)) > 0.5
    blk = lax.broadcasted_iota(jnp.int32, (n_sel, tq), 0)
    cur = lax.shift_right_logical(q0 + lax.broadcasted_iota(jnp.int32, (n_sel, tq), 1), int(math.log2(SEL_BLOCK)))
    forced = jnp.where(blk == 0, 1.0, jnp.where(blk == cur, 1.0, jnp.where(blk == cur - 1, 1.0, 0.0)))
    gl = jax.nn.sigmoid(zs_ref[...])

    def prepare(hk):
        q4 = jnp.concatenate([q_ref[:, head(hk * grp + g)] for g in range(grp)], axis=0)
        sc = jnp.where(cvalid4, _dot_nt(q4, kcmp_scr[:, head(hk)]) * c2, NEG_INF)
        pc = jnp.where(cvalid4, jnp.exp2(sc - jnp.max(sc, axis=1, keepdims=True)), 0.0)
        pc = pc / jnp.maximum(jnp.sum(pc, axis=1, keepdims=True), TINY)
        o_cmp = _dot(pc.astype(BF16), vcmp_scr[:, head(hk)])
        psum = (pc[0:tq] + pc[tq:2 * tq]) + (pc[2 * tq:3 * tq] + pc[3 * tq:4 * tq])
        nsub = tq // LANES
        for u in range(nsub):
            tr_scr[hk * nsub + u] = psum[u * LANES:(u + 1) * LANES, :].T
        pooled = lambda t: ((tr_scr[t, pl.ds(0, n_sel, stride=n_selb), :] + tr_scr[t, pl.ds(1, n_sel, stride=n_selb), :])
                            + (tr_scr[t, pl.ds(2, n_sel, stride=n_selb), :] + tr_scr[t, pl.ds(3, n_sel, stride=n_selb), :]))
        imp = jnp.concatenate([pooled(hk * nsub + u) for u in range(nsub)], axis=1)
        score = jnp.where(blk <= cur, imp + forced * FORCE_BONUS, NEG_INF)
        rank = jnp.zeros((n_sel, tq), F32)
        for kk in range(n_sel):
            row = score[kk:kk + 1, :]
            earlier = jnp.where(blk > kk, 1.0, 0.0)
            rank = rank + jnp.where(row > score, 1.0, jnp.where(row == score, earlier, 0.0))
        dropped = jnp.where(rank < float(SEL_TOPK), 0.0, 1.0)
        pieces = []
        for u in range(nsub):
            tr_scr[hk * nsub + u, 0:n_sel, :] = dropped[:, u * LANES:(u + 1) * LANES]
            tr_scr[hk * nsub + u, n_sel:, :] = jnp.zeros((LANES - n_sel, LANES), F32)
            pieces.append(tr_scr[hk * nsub + u].T)
        notsel = jnp.concatenate(pieces, axis=0).astype(BF16)
        return q4, jnp.concatenate([q4, stack4(notsel)], axis=1), o_cmp

    pre = [prepare(hk) for hk in range(nkv)]

    def sel_step(j, carries, bias4):
        rows = slice(j * tk, (j + 1) * tk) if isinstance(j, int) else pl.ds(pl.multiple_of(j * tk, tk), tk)
        out = []
        for hk in range(nkv):
            s2 = _dot_nt(pre[hk][1], ksaug_scr[rows, 2 * hk * HEAD_DIM:(2 * hk + 2) * HEAD_DIM]) * c2
            if bias4 is not None:
                s2 = s2 + bias4
            out.append(_online_step_sum(carries[hk], s2, vs_ref[rows, head(hk)]))
        return tuple(out)

    carries = tuple(_online_init_sum(rows4) for _ in range(nkv))
    if isinstance(n_full, int):
        for j in range(n_full):
            carries = sel_step(j, carries, None)
    else:
        carries = lax.fori_loop(0, n_full, lambda j, c: sel_step(j, c, None), carries)
    rk = lax.broadcasted_iota(jnp.int32, (tq, tk), 0)
    ck = lax.broadcasted_iota(jnp.int32, (tq, tk), 1)
    causal4 = stack4(jnp.where(n_full * tk + ck <= q0 + rk, 0.0, NEG_INF))
    carries = sel_step(n_full, carries, causal4)

    w0 = pl.multiple_of(jnp.maximum(q0 - WINDOW, 0), tq)
    rw = lax.broadcasted_iota(jnp.int32, (tq, WIN_KEYS), 0)
    cw = lax.broadcasted_iota(jnp.int32, (tq, WIN_KEYS), 1)
    dist = (q0 - w0) + rw - cw
    wbias4 = stack4(jnp.where(jnp.abs(2 * dist - (WINDOW - 1)) <= (WINDOW - 1), 0.0, NEG_INF))
    wrows = pl.ds(w0, WIN_KEYS)

    for hk in range(nkv):
        q4, _, o_cmp = pre[hk]
        o_sel = _online_finish_sum(carries[hk])
        s2 = _dot_nt(q4, kw_ref[wrows, head(hk)]) * c2 + wbias4
        o_win = _online_finish_sum(_online_step_sum(_online_init_sum(rows4), s2, vw_ref[wrows, head(hk)]))
        gate = lambda c: jnp.concatenate(
            [_lane_pick(gl, ZS_GATE0 + 3 * ((hk0 + hk) * grp + g) + c) for g in range(grp)], axis=0)
        o4 = gate(0) * o_cmp + gate(1) * o_sel + gate(2) * o_win
        for g in range(grp):
            o_ref[:, head(hk * grp + g)] = o4[g * tq:(g + 1) * tq, :].astype(o_ref.dtype)


def _nsa_attention(zq, zc, zs, posk, posv, w1k, w1v, w2k, w2v, emask, batch, seq):
    t = zq.shape[0]
    tq, nkv = NSA_TQ, NSA_KV_PER_STEP
    nq = seq // tq
    qw = nkv * NSA_GROUP * HEAD_DIM
    kvw = nkv * HEAD_DIM
    kv = lambda col: pl.BlockSpec((seq, kvw), lambda b, h, i: (b, col // nkv + h))
    zc_head = lambda first, u: pl.BlockSpec((seq, HEAD_DIM), lambda b, h, i: (b, first + h * nkv + u))
    const = lambda shape: pl.BlockSpec(shape, lambda b, h, i: (0,) * len(shape))
    return pl.pallas_call(
        _nsa_kernel,
        grid=(batch, NSA_KV_HEADS // nkv, nq),
        in_specs=[
            pl.BlockSpec((tq, qw), lambda b, h, i: (b * nq + i, ZQ_NQ * HEAD_DIM // qw + h)),
            kv(ZQ_KS), kv(ZQ_KW), kv(ZQ_VS), kv(ZQ_VW),
            *[zc_head(0, u) for u in range(nkv)], *[zc_head(NSA_KV_HEADS, u) for u in range(nkv)],
            pl.BlockSpec((tq, LANES), lambda b, h, i: (b * nq + i, 0)),
            const((CMP_BLOCK, HEAD_DIM)), const((CMP_BLOCK, HEAD_DIM)),
            const((CMP_BLOCK * HEAD_DIM, HEAD_DIM)), const((CMP_BLOCK * HEAD_DIM, HEAD_DIM)),
            const((HEAD_DIM, HEAD_DIM)), const((HEAD_DIM, HEAD_DIM)),
            const((seq, LANES)),
        ],
        out_specs=pl.BlockSpec((tq, qw), lambda b, h, i: (b * nq + i, h)),
        out_shape=jax.ShapeDtypeStruct((t, NSA_HEADS * HEAD_DIM), BF16),
        scratch_shapes=[
            pltpu.VMEM((seq, 2 * kvw), BF16),
            pltpu.VMEM((LANES, kvw), BF16),
            pltpu.VMEM((LANES, kvw), BF16),
            pltpu.VMEM((LANES, CMP_BLOCK * HEAD_DIM), BF16),
            pltpu.VMEM((nkv * (tq // LANES), LANES, LANES), F32),
        ],
        compiler_params=_cparams(("arbitrary", "arbitrary", "arbitrary")),
        name="nsa_attention",
    )(zq, zq, zq, zq, zq, *([zc] * (2 * nkv)), zs, posk, posv, w1k, w1v, w2k, w2v, emask)


def _out_proj_kernel(of_ref, on_ref, wf_ref, wn_ref, x_ref, mod_ref, o_ref):
    n = o_ref.shape[1]
    for c0 in range(0, n, COL_CHUNK):
        cols = slice(c0, c0 + COL_CHUNK)
        acc = _dot(of_ref[...], wf_ref[:, cols]) + _dot(on_ref[...], wn_ref[:, cols])
        o_ref[:, cols] = x_ref[:, cols] + mod_ref[0, 2:3, cols] * acc


def _out_proj(o_fox, o_nsa, w_fox, w_nsa, x2d, mod, seq):
    t, d = x2d.shape
    tm = ROW_TILE
    tpb = seq // tm
    kf, kn = o_fox.shape[1], o_nsa.shape[1]
    return pl.pallas_call(
        _out_proj_kernel,
        grid=(t // tm,),
        in_specs=[
            pl.BlockSpec((tm, kf), lambda i: (i, 0)),
            pl.BlockSpec((tm, kn), lambda i: (i, 0)),
            pl.BlockSpec((kf, d), lambda i: (0, 0), pipeline_mode=pl.Buffered(1)),
            pl.BlockSpec((kn, d), lambda i: (0, 0), pipeline_mode=pl.Buffered(1)),
            pl.BlockSpec((tm, d), lambda i: (i, 0)),
            pl.BlockSpec((1, 6, d), lambda i: (i // tpb, 0, 0)),
        ],
        out_specs=pl.BlockSpec((tm, d), lambda i: (i, 0)),
        out_shape=jax.ShapeDtypeStruct((t, d), F32),
        compiler_params=_cparams(("arbitrary",)),
        name="out_proj",
    )(o_fox, o_nsa, w_fox, w_nsa, x2d, mod)


def _conv_rows(u, cw_ref, cb_ref):
    y = cb_ref[...] + cw_ref[0:1, :] * pltpu.roll(u, 2, axis=0)
    y = y + cw_ref[1:2, :] * pltpu.roll(u, 1, axis=0)
    return y + cw_ref[2:3, :] * u


def _ffn_up_kernel(x_ref, xh_ref, mod_ref, g_ref, wg_ref, wv_ref, cwg_ref, cwv_ref, cbg_ref, cbv_ref,
                   a_ref, h_scr, *, tiles_per_batch):
    i = pl.program_id(0)
    c = pl.program_id(1)
    tm = x_ref.shape[0]

    @pl.when(c == 0)
    def _():
        gain, shift = g_ref[...] * (1.0 + mod_ref[0, 4:5, :]), mod_ref[0, 3:4, :]
        halo = _norm_mod_rows(xh_ref[...], gain, shift)
        first = (i % tiles_per_batch) == 0
        h_scr[0:HALO, :] = jnp.where(first, 0.0, halo).astype(BF16)
        _norm_mod_to_scratch(x_ref, h_scr, HALO, tm, gain, shift)

    h = h_scr[...]
    yg = _conv_rows(_dot(h, wg_ref[...]), cwg_ref, cbg_ref)[HALO:, :]
    yv = _conv_rows(_dot(h, wv_ref[...]), cwv_ref, cbv_ref)[HALO:, :]
    a_ref[...] = ((yg * jax.nn.sigmoid(yg)) * yv).astype(a_ref.dtype)


def _ffn_down_kernel(a_ref, w_ref, x_ref, mod_ref, fg_ref, o_ref, *, final_norm):
    n = o_ref.shape[1]
    sumsq = jnp.zeros((o_ref.shape[0], 1), F32)
    for c0 in range(0, n, COL_CHUNK):
        cols = slice(c0, c0 + COL_CHUNK)
        y = x_ref[:, cols] + mod_ref[0, 5:6, cols] * _dot(a_ref[...], w_ref[:, cols])
        o_ref[:, cols] = y
        if final_norm:
            sumsq = sumsq + jnp.sum(y * y, axis=1, keepdims=True)
    if final_norm:
        r = lax.rsqrt(sumsq * (1.0 / n) + EPS)
        for c0 in range(0, n, COL_CHUNK):
            cols = slice(c0, c0 + COL_CHUNK)
            o_ref[:, cols] = (o_ref[:, cols] * r) * fg_ref[:, cols]


def _ffn(x2d, mod, g, w_up, conv_w, conv_b, w_down, final_g, seq, final_norm):
    t, d = x2d.shape
    dff = w_down.shape[0]
    tm, tf = FFN_TILE, FF_CHUNK
    tpb = seq // tm
    nc = dff // tf
    hb = tm // HALO
    act = pl.pallas_call(
        functools.partial(_ffn_up_kernel, tiles_per_batch=tpb),
        grid=(t // tm, nc),
        in_specs=[
            pl.BlockSpec((tm, d), lambda i, c: (i, 0)),
            pl.BlockSpec((HALO, d), lambda i, c: (jnp.maximum(i * hb - 1, 0), 0)),
            pl.BlockSpec((1, 6, d), lambda i, c: (i // tpb, 0, 0)),
            pl.BlockSpec((1, d), lambda i, c: (0, 0)),
            pl.BlockSpec((d, tf), lambda i, c: (0, c)),
            pl.BlockSpec((d, tf), lambda i, c: (0, nc + c)),
            pl.BlockSpec((CONV_WIDTH, tf), lambda i, c: (0, c)),
            pl.BlockSpec((CONV_WIDTH, tf), lambda i, c: (0, nc + c)),
            pl.BlockSpec((1, tf), lambda i, c: (0, c)),
            pl.BlockSpec((1, tf), lambda i, c: (0, nc + c)),
        ],
        out_specs=pl.BlockSpec((tm, tf), lambda i, c: (i, c)),
        out_shape=jax.ShapeDtypeStruct((t, dff), BF16),
        scratch_shapes=[pltpu.VMEM((HALO + tm, d), BF16)],
        compiler_params=_cparams(("arbitrary", "arbitrary")),
        name="ffn_up",
    )(x2d, x2d, mod, g, w_up, w_up, conv_w, conv_w, conv_b, conv_b)

    tr = ROW_TILE
    rpb = seq // tr
    return pl.pallas_call(
        functools.partial(_ffn_down_kernel, final_norm=final_norm),
        grid=(t // tr,),
        in_specs=[
            pl.BlockSpec((tr, dff), lambda i: (i, 0)),
            pl.BlockSpec((dff, d), lambda i: (0, 0), pipeline_mode=pl.Buffered(1)),
            pl.BlockSpec((tr, d), lambda i: (i, 0)),
            pl.BlockSpec((1, 6, d), lambda i: (i // rpb, 0, 0)),
            pl.BlockSpec((1, d), lambda i: (0, 0)),
        ],
        out_specs=pl.BlockSpec((tr, d), lambda i: (i, 0)),
        out_shape=jax.ShapeDtypeStruct((t, d), F32),
        compiler_params=_cparams(("arbitrary",)),
        name="ffn_down",
    )(act, w_down, x2d, mod, final_g)


def _rope_tables(seq):
    inv = ROPE_THETA ** (-jnp.arange(0, HEAD_DIM, 2, dtype=F32) / HEAD_DIM)
    ang = jnp.arange(seq, dtype=F32)[:, None] * inv[None, :]
    cos, sin = jnp.cos(ang), jnp.sin(ang)
    return jnp.concatenate([cos, cos], axis=-1), jnp.concatenate([-sin, sin], axis=-1)


def _select_mask_table(seq):
    key_blk = (np.arange(seq) // SEL_BLOCK).reshape(seq, 1)
    return jnp.asarray(np.where(np.arange(LANES).reshape(1, LANES) == key_blk, NEG_INF, 0.0), dtype=BF16)


def _split_w_in(w):
    hd = HEAD_DIM
    sizes = [FOX_HEADS * hd] * 3 + [FOX_HEADS] + [NSA_HEADS * hd] + [NSA_KV_HEADS * hd] * 6 + [3 * NSA_HEADS]
    w = w.astype(BF16)
    fq, fk, fv, ff, nq, kc, vc, ks, vs, kw, vw, ng = jnp.split(w, [int(o) for o in np.cumsum(sizes)[:-1]], axis=-1)
    pad = jnp.zeros((w.shape[0], LANES - FOX_HEADS - 3 * NSA_HEADS), w.dtype)
    return jnp.concatenate([fq, fk, fv, nq, ks, kw, vs, vw, kc, vc, ff, ng, pad], axis=-1)


def kernel(x, c, attn_norm_g, ffn_norm_g, w_ada, b_ada, w_in, b_fgate, cmp_pos_k, cmp_pos_v,
           w_cmp1_k, w_cmp2_k, w_cmp1_v, w_cmp2_v, w_out, w_up, conv_w, conv_b, w_down, final_norm_g):
    batch, seq, d = x.shape
    depth = w_ada.shape[0]
    assert seq % ROW_TILE == 0 and seq % FFN_TILE == 0 and seq % FOX_TQ == 0 and seq % NSA_TK == 0 and seq // SEL_BLOCK == LANES // 4
    assert seq >= WIN_KEYS and w_down.shape[1] % FF_CHUNK == 0 and d % COL_CHUNK == 0

    cosf, sinf = _rope_tables(seq)
    emask = _select_mask_table(seq)
    tri = jnp.asarray(np.tril(np.ones((LANES, LANES))), dtype=BF16)
    mod_all = _ada_mod(c, w_ada, b_ada)

    x2d = x.reshape(batch * seq, d)
    n_fox = FOX_HEADS * HEAD_DIM
    for l in range(depth):
        mod = mod_all[l].reshape(batch, 6, d)
        zq, zc, zs = _in_proj(x2d, mod, attn_norm_g[l].reshape(1, d), _split_w_in(w_in[l]), cosf, sinf, seq)
        bias_row = jnp.zeros((1, LANES), F32).at[0, :FOX_HEADS].set(b_fgate[l])
        cumc = _fgate(zs, bias_row, tri, batch, seq)
        o_fox = _fox_attention(zq, cumc, batch, seq)
        o_nsa = _nsa_attention(zq, zc, zs, cmp_pos_k[l], cmp_pos_v[l],
                               w_cmp1_k[l].astype(BF16), w_cmp1_v[l].astype(BF16),
                               w_cmp2_k[l].astype(BF16), w_cmp2_v[l].astype(BF16), emask, batch, seq)
        w_o = w_out[l].astype(BF16)
        x2d = _out_proj(o_fox, o_nsa, w_o[:n_fox], w_o[n_fox:], x2d, mod, seq)
        x2d = _ffn(x2d, mod, ffn_norm_g[l].reshape(1, d), w_up[l].astype(BF16), conv_w[l],
                   conv_b[l].reshape(1, -1), w_down[l].astype(BF16), final_norm_g.reshape(1, d), seq,
                   final_norm=(l == depth - 1))
    return x2d.reshape(batch, seq, d)
```

```python
import functools
import math

import numpy as np
import jax
import jax.numpy as jnp
from jax import lax
from jax.experimental import pallas as pl
from jax.experimental.pallas import tpu as pltpu

HEAD_DIM = 128
FOX_HEADS = 8
NSA_HEADS = 8
NSA_KV_HEADS = 2
NSA_GROUP = NSA_HEADS // NSA_KV_HEADS
CMP_BLOCK = 32
CMP_STRIDE = 16
SEL_BLOCK = 64
SEL_TOPK = 16
WINDOW = 512
CONV_WIDTH = 3
ROPE_THETA = 10000.0
FORCE_BONUS = 1000.0
NEG_INF = -1e30
TINY = 1e-30
EPS = 1e-6
LOG2E = math.log2(math.e)

LANES = 128
SUBLANES = 8
VMEM_LIMIT_BYTES = 56 * 1024 * 1024

ROW_TILE = 512
FFN_TILE = 1024
COL_CHUNK = 512
NORM_SUBROWS = 16
NORM_UNROLL = 8
FOX_TQ = 512
FOX_GROUP = 2
NSA_TQ = 128
NSA_SEL_TQ = 512
NSA_KV_PER_STEP = 2
NSA_STATIC_SWEEP = True
NSA_TK = 512
WIN_KEYS = WINDOW + NSA_TQ
HALO = 2 * SUBLANES
FF_CHUNK = 512

BF16 = jnp.bfloat16
F32 = jnp.float32

ZQ_FQ, ZQ_FK, ZQ_FV, ZQ_NQ, ZQ_KS, ZQ_KW, ZQ_VS, ZQ_VW = 0, 8, 16, 24, 32, 34, 36, 38
ZQ_HEADS = 40
ZQ_ROPED = tuple(range(ZQ_NQ, ZQ_KW + NSA_KV_HEADS))
ZC_HEADS = 4
ZS_GATE0 = FOX_HEADS


def _cparams(sem):
    return pltpu.CompilerParams(dimension_semantics=sem, vmem_limit_bytes=VMEM_LIMIT_BYTES)


def _dot(a, b):
    return jnp.dot(a, b, preferred_element_type=F32)


def _dot_nt(a, b):
    return lax.dot_general(a, b, (((1,), (1,)), ((), ())), preferred_element_type=F32)


def _ada_kernel(c_ref, w_ref, b_ref, o_ref):
    c = c_ref[...]
    ca = (c * jax.nn.sigmoid(c)).astype(BF16)
    o_ref[0] = _dot(ca, w_ref[0].astype(BF16)) + b_ref[0]


def _ada_mod(c, w_ada, b_ada):
    depth, d, n = w_ada.shape
    b = c.shape[0]
    tn = 1024
    return pl.pallas_call(
        _ada_kernel,
        grid=(depth, n // tn),
        in_specs=[
            pl.BlockSpec((b, d), lambda l, j: (0, 0)),
            pl.BlockSpec((1, d, tn), lambda l, j: (l, 0, j)),
            pl.BlockSpec((1, 1, tn), lambda l, j: (l, 0, j)),
        ],
        out_specs=pl.BlockSpec((1, b, tn), lambda l, j: (l, 0, j)),
        out_shape=jax.ShapeDtypeStruct((depth, b, n), F32),
        compiler_params=_cparams(("arbitrary", "arbitrary")),
        name="ada_mod",
    )(c, w_ada, b_ada.reshape(depth, 1, n))


def _norm_mod_rows(x, gain, shift):
    ms = jnp.mean(x * x, axis=-1, keepdims=True)
    return (x * lax.rsqrt(ms + EPS)) * gain + shift


def _norm_mod_to_scratch(x_ref, h_scr, row0, n_rows, gain, shift):
    nr = NORM_SUBROWS
    def body(r, _):
        rows = pl.ds(pl.multiple_of(r * nr, nr), nr)
        dst = pl.ds(pl.multiple_of(row0 + r * nr, nr), nr)
        h_scr[dst, :] = _norm_mod_rows(x_ref[rows, :], gain, shift).astype(BF16)
        return 0
    lax.fori_loop(0, n_rows // nr, body, 0, unroll=NORM_UNROLL)


def _rope_tile(t, cosf, sinf):
    return t * cosf + pltpu.roll(t, HEAD_DIM // 2, axis=1) * sinf


def _in_proj_kernel(x_ref, mod_ref, g_ref, w_ref, cos_ref, sin_ref, zq_ref, zc_ref, zs_ref, h_scr):
    tm = x_ref.shape[0]
    _norm_mod_to_scratch(x_ref, h_scr, 0, tm, g_ref[...] * (1.0 + mod_ref[0, 1:2, :]), mod_ref[0, 0:1, :])
    heads_per_chunk = COL_CHUNK // HEAD_DIM
    for c0 in range(0, ZQ_HEADS, heads_per_chunk):
        acc = _dot(h_scr[...], w_ref[:, c0 * HEAD_DIM:(c0 + heads_per_chunk) * HEAD_DIM])
        for hh in range(heads_per_chunk):
            t = acc[:, hh * HEAD_DIM:(hh + 1) * HEAD_DIM]
            if c0 + hh in ZQ_ROPED:
                t = _rope_tile(t, cos_ref[...], sin_ref[...])
            zq_ref[:, (c0 + hh) * HEAD_DIM:(c0 + hh + 1) * HEAD_DIM] = t.astype(BF16)
    base = ZQ_HEADS * HEAD_DIM
    acc = _dot(h_scr[...], w_ref[:, base:base + ZC_HEADS * HEAD_DIM])
    for hh in range(ZC_HEADS):
        t = acc[:, hh * HEAD_DIM:(hh + 1) * HEAD_DIM]
        if hh < NSA_KV_HEADS:
            t = _rope_tile(t, cos_ref[...], sin_ref[...])
        zc_ref[:, hh * HEAD_DIM:(hh + 1) * HEAD_DIM] = t
    base += ZC_HEADS * HEAD_DIM
    zs_ref[...] = _dot(h_scr[...], w_ref[:, base:base + LANES])


def _in_proj(x2d, mod, g, wcat, cosf, sinf, seq):
    t, d = x2d.shape
    tm = ROW_TILE
    tpb = seq // tm
    nw = wcat.shape[1]
    return pl.pallas_call(
        _in_proj_kernel,
        grid=(t // tm,),
        in_specs=[
            pl.BlockSpec((tm, d), lambda i: (i, 0)),
            pl.BlockSpec((1, 6, d), lambda i: (i // tpb, 0, 0)),
            pl.BlockSpec((1, d), lambda i: (0, 0)),
            pl.BlockSpec((d, nw), lambda i: (0, 0), pipeline_mode=pl.Buffered(1)),
            pl.BlockSpec((tm, HEAD_DIM), lambda i: (i % tpb, 0)),
            pl.BlockSpec((tm, HEAD_DIM), lambda i: (i % tpb, 0)),
        ],
        out_specs=[
            pl.BlockSpec((tm, ZQ_HEADS * HEAD_DIM), lambda i: (i, 0)),
            pl.BlockSpec((tm, ZC_HEADS * HEAD_DIM), lambda i: (i, 0)),
            pl.BlockSpec((tm, LANES), lambda i: (i, 0)),
        ],
        out_shape=[
            jax.ShapeDtypeStruct((t, ZQ_HEADS * HEAD_DIM), BF16),
            jax.ShapeDtypeStruct((t, ZC_HEADS * HEAD_DIM), F32),
            jax.ShapeDtypeStruct((t, LANES), F32),
        ],
        scratch_shapes=[pltpu.VMEM((tm, d), BF16)],
        compiler_params=_cparams(("arbitrary",)),
        name="in_proj",
    )(x2d, mod, g, wcat, cosf, sinf)


def _split3(v):
    hi = v.astype(BF16)
    r1 = v - hi.astype(F32)
    mid = r1.astype(BF16)
    lo = (r1 - mid.astype(F32)).astype(BF16)
    return hi, mid, lo


def _fgate_kernel(zs_ref, bias_ref, tri_ref, cumc_ref):
    seq = zs_ref.shape[0]
    tri = tri_ref[...]
    carry = jnp.zeros((1, LANES), F32)
    for c in range(seq // LANES):
        zf = zs_ref[c * LANES:(c + 1) * LANES, :] + bias_ref[...]
        lf = jnp.minimum(zf, 0.0) - jnp.log1p(jnp.exp(-jnp.abs(zf)))
        hi, mid, lo = _split3(lf)
        cs = (_dot(tri, lo) + _dot(tri, mid)) + _dot(tri, hi) + carry
        carry = cs[LANES - 1:LANES, :]
        cumc_ref[c * LANES:(c + 1) * LANES, :] = cs


def _fgate(zs, bias_row, tri, batch, seq):
    t = zs.shape[0]
    return pl.pallas_call(
        _fgate_kernel,
        grid=(batch,),
        in_specs=[
            pl.BlockSpec((seq, LANES), lambda b: (b, 0)),
            pl.BlockSpec((1, LANES), lambda b: (0, 0)),
            pl.BlockSpec((LANES, LANES), lambda b: (0, 0)),
        ],
        out_specs=pl.BlockSpec((seq, LANES), lambda b: (b, 0)),
        out_shape=jax.ShapeDtypeStruct((t, LANES), F32),
        compiler_params=_cparams(("arbitrary",)),
        name="fgate_cumsum",
    )(zs, bias_row, tri)


def _lane_pick(tile, lane_idx):
    lane = lax.broadcasted_iota(jnp.int32, tile.shape, 1)
    return jnp.sum(jnp.where(lane == lane_idx, tile, 0.0), axis=1, keepdims=True)


def _online_step(carry, s2, v_ones):
    m, acc = carry
    m_new = jnp.maximum(m, jnp.max(s2, axis=1, keepdims=True))
    p = jnp.exp2(s2 - m_new)
    acc = jnp.exp2(m - m_new) * acc + _dot(p.astype(BF16), v_ones)
    return m_new, acc


def _online_init(rows):
    return jnp.full((rows, 1), NEG_INF, F32), jnp.zeros((rows, 2 * HEAD_DIM), F32)


def _online_finish(carry):
    _, acc = carry
    return acc[:, :HEAD_DIM] / acc[:, HEAD_DIM:HEAD_DIM + 1]


def _online_step_sum(carry, s2, v):
    m, l, acc = carry
    m_new = jnp.maximum(m, jnp.max(s2, axis=1, keepdims=True))
    alpha = jnp.exp2(m - m_new)
    p = jnp.exp2(s2 - m_new)
    l = alpha * l + jnp.sum(p, axis=1, keepdims=True)
    acc = alpha * acc + _dot(p.astype(BF16), v)
    return m_new, l, acc


def _online_init_sum(rows):
    return (jnp.full((rows, 1), NEG_INF, F32), jnp.zeros((rows, 1), F32), jnp.zeros((rows, HEAD_DIM), F32))


def _online_finish_sum(carry):
    _, l, acc = carry
    return acc / l


def _ones_lane0(rows):
    return jnp.where(lax.broadcasted_iota(jnp.int32, (rows, LANES), 1) == 0, 1.0, 0.0).astype(BF16)


def _bias_lanes(pieces, head):
    npc = len(pieces)

    def place(n_in):
        row = lax.broadcasted_iota(jnp.int32, (n_in * LANES, 2 * LANES), 0)
        col = lax.broadcasted_iota(jnp.int32, (n_in * LANES, 2 * LANES), 1)
        return row, col
    hi_mid = jnp.concatenate(pieces[:2], axis=1)
    row, col = place(2)
    pc = jnp.where(row >= LANES, 1, 0)
    hit = row - pc * LANES == head
    w_a = jnp.where(hit, jnp.where(col == pc, 1.0, jnp.where(col == LANES + npc + pc, -1.0, 0.0)), 0.0).astype(BF16)
    row, col = place(1)
    w_b = jnp.where(row == head,
                    jnp.where(col == npc - 1, 1.0, jnp.where(col == LANES + 2 * npc - 1, -1.0, 0.0)), 0.0).astype(BF16)
    lane = lax.broadcasted_iota(jnp.int32, (1, 2 * LANES), 1)
    in_range = lambda lo, hi: jnp.where(lane >= lo, jnp.where(lane < hi, 1.0, 0.0), 0.0)
    ones = in_range(npc, 2 * npc) + in_range(LANES, LANES + npc)
    return (_dot(hi_mid, w_a) + _dot(pieces[2], w_b) + ones).astype(BF16)


def _fox_kernel(q_ref, k_ref, v_ref, cum_ref, o_ref, qaug_scr, kaug_scr, vaug_scr):
    hb = pl.program_id(1)
    tile, grp = FOX_TQ, FOX_GROUP
    seq = q_ref.shape[0]
    scale = 1.0 / math.sqrt(HEAD_DIM)
    c2 = scale * LOG2E
    head = lambda h: slice(h * HEAD_DIM, (h + 1) * HEAD_DIM)

    pieces = _split3(cum_ref[...] * (1.0 / scale))
    ones = _ones_lane0(seq)
    for g in range(grp):
        bias = _bias_lanes(pieces, hb * grp + g)
        qaug_scr[:, head(2 * g)] = q_ref[:, head(g)]
        qaug_scr[:, head(2 * g + 1)] = bias[:, :LANES]
        kaug_scr[:, head(2 * g)] = k_ref[:, head(g)]
        kaug_scr[:, head(2 * g + 1)] = bias[:, LANES:]
        vaug_scr[:, head(2 * g)] = v_ref[:, head(g)]
        vaug_scr[:, head(2 * g + 1)] = ones

    causal = (lax.broadcasted_iota(jnp.int32, (tile, tile), 1) <= lax.broadcasted_iota(jnp.int32, (tile, tile), 0))
    for i in range(seq // tile):
        qrows = slice(i * tile, (i + 1) * tile)
        for g in range(grp):
            carry = _online_init(tile)
            for j in range(i + 1):
                krows = slice(j * tile, (j + 1) * tile)
                s2 = _dot_nt(qaug_scr[qrows, 2 * g * HEAD_DIM:(2 * g + 2) * HEAD_DIM],
                             kaug_scr[krows, 2 * g * HEAD_DIM:(2 * g + 2) * HEAD_DIM]) * c2
                if j == i:
                    s2 = jnp.where(causal, s2, NEG_INF)
                carry = _online_step(carry, s2, vaug_scr[krows, 2 * g * HEAD_DIM:(2 * g + 2) * HEAD_DIM])
            o_ref[qrows, head(g)] = _online_finish(carry).astype(o_ref.dtype)


def _fox_attention(zq, cumc, batch, seq):
    t = zq.shape[0]
    grp = FOX_GROUP
    gw = grp * HEAD_DIM
    return pl.pallas_call(
        _fox_kernel,
        grid=(batch, FOX_HEADS // grp),
        in_specs=[
            pl.BlockSpec((seq, gw), lambda b, h: (b, ZQ_FQ // grp + h)),
            pl.BlockSpec((seq, gw), lambda b, h: (b, ZQ_FK // grp + h)),
            pl.BlockSpec((seq, gw), lambda b, h: (b, ZQ_FV // grp + h)),
            pl.BlockSpec((seq, LANES), lambda b, h: (b, 0)),
        ],
        out_specs=pl.BlockSpec((seq, gw), lambda b, h: (b, h)),
        out_shape=jax.ShapeDtypeStruct((t, FOX_HEADS * HEAD_DIM), BF16),
        scratch_shapes=[pltpu.VMEM((seq, 2 * gw), BF16)] * 3,
        compiler_params=_cparams(("arbitrary", "arbitrary")),
        name="fox_attention",
    )(zq, zq, zq, cumc)


def _compress_to(src_ref, hk, pos_ref, w1_ref, w2_ref, flat_scr, dst_scr):
    half = CMP_BLOCK // 2
    n_rows = dst_scr.shape[0]
    cols = slice(hk * HEAD_DIM, (hk + 1) * HEAD_DIM)
    for jp in range(half):
        rows = src_ref[pl.ds(jp, n_rows, stride=CMP_STRIDE), :]
        flat_scr[:, jp * HEAD_DIM:(jp + 1) * HEAD_DIM] = (rows + pos_ref[jp:jp + 1, :]).astype(BF16)
        nxt = pltpu.roll(rows, n_rows - 1, axis=0)
        flat_scr[:, (half + jp) * HEAD_DIM:(half + jp + 1) * HEAD_DIM] = (
            nxt + pos_ref[half + jp:half + jp + 1, :]).astype(BF16)
    pre = _dot(flat_scr[...], w1_ref[...])
    dst_scr[:, cols] = _dot(jax.nn.gelu(pre).astype(BF16), w2_ref[...]).astype(BF16)


def _nsa_kernel(q_ref, ks_ref, kw_ref, vs_ref, vw_ref, *rest):
    nkv = NSA_KV_PER_STEP
    zs_ref, ocmp_ref, nots_ref, emask_ref, o_ref, ksaug_scr = rest
    hk0 = pl.program_id(1) * nkv
    i = pl.program_id(2)
    tq, tk = NSA_TQ, NSA_TK
    q0 = i * tq
    head = lambda h: slice(h * HEAD_DIM, (h + 1) * HEAD_DIM)

    @pl.when(i == 0)
    def _():
        for hk in range(nkv):
            ksaug_scr[:, head(2 * hk)] = ks_ref[:, head(hk)]
            ksaug_scr[:, head(2 * hk + 1)] = emask_ref[...]

    tile = functools.partial(_nsa_tile, q0=q0, hk0=hk0, q_ref=q_ref, kw_ref=kw_ref, vs_ref=vs_ref, vw_ref=vw_ref,
                             zs_ref=zs_ref, ocmp_ref=ocmp_ref, nots_ref=nots_ref, o_ref=o_ref, ksaug_scr=ksaug_scr)
    if NSA_STATIC_SWEEP:
        for n_full in range(ks_ref.shape[0] // tk):
            pl.when(q0 // tk == n_full)(functools.partial(tile, n_full))
    else:
        tile(q0 // tk)


def _nsa_tile(n_full, *, q0, hk0, q_ref, kw_ref, vs_ref, vw_ref, zs_ref, ocmp_ref, nots_ref, o_ref, ksaug_scr):
    grp, nkv = NSA_GROUP, NSA_KV_PER_STEP
    head = lambda h: slice(h * HEAD_DIM, (h + 1) * HEAD_DIM)

    def stack4(tile):
        return jnp.concatenate([tile] * grp, axis=0)

    gl = jax.nn.sigmoid(zs_ref[...])
    pre = []
    for hk in range(nkv):
        q4 = jnp.concatenate([q_ref[:, head(hk * grp + g)] for g in range(grp)], axis=0)
        o_cmp = jnp.concatenate([ocmp_ref[:, head(hk * grp + g)] for g in range(grp)], axis=0)
        pre.append((q4, jnp.concatenate([q4, stack4(nots_ref[:, head(hk)])], axis=1), o_cmp))
    _nsa_sweeps(n_full, pre, gl, q0=q0, hk0=hk0, kw_ref=kw_ref, vs_ref=vs_ref, vw_ref=vw_ref, o_ref=o_ref,
                ksaug_scr=ksaug_scr)


def _nsa_select_kernel(q_ref, *rest):
    nkv = NSA_KV_HEADS
    kcf_refs, vcf_refs = rest[:nkv], rest[nkv:2 * nkv]
    (posk_ref, posv_ref, w1k_ref, w1v_ref, w2k_ref, w2v_ref,
     ocmp_ref, nots_ref, kcmp_scr, vcmp_scr, flat_scr, tr_scr) = rest[2 * nkv:]
    i = pl.program_id(1)
    tq, grp = NSA_SEL_TQ, NSA_GROUP
    n_selb = SEL_BLOCK // CMP_STRIDE
    n_sel = LANES // n_selb
    c2 = LOG2E / math.sqrt(HEAD_DIM)
    q0 = i * tq
    head = lambda h: slice(h * HEAD_DIM, (h + 1) * HEAD_DIM)

    @pl.when(i == 0)
    def _():
        for hk in range(nkv):
            _compress_to(kcf_refs[hk], hk, posk_ref, w1k_ref, w2k_ref, flat_scr, kcmp_scr)
            _compress_to(vcf_refs[hk], hk, posv_ref, w1v_ref, w2v_ref, flat_scr, vcmp_scr)

    rloc = lax.broadcasted_iota(jnp.int32, (tq, LANES), 0)
    lane = lax.broadcasted_iota(jnp.int32, (tq, LANES), 1)
    cvalid = jnp.where((lane * CMP_STRIDE + (CMP_BLOCK - 1)) <= q0 + rloc, 1.0, 0.0)
    cvalid4 = jnp.concatenate([cvalid] * grp, axis=0) > 0.5
    blk = lax.broadcasted_iota(jnp.int32, (n_sel, tq), 0)
    cur = lax.shift_right_logical(q0 + lax.broadcasted_iota(jnp.int32, (n_sel, tq), 1), int(math.log2(SEL_BLOCK)))
    forced = jnp.where(blk == 0, 1.0, jnp.where(blk == cur, 1.0, jnp.where(blk == cur - 1, 1.0, 0.0)))

    for hk in range(nkv):
        q4 = jnp.concatenate([q_ref[:, head(hk * grp + g)] for g in range(grp)], axis=0)
        sc = jnp.where(cvalid4, _dot_nt(q4, kcmp_scr[:, head(hk)]) * c2, NEG_INF)
        pc = jnp.where(cvalid4, jnp.exp2(sc - jnp.max(sc, axis=1, keepdims=True)), 0.0)
        pc = pc / jnp.maximum(jnp.sum(pc, axis=1, keepdims=True), TINY)
        o_cmp = _dot(pc.astype(BF16), vcmp_scr[:, head(hk)])
        psum = (pc[0:tq] + pc[tq:2 * tq]) + (pc[2 * tq:3 * tq] + pc[3 * tq:4 * tq])
        nsub = tq // LANES
        for u in range(nsub):
            tr_scr[hk * nsub + u] = psum[u * LANES:(u + 1) * LANES, :].T
        pooled = lambda t: ((tr_scr[t, pl.ds(0, n_sel, stride=n_selb), :] + tr_scr[t, pl.ds(1, n_sel, stride=n_selb), :])
                            + (tr_scr[t, pl.ds(2, n_sel, stride=n_selb), :] + tr_scr[t, pl.ds(3, n_sel, stride=n_selb), :]))
        imp = jnp.concatenate([pooled(hk * nsub + u) for u in range(nsub)], axis=1)
        score = jnp.where(blk <= cur, imp + forced * FORCE_BONUS, NEG_INF)
        rank = jnp.zeros((n_sel, tq), F32)
        for kk in range(n_sel):
            row = score[kk:kk + 1, :]
            earlier = jnp.where(blk > kk, 1.0, 0.0)
            rank = rank + jnp.where(row > score, 1.0, jnp.where(row == score, earlier, 0.0))
        dropped = jnp.where(rank < float(SEL_TOPK), 0.0, 1.0)
        pieces = []
        for u in range(nsub):
            tr_scr[hk * nsub + u, 0:n_sel, :] = dropped[:, u * LANES:(u + 1) * LANES]
            tr_scr[hk * nsub + u, n_sel:, :] = jnp.zeros((LANES - n_sel, LANES), F32)
            pieces.append(tr_scr[hk * nsub + u].T)
        nots_ref[:, head(hk)] = jnp.concatenate(pieces, axis=0).astype(BF16)
        for g in range(grp):
            ocmp_ref[:, head(hk * grp + g)] = o_cmp[g * tq:(g + 1) * tq, :]


def _nsa_sweeps(n_full, pre, gl, *, q0, hk0, kw_ref, vs_ref, vw_ref, o_ref, ksaug_scr):
    tq, tk, grp, nkv = NSA_TQ, NSA_TK, NSA_GROUP, NSA_KV_PER_STEP
    rows4 = grp * tq
    c2 = LOG2E / math.sqrt(HEAD_DIM)
    head = lambda h: slice(h * HEAD_DIM, (h + 1) * HEAD_DIM)

    def stack4(tile):
        return jnp.concatenate([tile] * grp, axis=0)

    def sel_step(j, carries, bias4):
        rows = slice(j * tk, (j + 1) * tk) if isinstance(j, int) else pl.ds(pl.multiple_of(j * tk, tk), tk)
        out = []
        for hk in range(nkv):
            s2 = _dot_nt(pre[hk][1], ksaug_scr[rows, 2 * hk * HEAD_DIM:(2 * hk + 2) * HEAD_DIM]) * c2
            if bias4 is not None:
                s2 = s2 + bias4
            out.append(_online_step_sum(carries[hk], s2, vs_ref[rows, head(hk)]))
        return tuple(out)

    carries = tuple(_online_init_sum(rows4) for _ in range(nkv))
    if isinstance(n_full, int):
        for j in range(n_full):
            carries = sel_step(j, carries, None)
    else:
        carries = lax.fori_loop(0, n_full, lambda j, c: sel_step(j, c, None), carries)
    rk = lax.broadcasted_iota(jnp.int32, (tq, tk), 0)
    ck = lax.broadcasted_iota(jnp.int32, (tq, tk), 1)
    causal4 = stack4(jnp.where(n_full * tk + ck <= q0 + rk, 0.0, NEG_INF))
    carries = sel_step(n_full, carries, causal4)

    w0 = pl.multiple_of(jnp.maximum(q0 - WINDOW, 0), tq)
    rw = lax.broadcasted_iota(jnp.int32, (tq, WIN_KEYS), 0)
    cw = lax.broadcasted_iota(jnp.int32, (tq, WIN_KEYS), 1)
    dist = (q0 - w0) + rw - cw
    wbias4 = stack4(jnp.where(jnp.abs(2 * dist - (WINDOW - 1)) <= (WINDOW - 1), 0.0, NEG_INF))
    wrows = pl.ds(w0, WIN_KEYS)

    for hk in range(nkv):
        q4, _, o_cmp = pre[hk]
        o_sel = _online_finish_sum(carries[hk])
        s2 = _dot_nt(q4, kw_ref[wrows, head(hk)]) * c2 + wbias4
        o_win = _online_finish_sum(_online_step_sum(_online_init_sum(rows4), s2, vw_ref[wrows, head(hk)]))
        gate = lambda c: jnp.concatenate(
            [_lane_pick(gl, ZS_GATE0 + 3 * ((hk0 + hk) * grp + g) + c) for g in range(grp)], axis=0)
        o4 = gate(0) * o_cmp + gate(1) * o_sel + gate(2) * o_win
        for g in range(grp):
            o_ref[:, head(hk * grp + g)] = o4[g * tq:(g + 1) * tq, :].astype(o_ref.dtype)


def _nsa_attention(zq, zc, zs, posk, posv, w1k, w1v, w2k, w2v, emask, batch, seq):
    t = zq.shape[0]
    all_q = NSA_HEADS * HEAD_DIM
    all_kv = NSA_KV_HEADS * HEAD_DIM

    ts = NSA_SEL_TQ
    ns = seq // ts
    const2 = lambda shape: pl.BlockSpec(shape, lambda b, i: (0,) * len(shape))
    o_cmp, notsel = pl.pallas_call(
        _nsa_select_kernel,
        grid=(batch, ns),
        in_specs=[
            pl.BlockSpec((ts, all_q), lambda b, i: (b * ns + i, ZQ_NQ * HEAD_DIM // all_q)),
            *[pl.BlockSpec((seq, HEAD_DIM), functools.partial(lambda b, i, h: (b, h), h=h)) for h in range(ZC_HEADS)],
            const2((CMP_BLOCK, HEAD_DIM)), const2((CMP_BLOCK, HEAD_DIM)),
            const2((CMP_BLOCK * HEAD_DIM, HEAD_DIM)), const2((CMP_BLOCK * HEAD_DIM, HEAD_DIM)),
            const2((HEAD_DIM, HEAD_DIM)), const2((HEAD_DIM, HEAD_DIM)),
        ],
        out_specs=[
            pl.BlockSpec((ts, all_q), lambda b, i: (b * ns + i, 0)),
            pl.BlockSpec((ts, all_kv), lambda b, i: (b * ns + i, 0)),
        ],
        out_shape=[jax.ShapeDtypeStruct((t, all_q), F32), jax.ShapeDtypeStruct((t, all_kv), BF16)],
        scratch_shapes=[
            pltpu.VMEM((LANES, all_kv), BF16),
            pltpu.VMEM((LANES, all_kv), BF16),
            pltpu.VMEM((LANES, CMP_BLOCK * HEAD_DIM), BF16),
            pltpu.VMEM((NSA_KV_HEADS * (ts // LANES), LANES, LANES), F32),
        ],
        compiler_params=_cparams(("arbitrary", "arbitrary")),
        name="nsa_select",
    )(zq, zc, zc, zc, zc, posk, posv, w1k, w1v, w2k, w2v)

    tq, nkv = NSA_TQ, NSA_KV_PER_STEP
    nq = seq // tq
    qw = nkv * NSA_GROUP * HEAD_DIM
    kvw = nkv * HEAD_DIM
    kv = lambda col: pl.BlockSpec((seq, kvw), lambda b, h, i: (b, col // nkv + h))
    return pl.pallas_call(
        _nsa_kernel,
        grid=(batch, NSA_KV_HEADS // nkv, nq),
        in_specs=[
            pl.BlockSpec((tq, qw), lambda b, h, i: (b * nq + i, ZQ_NQ * HEAD_DIM // qw + h)),
            kv(ZQ_KS), kv(ZQ_KW), kv(ZQ_VS), kv(ZQ_VW),
            pl.BlockSpec((tq, LANES), lambda b, h, i: (b * nq + i, 0)),
            pl.BlockSpec((tq, qw), lambda b, h, i: (b * nq + i, h)),
            pl.BlockSpec((tq, kvw), lambda b, h, i: (b * nq + i, h)),
            pl.BlockSpec((seq, LANES), lambda b, h, i: (0, 0)),
        ],
        out_specs=pl.BlockSpec((tq, qw), lambda b, h, i: (b * nq + i, h)),
        out_shape=jax.ShapeDtypeStruct((t, all_q), BF16),
        scratch_shapes=[pltpu.VMEM((seq, 2 * kvw), BF16)],
        compiler_params=_cparams(("arbitrary", "arbitrary", "arbitrary")),
        name="nsa_attention",
    )(zq, zq, zq, zq, zq, zs, o_cmp, notsel, emask)


def _out_proj_kernel(of_ref, on_ref, wf_ref, wn_ref, x_ref, mod_ref, o_ref):
    n = o_ref.shape[1]
    for c0 in range(0, n, COL_CHUNK):
        cols = slice(c0, c0 + COL_CHUNK)
        acc = _dot(of_ref[...], wf_ref[:, cols]) + _dot(on_ref[...], wn_ref[:, cols])
        o_ref[:, cols] = x_ref[:, cols] + mod_ref[0, 2:3, cols] * acc


def _out_proj(o_fox, o_nsa, w_fox, w_nsa, x2d, mod, seq):
    t, d = x2d.shape
    tm = ROW_TILE
    tpb = seq // tm
    kf, kn = o_fox.shape[1], o_nsa.shape[1]
    return pl.pallas_call(
        _out_proj_kernel,
        grid=(t // tm,),
        in_specs=[
            pl.BlockSpec((tm, kf), lambda i: (i, 0)),
            pl.BlockSpec((tm, kn), lambda i: (i, 0)),
            pl.BlockSpec((kf, d), lambda i: (0, 0), pipeline_mode=pl.Buffered(1)),
            pl.BlockSpec((kn, d), lambda i: (0, 0), pipeline_mode=pl.Buffered(1)),
            pl.BlockSpec((tm, d), lambda i: (i, 0)),
            pl.BlockSpec((1, 6, d), lambda i: (i // tpb, 0, 0)),
        ],
        out_specs=pl.BlockSpec((tm, d), lambda i: (i, 0)),
        out_shape=jax.ShapeDtypeStruct((t, d), F32),
        compiler_params=_cparams(("arbitrary",)),
        name="out_proj",
    )(o_fox, o_nsa, w_fox, w_nsa, x2d, mod)


def _conv_rows(u, cw_ref, cb_ref):
    y = cb_ref[...] + cw_ref[0:1, :] * pltpu.roll(u, 2, axis=0)
    y = y + cw_ref[1:2, :] * pltpu.roll(u, 1, axis=0)
    return y + cw_ref[2:3, :] * u


def _ffn_up_kernel(x_ref, xh_ref, mod_ref, g_ref, wg_ref, wv_ref, cwg_ref, cwv_ref, cbg_ref, cbv_ref,
                   a_ref, h_scr, *, tiles_per_batch):
    i = pl.program_id(0)
    c = pl.program_id(1)
    tm = x_ref.shape[0]

    @pl.when(c == 0)
    def _():
        gain, shift = g_ref[...] * (1.0 + mod_ref[0, 4:5, :]), mod_ref[0, 3:4, :]
        halo = _norm_mod_rows(xh_ref[...], gain, shift)
        first = (i % tiles_per_batch) == 0
        h_scr[0:HALO, :] = jnp.where(first, 0.0, halo).astype(BF16)
        _norm_mod_to_scratch(x_ref, h_scr, HALO, tm, gain, shift)

    h = h_scr[...]
    yg = _conv_rows(_dot(h, wg_ref[...]), cwg_ref, cbg_ref)[HALO:, :]
    yv = _conv_rows(_dot(h, wv_ref[...]), cwv_ref, cbv_ref)[HALO:, :]
    a_ref[...] = ((yg * jax.nn.sigmoid(yg)) * yv).astype(a_ref.dtype)


def _ffn_down_kernel(a_ref, w_ref, x_ref, mod_ref, fg_ref, o_ref, *, final_norm):
    n = o_ref.shape[1]
    sumsq = jnp.zeros((o_ref.shape[0], 1), F32)
    for c0 in range(0, n, COL_CHUNK):
        cols = slice(c0, c0 + COL_CHUNK)
        y = x_ref[:, cols] + mod_ref[0, 5:6, cols] * _dot(a_ref[...], w_ref[:, cols])
        o_ref[:, cols] = y
        if final_norm:
            sumsq = sumsq + jnp.sum(y * y, axis=1, keepdims=True)
    if final_norm:
        r = lax.rsqrt(sumsq * (1.0 / n) + EPS)
        for c0 in range(0, n, COL_CHUNK):
            cols = slice(c0, c0 + COL_CHUNK)
            o_ref[:, cols] = (o_ref[:, cols] * r) * fg_ref[:, cols]


def _ffn(x2d, mod, g, w_up, conv_w, conv_b, w_down, final_g, seq, final_norm):
    t, d = x2d.shape
    dff = w_down.shape[0]
    tm, tf = FFN_TILE, FF_CHUNK
    tpb = seq // tm
    nc = dff // tf
    hb = tm // HALO
    act = pl.pallas_call(
        functools.partial(_ffn_up_kernel, tiles_per_batch=tpb),
        grid=(t // tm, nc),
        in_specs=[
            pl.BlockSpec((tm, d), lambda i, c: (i, 0)),
            pl.BlockSpec((HALO, d), lambda i, c: (jnp.maximum(i * hb - 1, 0), 0)),
            pl.BlockSpec((1, 6, d), lambda i, c: (i // tpb, 0, 0)),
            pl.BlockSpec((1, d), lambda i, c: (0, 0)),
            pl.BlockSpec((d, tf), lambda i, c: (0, c)),
            pl.BlockSpec((d, tf), lambda i, c: (0, nc + c)),
            pl.BlockSpec((CONV_WIDTH, tf), lambda i, c: (0, c)),
            pl.BlockSpec((CONV_WIDTH, tf), lambda i, c: (0, nc + c)),
            pl.BlockSpec((1, tf), lambda i, c: (0, c)),
            pl.BlockSpec((1, tf), lambda i, c: (0, nc + c)),
        ],
        out_specs=pl.BlockSpec((tm, tf), lambda i, c: (i, c)),
        out_shape=jax.ShapeDtypeStruct((t, dff), BF16),
        scratch_shapes=[pltpu.VMEM((HALO + tm, d), BF16)],
        compiler_params=_cparams(("arbitrary", "arbitrary")),
        name="ffn_up",
    )(x2d, x2d, mod, g, w_up, w_up, conv_w, conv_w, conv_b, conv_b)

    tr = ROW_TILE
    rpb = seq // tr
    return pl.pallas_call(
        functools.partial(_ffn_down_kernel, final_norm=final_norm),
        grid=(t // tr,),
        in_specs=[
            pl.BlockSpec((tr, dff), lambda i: (i, 0)),
            pl.BlockSpec((dff, d), lambda i: (0, 0), pipeline_mode=pl.Buffered(1)),
            pl.BlockSpec((tr, d), lambda i: (i, 0)),
            pl.BlockSpec((1, 6, d), lambda i: (i // rpb, 0, 0)),
            pl.BlockSpec((1, d), lambda i: (0, 0)),
        ],
        out_specs=pl.BlockSpec((tr, d), lambda i: (i, 0)),
        out_shape=jax.ShapeDtypeStruct((t, d), F32),
        compiler_params=_cparams(("arbitrary",)),
        name="ffn_down",
    )(act, w_down, x2d, mod, final_g)


def _rope_tables(seq):
    inv = ROPE_THETA ** (-jnp.arange(0, HEAD_DIM, 2, dtype=F32) / HEAD_DIM)
    ang = jnp.arange(seq, dtype=F32)[:, None] * inv[None, :]
    cos, sin = jnp.cos(ang), jnp.sin(ang)
    return jnp.concatenate([cos, cos], axis=-1), jnp.concatenate([-sin, sin], axis=-1)


def _select_mask_table(seq):
    key_blk = (np.arange(seq) // SEL_BLOCK).reshape(seq, 1)
    return jnp.asarray(np.where(np.arange(LANES).reshape(1, LANES) == key_blk, NEG_INF, 0.0), dtype=BF16)


def _split_w_in(w):
    hd = HEAD_DIM
    sizes = [FOX_HEADS * hd] * 3 + [FOX_HEADS] + [NSA_HEADS * hd] + [NSA_KV_HEADS * hd] * 6 + [3 * NSA_HEADS]
    w = w.astype(BF16)
    fq, fk, fv, ff, nq, kc, vc, ks, vs, kw, vw, ng = jnp.split(w, [int(o) for o in np.cumsum(sizes)[:-1]], axis=-1)
    pad = jnp.zeros((w.shape[0], LANES - FOX_HEADS - 3 * NSA_HEADS), w.dtype)
    return jnp.concatenate([fq, fk, fv, nq, ks, kw, vs, vw, kc, vc, ff, ng, pad], axis=-1)


def kernel(x, c, attn_norm_g, ffn_norm_g, w_ada, b_ada, w_in, b_fgate, cmp_pos_k, cmp_pos_v,
           w_cmp1_k, w_cmp2_k, w_cmp1_v, w_cmp2_v, w_out, w_up, conv_w, conv_b, w_down, final_norm_g):
    batch, seq, d = x.shape
    depth = w_ada.shape[0]
    assert seq % ROW_TILE == 0 and seq % FFN_TILE == 0 and seq % FOX_TQ == 0 and seq % NSA_TK == 0 and seq // SEL_BLOCK == LANES // 4
    assert seq >= WIN_KEYS and w_down.shape[1] % FF_CHUNK == 0 and d % COL_CHUNK == 0

    cosf, sinf = _rope_tables(seq)
    emask = _select_mask_table(seq)
    tri = jnp.asarray(np.tril(np.ones((LANES, LANES))), dtype=BF16)
    mod_all = _ada_mod(c, w_ada, b_ada)

    x2d = x.reshape(batch * seq, d)
    n_fox = FOX_HEADS * HEAD_DIM
    for l in range(depth):
        mod = mod_all[l].reshape(batch, 6, d)
        zq, zc, zs = _in_proj(x2d, mod, attn_norm_g[l].reshape(1, d), _split_w_in(w_in[l]), cosf, sinf, seq)
        bias_row = jnp.zeros((1, LANES), F32).at[0, :FOX_HEADS].set(b_fgate[l])
        cumc = _fgate(zs, bias_row, tri, batch, seq)
        o_fox = _fox_attention(zq, cumc, batch, seq)
        o_nsa = _nsa_attention(zq, zc, zs, cmp_pos_k[l], cmp_pos_v[l],
                               w_cmp1_k[l].astype(BF16), w_cmp1_v[l].astype(BF16),
                               w_cmp2_k[l].astype(BF16), w_cmp2_v[l].astype(BF16), emask, batch, seq)
        w_o = w_out[l].astype(BF16)
        x2d = _out_proj(o_fox, o_nsa, w_o[:n_fox], w_o[n_fox:], x2d, mod, seq)
        x2d = _ffn(x2d, mod, ffn_norm_g[l].reshape(1, d), w_up[l].astype(BF16), conv_w[l],
                   conv_b[l].reshape(1, -1), w_down[l].astype(BF16), final_norm_g.reshape(1, d), seq,
                   final_norm=(l == depth - 1))
    return x2d.reshape(batch, seq, d)
```

```python
import functools
import math

import numpy as np
import jax
import jax.numpy as jnp
from jax import lax
from jax.experimental import pallas as pl
from jax.experimental.pallas import tpu as pltpu

HEAD_DIM = 128
FOX_HEADS = 8
NSA_HEADS = 8
NSA_KV_HEADS = 2
NSA_GROUP = NSA_HEADS // NSA_KV_HEADS
CMP_BLOCK = 32
CMP_STRIDE = 16
SEL_BLOCK = 64
SEL_TOPK = 16
WINDOW = 512
CONV_WIDTH = 3
ROPE_THETA = 10000.0
FORCE_BONUS = 1000.0
NEG_INF = -1e30
TINY = 1e-30
EPS = 1e-6
LOG2E = math.log2(math.e)

LANES = 128
SUBLANES = 8
VMEM_LIMIT_BYTES = 56 * 1024 * 1024

ROW_TILE = 512
FFN_TILE = 1024
COL_CHUNK = 512
NORM_SUBROWS = 16
NORM_UNROLL = 8
FOX_TQ = 512
FOX_GROUP = 2
NSA_TQ = 128
NSA_SEL_TQ = 512
NSA_KV_PER_STEP = 2
NSA_STATIC_SWEEP = True
NSA_TK = 512
WIN_KEYS = WINDOW + NSA_TQ
HALO = 2 * SUBLANES
FF_CHUNK = 512

BF16 = jnp.bfloat16
F32 = jnp.float32

ZQ_FQ, ZQ_FK, ZQ_FV, ZQ_NQ, ZQ_KS, ZQ_KW, ZQ_VS, ZQ_VW = 0, 8, 16, 24, 32, 34, 36, 38
ZQ_HEADS = 40
ZQ_ROPED = tuple(range(ZQ_NQ, ZQ_KW + NSA_KV_HEADS))
ZC_HEADS = 4
ZS_GATE0 = FOX_HEADS


def _cparams(sem):
    return pltpu.CompilerParams(dimension_semantics=sem, vmem_limit_bytes=VMEM_LIMIT_BYTES)


def _dot(a, b):
    return jnp.dot(a, b, preferred_element_type=F32)


def _dot_nt(a, b):
    return lax.dot_general(a, b, (((1,), (1,)), ((), ())), preferred_element_type=F32)


def _ada_kernel(c_ref, w_ref, b_ref, o_ref):
    c = c_ref[...]
    ca = (c * jax.nn.sigmoid(c)).astype(BF16)
    o_ref[0] = _dot(ca, w_ref[0].astype(BF16)) + b_ref[0]


def _ada_mod(c, w_ada, b_ada):
    depth, d, n = w_ada.shape
    b = c.shape[0]
    tn = 1024
    return pl.pallas_call(
        _ada_kernel,
        grid=(depth, n // tn),
        in_specs=[
            pl.BlockSpec((b, d), lambda l, j: (0, 0)),
            pl.BlockSpec((1, d, tn), lambda l, j: (l, 0, j)),
            pl.BlockSpec((1, 1, tn), lambda l, j: (l, 0, j)),
        ],
        out_specs=pl.BlockSpec((1, b, tn), lambda l, j: (l, 0, j)),
        out_shape=jax.ShapeDtypeStruct((depth, b, n), F32),
        compiler_params=_cparams(("arbitrary", "arbitrary")),
        name="ada_mod",
    )(c, w_ada, b_ada.reshape(depth, 1, n))


def _norm_mod_rows(x, gain, shift):
    ms = jnp.mean(x * x, axis=-1, keepdims=True)
    return (x * lax.rsqrt(ms + EPS)) * gain + shift


def _norm_mod_to_scratch(x_ref, h_scr, row0, n_rows, gain, shift):
    nr = NORM_SUBROWS
    def body(r, _):
        rows = pl.ds(pl.multiple_of(r * nr, nr), nr)
        dst = pl.ds(pl.multiple_of(row0 + r * nr, nr), nr)
        h_scr[dst, :] = _norm_mod_rows(x_ref[rows, :], gain, shift).astype(BF16)
        return 0
    lax.fori_loop(0, n_rows // nr, body, 0, unroll=NORM_UNROLL)


def _rope_tile(t, cosf, sinf):
    return t * cosf + pltpu.roll(t, HEAD_DIM // 2, axis=1) * sinf


def _in_proj_kernel(x_ref, mod_ref, g_ref, w_ref, cos_ref, sin_ref, zq_ref, zc_ref, zs_ref, h_scr):
    tm = x_ref.shape[0]
    _norm_mod_to_scratch(x_ref, h_scr, 0, tm, g_ref[...] * (1.0 + mod_ref[0, 1:2, :]), mod_ref[0, 0:1, :])
    heads_per_chunk = COL_CHUNK // HEAD_DIM
    for c0 in range(0, ZQ_HEADS, heads_per_chunk):
        acc = _dot(h_scr[...], w_ref[:, c0 * HEAD_DIM:(c0 + heads_per_chunk) * HEAD_DIM])
        for hh in range(heads_per_chunk):
            t = acc[:, hh * HEAD_DIM:(hh + 1) * HEAD_DIM]
            if c0 + hh in ZQ_ROPED:
                t = _rope_tile(t, cos_ref[...], sin_ref[...])
            zq_ref[:, (c0 + hh) * HEAD_DIM:(c0 + hh + 1) * HEAD_DIM] = t.astype(BF16)
    base = ZQ_HEADS * HEAD_DIM
    acc = _dot(h_scr[...], w_ref[:, base:base + ZC_HEADS * HEAD_DIM])
    for hh in range(ZC_HEADS):
        t = acc[:, hh * HEAD_DIM:(hh + 1) * HEAD_DIM]
        if hh < NSA_KV_HEADS:
            t = _rope_tile(t, cos_ref[...], sin_ref[...])
        zc_ref[:, hh * HEAD_DIM:(hh + 1) * HEAD_DIM] = t
    base += ZC_HEADS * HEAD_DIM
    zs_ref[...] = _dot(h_scr[...], w_ref[:, base:base + LANES])


def _in_proj(x2d, mod, g, wcat, cosf, sinf, seq):
    t, d = x2d.shape
    tm = ROW_TILE
    tpb = seq // tm
    nw = wcat.shape[1]
    return pl.pallas_call(
        _in_proj_kernel,
        grid=(t // tm,),
        in_specs=[
            pl.BlockSpec((tm, d), lambda i: (i, 0)),
            pl.BlockSpec((1, 6, d), lambda i: (i // tpb, 0, 0)),
            pl.BlockSpec((1, d), lambda i: (0, 0)),
            pl.BlockSpec((d, nw), lambda i: (0, 0), pipeline_mode=pl.Buffered(1)),
            pl.BlockSpec((tm, HEAD_DIM), lambda i: (i % tpb, 0)),
            pl.BlockSpec((tm, HEAD_DIM), lambda i: (i % tpb, 0)),
        ],
        out_specs=[
            pl.BlockSpec((tm, ZQ_HEADS * HEAD_DIM), lambda i: (i, 0)),
            pl.BlockSpec((tm, ZC_HEADS * HEAD_DIM), lambda i: (i, 0)),
            pl.BlockSpec((tm, LANES), lambda i: (i, 0)),
        ],
        out_shape=[
            jax.ShapeDtypeStruct((t, ZQ_HEADS * HEAD_DIM), BF16),
            jax.ShapeDtypeStruct((t, ZC_HEADS * HEAD_DIM), F32),
            jax.ShapeDtypeStruct((t, LANES), F32),
        ],
        scratch_shapes=[pltpu.VMEM((tm, d), BF16)],
        compiler_params=_cparams(("arbitrary",)),
        name="in_proj",
    )(x2d, mod, g, wcat, cosf, sinf)


def _split3(v):
    hi = v.astype(BF16)
    r1 = v - hi.astype(F32)
    mid = r1.astype(BF16)
    lo = (r1 - mid.astype(F32)).astype(BF16)
    return hi, mid, lo


def _fgate_kernel(zs_ref, bias_ref, tri_ref, cumc_ref):
    seq = zs_ref.shape[0]
    tri = tri_ref[...]
    carry = jnp.zeros((1, LANES), F32)
    for c in range(seq // LANES):
        zf = zs_ref[c * LANES:(c + 1) * LANES, :] + bias_ref[...]
        lf = jnp.minimum(zf, 0.0) - jnp.log1p(jnp.exp(-jnp.abs(zf)))
        hi, mid, lo = _split3(lf)
        cs = (_dot(tri, lo) + _dot(tri, mid)) + _dot(tri, hi) + carry
        carry = cs[LANES - 1:LANES, :]
        cumc_ref[c * LANES:(c + 1) * LANES, :] = cs


def _fgate(zs, bias_row, tri, batch, seq):
    t = zs.shape[0]
    return pl.pallas_call(
        _fgate_kernel,
        grid=(batch,),
        in_specs=[
            pl.BlockSpec((seq, LANES), lambda b: (b, 0)),
            pl.BlockSpec((1, LANES), lambda b: (0, 0)),
            pl.BlockSpec((LANES, LANES), lambda b: (0, 0)),
        ],
        out_specs=pl.BlockSpec((seq, LANES), lambda b: (b, 0)),
        out_shape=jax.ShapeDtypeStruct((t, LANES), F32),
        compiler_params=_cparams(("arbitrary",)),
        name="fgate_cumsum",
    )(zs, bias_row, tri)


def _lane_pick(tile, lane_idx):
    lane = lax.broadcasted_iota(jnp.int32, tile.shape, 1)
    return jnp.sum(jnp.where(lane == lane_idx, tile, 0.0), axis=1, keepdims=True)


def _online_step(carry, s2, v_ones):
    m, acc = carry
    m_new = jnp.maximum(m, jnp.max(s2, axis=1, keepdims=True))
    p = jnp.exp2(s2 - m_new)
    acc = jnp.exp2(m - m_new) * acc + _dot(p.astype(BF16), v_ones)
    return m_new, acc


def _online_init(rows):
    return jnp.full((rows, 1), NEG_INF, F32), jnp.zeros((rows, 2 * HEAD_DIM), F32)


def _online_finish(carry):
    _, acc = carry
    return acc[:, :HEAD_DIM] / acc[:, HEAD_DIM:HEAD_DIM + 1]


def _online_step_sum(carry, s2, v):
    m, l, acc = carry
    m_new = jnp.maximum(m, jnp.max(s2, axis=1, keepdims=True))
    alpha = jnp.exp2(m - m_new)
    p = jnp.exp2(s2 - m_new)
    l = alpha * l + jnp.sum(p, axis=1, keepdims=True)
    acc = alpha * acc + _dot(p.astype(BF16), v)
    return m_new, l, acc


def _online_init_sum(rows):
    return (jnp.full((rows, 1), NEG_INF, F32), jnp.zeros((rows, 1), F32), jnp.zeros((rows, HEAD_DIM), F32))


def _online_finish_sum(carry):
    _, l, acc = carry
    return acc / l


def _ones_lane0(rows):
    return jnp.where(lax.broadcasted_iota(jnp.int32, (rows, LANES), 1) == 0, 1.0, 0.0).astype(BF16)


def _bias_lanes(pieces, head):
    npc = len(pieces)

    def place(n_in):
        row = lax.broadcasted_iota(jnp.int32, (n_in * LANES, 2 * LANES), 0)
        col = lax.broadcasted_iota(jnp.int32, (n_in * LANES, 2 * LANES), 1)
        return row, col
    hi_mid = jnp.concatenate(pieces[:2], axis=1)
    row, col = place(2)
    pc = jnp.where(row >= LANES, 1, 0)
    hit = row - pc * LANES == head
    w_a = jnp.where(hit, jnp.where(col == pc, 1.0, jnp.where(col == LANES + npc + pc, -1.0, 0.0)), 0.0).astype(BF16)
    row, col = place(1)
    w_b = jnp.where(row == head,
                    jnp.where(col == npc - 1, 1.0, jnp.where(col == LANES + 2 * npc - 1, -1.0, 0.0)), 0.0).astype(BF16)
    lane = lax.broadcasted_iota(jnp.int32, (1, 2 * LANES), 1)
    in_range = lambda lo, hi: jnp.where(lane >= lo, jnp.where(lane < hi, 1.0, 0.0), 0.0)
    ones = in_range(npc, 2 * npc) + in_range(LANES, LANES + npc)
    return (_dot(hi_mid, w_a) + _dot(pieces[2], w_b) + ones).astype(BF16)


def _fox_kernel(q_ref, k_ref, v_ref, cum_ref, o_ref, qaug_scr, kaug_scr, vaug_scr):
    hb = pl.program_id(1)
    tile, grp = FOX_TQ, FOX_GROUP
    seq = q_ref.shape[0]
    scale = 1.0 / math.sqrt(HEAD_DIM)
    c2 = scale * LOG2E
    head = lambda h: slice(h * HEAD_DIM, (h + 1) * HEAD_DIM)

    pieces = _split3(cum_ref[...] * (1.0 / scale))
    ones = _ones_lane0(seq)
    for g in range(grp):
        bias = _bias_lanes(pieces, hb * grp + g)
        qaug_scr[:, head(2 * g)] = q_ref[:, head(g)]
        qaug_scr[:, head(2 * g + 1)] = bias[:, :LANES]
        kaug_scr[:, head(2 * g)] = k_ref[:, head(g)]
        kaug_scr[:, head(2 * g + 1)] = bias[:, LANES:]
        vaug_scr[:, head(2 * g)] = v_ref[:, head(g)]
        vaug_scr[:, head(2 * g + 1)] = ones

    causal = (lax.broadcasted_iota(jnp.int32, (tile, tile), 1) <= lax.broadcasted_iota(jnp.int32, (tile, tile), 0))
    for i in range(seq // tile):
        qrows = slice(i * tile, (i + 1) * tile)
        for g in range(grp):
            carry = _online_init(tile)
            for j in range(i + 1):
                krows = slice(j * tile, (j + 1) * tile)
                s2 = _dot_nt(qaug_scr[qrows, 2 * g * HEAD_DIM:(2 * g + 2) * HEAD_DIM],
                             kaug_scr[krows, 2 * g * HEAD_DIM:(2 * g + 2) * HEAD_DIM]) * c2
                if j == i:
                    s2 = jnp.where(causal, s2, NEG_INF)
                carry = _online_step(carry, s2, vaug_scr[krows, 2 * g * HEAD_DIM:(2 * g + 2) * HEAD_DIM])
            o_ref[qrows, head(g)] = _online_finish(carry).astype(o_ref.dtype)


def _fox_attention(zq, cumc, batch, seq):
    t = zq.shape[0]
    grp = FOX_GROUP
    gw = grp * HEAD_DIM
    return pl.pallas_call(
        _fox_kernel,
        grid=(batch, FOX_HEADS // grp),
        in_specs=[
            pl.BlockSpec((seq, gw), lambda b, h: (b, ZQ_FQ // grp + h)),
            pl.BlockSpec((seq, gw), lambda b, h: (b, ZQ_FK // grp + h)),
            pl.BlockSpec((seq, gw), lambda b, h: (b, ZQ_FV // grp + h)),
            pl.BlockSpec((seq, LANES), lambda b, h: (b, 0)),
        ],
        out_specs=pl.BlockSpec((seq, gw), lambda b, h: (b, h)),
        out_shape=jax.ShapeDtypeStruct((t, FOX_HEADS * HEAD_DIM), BF16),
        scratch_shapes=[pltpu.VMEM((seq, 2 * gw), BF16)] * 3,
        compiler_params=_cparams(("arbitrary", "arbitrary")),
        name="fox_attention",
    )(zq, zq, zq, cumc)


def _compress_to(src_ref, hk, pos_ref, w1_ref, w2_ref, flat_scr, dst_scr):
    half = CMP_BLOCK // 2
    n_rows = dst_scr.shape[0]
    cols = slice(hk * HEAD_DIM, (hk + 1) * HEAD_DIM)
    for jp in range(half):
        rows = src_ref[pl.ds(jp, n_rows, stride=CMP_STRIDE), :]
        flat_scr[:, jp * HEAD_DIM:(jp + 1) * HEAD_DIM] = (rows + pos_ref[jp:jp + 1, :]).astype(BF16)
        nxt = pltpu.roll(rows, n_rows - 1, axis=0)
        flat_scr[:, (half + jp) * HEAD_DIM:(half + jp + 1) * HEAD_DIM] = (
            nxt + pos_ref[half + jp:half + jp + 1, :]).astype(BF16)
    pre = _dot(flat_scr[...], w1_ref[...])
    dst_scr[:, cols] = _dot(jax.nn.gelu(pre).astype(BF16), w2_ref[...]).astype(BF16)


def _nsa_kernel(q_ref, ks_ref, kw_ref, vs_ref, vw_ref, *rest):
    nkv = NSA_KV_PER_STEP
    zs_ref, ocmp_ref, nots_ref, emask_ref, o_ref, ksaug_scr = rest
    hk0 = pl.program_id(1) * nkv
    i = pl.program_id(2)
    tq, tk = NSA_TQ, NSA_TK
    q0 = i * tq
    head = lambda h: slice(h * HEAD_DIM, (h + 1) * HEAD_DIM)

    @pl.when(i == 0)
    def _():
        for hk in range(nkv):
            ksaug_scr[:, head(2 * hk)] = ks_ref[:, head(hk)]
            ksaug_scr[:, head(2 * hk + 1)] = emask_ref[...]

    tile = functools.partial(_nsa_tile, q0=q0, hk0=hk0, q_ref=q_ref, kw_ref=kw_ref, vs_ref=vs_ref, vw_ref=vw_ref,
                             zs_ref=zs_ref, ocmp_ref=ocmp_ref, nots_ref=nots_ref, o_ref=o_ref, ksaug_scr=ksaug_scr)
    if NSA_STATIC_SWEEP:
        for n_full in range(ks_ref.shape[0] // tk):
            pl.when(q0 // tk == n_full)(functools.partial(tile, n_full))
    else:
        tile(q0 // tk)


def _nsa_tile(n_full, *, q0, hk0, q_ref, kw_ref, vs_ref, vw_ref, zs_ref, ocmp_ref, nots_ref, o_ref, ksaug_scr):
    grp, nkv = NSA_GROUP, NSA_KV_PER_STEP
    head = lambda h: slice(h * HEAD_DIM, (h + 1) * HEAD_DIM)

    def stack4(tile):
        return jnp.concatenate([tile] * grp, axis=0)

    gl = jax.nn.sigmoid(zs_ref[...])
    pre = []
    for hk in range(nkv):
        q4 = jnp.concatenate([q_ref[:, head(hk * grp + g)] for g in range(grp)], axis=0)
        o_cmp = jnp.concatenate([ocmp_ref[:, head(hk * grp + g)] for g in range(grp)], axis=0)
        pre.append((q4, jnp.concatenate([q4, stack4(nots_ref[:, head(hk)])], axis=1), o_cmp))
    _nsa_sweeps(n_full, pre, gl, q0=q0, hk0=hk0, kw_ref=kw_ref, vs_ref=vs_ref, vw_ref=vw_ref, o_ref=o_ref,
                ksaug_scr=ksaug_scr)


def _nsa_select_kernel(q_ref, *rest):
    nkv = NSA_KV_HEADS
    kcf_refs, vcf_refs = rest[:nkv], rest[nkv:2 * nkv]
    (posk_ref, posv_ref, w1k_ref, w1v_ref, w2k_ref, w2v_ref,
     ocmp_ref, nots_ref, kcmp_scr, vcmp_scr, flat_scr, tr_scr) = rest[2 * nkv:]
    i = pl.program_id(1)
    tq, grp = NSA_SEL_TQ, NSA_GROUP
    n_selb = SEL_BLOCK // CMP_STRIDE
    n_sel = LANES // n_selb
    c2 = LOG2E / math.sqrt(HEAD_DIM)
    q0 = i * tq
    head = lambda h: slice(h * HEAD_DIM, (h + 1) * HEAD_DIM)

    @pl.when(i == 0)
    def _():
        for hk in range(nkv):
            _compress_to(kcf_refs[hk], hk, posk_ref, w1k_ref, w2k_ref, flat_scr, kcmp_scr)
            _compress_to(vcf_refs[hk], hk, posv_ref, w1v_ref, w2v_ref, flat_scr, vcmp_scr)

    rloc = lax.broadcasted_iota(jnp.int32, (tq, LANES), 0)
    lane = lax.broadcasted_iota(jnp.int32, (tq, LANES), 1)
    cvalid = jnp.where((lane * CMP_STRIDE + (CMP_BLOCK - 1)) <= q0 + rloc, 1.0, 0.0)
    cvalid4 = jnp.concatenate([cvalid] * grp, axis=0) > 0.5
    blk = lax.broadcasted_iota(jnp.int32, (n_sel, tq), 0)
    cur = lax.shift_right_logical(q0 + lax.broadcasted_iota(jnp.int32, (n_sel, tq), 1), int(math.log2(SEL_BLOCK)))
    forced = jnp.where(blk == 0, 1.0, jnp.where(blk == cur, 1.0, jnp.where(blk == cur - 1, 1.0, 0.0)))

    for hk in range(nkv):
        q4 = jnp.concatenate([q_ref[:, head(hk * grp + g)] for g in range(grp)], axis=0)
        sc = jnp.where(cvalid4, _dot_nt(q4, kcmp_scr[:, head(hk)]) * c2, NEG_INF)
        pc = jnp.where(cvalid4, jnp.exp2(sc - jnp.max(sc, axis=1, keepdims=True)), 0.0)
        pc = pc / jnp.maximum(jnp.sum(pc, axis=1, keepdims=True), TINY)
        o_cmp = _dot(pc.astype(BF16), vcmp_scr[:, head(hk)])
        psum = (pc[0:tq] + pc[tq:2 * tq]) + (pc[2 * tq:3 * tq] + pc[3 * tq:4 * tq])
        nsub = tq // LANES
        for u in range(nsub):
            tr_scr[hk * nsub + u] = psum[u * LANES:(u + 1) * LANES, :].T
        pooled = lambda t: ((tr_scr[t, pl.ds(0, n_sel, stride=n_selb), :] + tr_scr[t, pl.ds(1, n_sel, stride=n_selb), :])
                            + (tr_scr[t, pl.ds(2, n_sel, stride=n_selb), :] + tr_scr[t, pl.ds(3, n_sel, stride=n_selb), :]))
        imp = jnp.concatenate([pooled(hk * nsub + u) for u in range(nsub)], axis=1)
        score = jnp.where(blk <= cur, imp + forced * FORCE_BONUS, NEG_INF)
        rank = jnp.zeros((n_sel, tq), F32)
        for kk in range(n_sel):
            row = score[kk:kk + 1, :]
            earlier = jnp.where(blk > kk, 1.0, 0.0)
            rank = rank + jnp.where(row > score, 1.0, jnp.where(row == score, earlier, 0.0))
        dropped = jnp.where(rank < float(SEL_TOPK), 0.0, 1.0)
        pieces = []
        for u in range(nsub):
            tr_scr[hk * nsub + u, 0:n_sel, :] = dropped[:, u * LANES:(u + 1) * LANES]
            tr_scr[hk * nsub + u, n_sel:, :] = jnp.zeros((LANES - n_sel, LANES), F32)
            pieces.append(tr_scr[hk * nsub + u].T)
        nots_ref[:, head(hk)] = jnp.concatenate(pieces, axis=0).astype(BF16)
        for g in range(grp):
            ocmp_ref[:, head(hk * grp + g)] = o_cmp[g * tq:(g + 1) * tq, :]


def _nsa_sweeps(n_full, pre, gl, *, q0, hk0, kw_ref, vs_ref, vw_ref, o_ref, ksaug_scr):
    tq, tk, grp, nkv = NSA_TQ, NSA_TK, NSA_GROUP, NSA_KV_PER_STEP
    rows4 = grp * tq
    c2 = LOG2E / math.sqrt(HEAD_DIM)
    head = lambda h: slice(h * HEAD_DIM, (h + 1) * HEAD_DIM)

    def stack4(tile):
        return jnp.concatenate([tile] * grp, axis=0)

    def sel_step(j, carries, bias4):
        rows = slice(j * tk, (j + 1) * tk) if isinstance(j, int) else pl.ds(pl.multiple_of(j * tk, tk), tk)
        out = []
        for hk in range(nkv):
            s2 = _dot_nt(pre[hk][1], ksaug_scr[rows, 2 * hk * HEAD_DIM:(2 * hk + 2) * HEAD_DIM]) * c2
            if bias4 is not None:
                s2 = s2 + bias4
            out.append(_online_step(carries[hk], s2, jnp.concatenate([vs_ref[rows, head(hk)], ones_tk], axis=1)))
        return tuple(out)

    ones_tk = _ones_lane0(tk)
    carries = tuple(_online_init(rows4) for _ in range(nkv))
    if isinstance(n_full, int):
        for j in range(n_full):
            carries = sel_step(j, carries, None)
    else:
        carries = lax.fori_loop(0, n_full, lambda j, c: sel_step(j, c, None), carries)
    rk = lax.broadcasted_iota(jnp.int32, (tq, tk), 0)
    ck = lax.broadcasted_iota(jnp.int32, (tq, tk), 1)
    causal4 = stack4(jnp.where(n_full * tk + ck <= q0 + rk, 0.0, NEG_INF))
    carries = sel_step(n_full, carries, causal4)

    w0 = pl.multiple_of(jnp.maximum(q0 - WINDOW, 0), tq)
    rw = lax.broadcasted_iota(jnp.int32, (tq, WIN_KEYS), 0)
    cw = lax.broadcasted_iota(jnp.int32, (tq, WIN_KEYS), 1)
    dist = (q0 - w0) + rw - cw
    wbias4 = stack4(jnp.where(jnp.abs(2 * dist - (WINDOW - 1)) <= (WINDOW - 1), 0.0, NEG_INF))
    wrows = pl.ds(w0, WIN_KEYS)

    for hk in range(nkv):
        q4, _, o_cmp = pre[hk]
        o_sel = _online_finish(carries[hk])
        s2 = _dot_nt(q4, kw_ref[wrows, head(hk)]) * c2 + wbias4
        vw_ones = jnp.concatenate([vw_ref[wrows, head(hk)], _ones_lane0(WIN_KEYS)], axis=1)
        o_win = _online_finish(_online_step(_online_init(rows4), s2, vw_ones))
        gate = lambda c: jnp.concatenate(
            [_lane_pick(gl, ZS_GATE0 + 3 * ((hk0 + hk) * grp + g) + c) for g in range(grp)], axis=0)
        o4 = gate(0) * o_cmp + gate(1) * o_sel + gate(2) * o_win
        for g in range(grp):
            o_ref[:, head(hk * grp + g)] = o4[g * tq:(g + 1) * tq, :].astype(o_ref.dtype)


def _nsa_attention(zq, zc, zs, posk, posv, w1k, w1v, w2k, w2v, emask, batch, seq):
    t = zq.shape[0]
    all_q = NSA_HEADS * HEAD_DIM
    all_kv = NSA_KV_HEADS * HEAD_DIM

    ts = NSA_SEL_TQ
    ns = seq // ts
    const2 = lambda shape: pl.BlockSpec(shape, lambda b, i: (0,) * len(shape))
    o_cmp, notsel = pl.pallas_call(
        _nsa_select_kernel,
        grid=(batch, ns),
        in_specs=[
            pl.BlockSpec((ts, all_q), lambda b, i: (b * ns + i, ZQ_NQ * HEAD_DIM // all_q)),
            *[pl.BlockSpec((seq, HEAD_DIM), functools.partial(lambda b, i, h: (b, h), h=h)) for h in range(ZC_HEADS)],
            const2((CMP_BLOCK, HEAD_DIM)), const2((CMP_BLOCK, HEAD_DIM)),
            const2((CMP_BLOCK * HEAD_DIM, HEAD_DIM)), const2((CMP_BLOCK * HEAD_DIM, HEAD_DIM)),
            const2((HEAD_DIM, HEAD_DIM)), const2((HEAD_DIM, HEAD_DIM)),
        ],
        out_specs=[
            pl.BlockSpec((ts, all_q), lambda b, i: (b * ns + i, 0)),
            pl.BlockSpec((ts, all_kv), lambda b, i: (b * ns + i, 0)),
        ],
        out_shape=[jax.ShapeDtypeStruct((t, all_q), F32), jax.ShapeDtypeStruct((t, all_kv), BF16)],
        scratch_shapes=[
            pltpu.VMEM((LANES, all_kv), BF16),
            pltpu.VMEM((LANES, all_kv), BF16),
            pltpu.VMEM((LANES, CMP_BLOCK * HEAD_DIM), BF16),
            pltpu.VMEM((NSA_KV_HEADS * (ts // LANES), LANES, LANES), F32),
        ],
        compiler_params=_cparams(("arbitrary", "arbitrary")),
        name="nsa_select",
    )(zq, zc, zc, zc, zc, posk, posv, w1k, w1v, w2k, w2v)

    tq, nkv = NSA_TQ, NSA_KV_PER_STEP
    nq = seq // tq
    qw = nkv * NSA_GROUP * HEAD_DIM
    kvw = nkv * HEAD_DIM
    kv = lambda col: pl.BlockSpec((seq, kvw), lambda b, h, i: (b, col // nkv + h))
    return pl.pallas_call(
        _nsa_kernel,
        grid=(batch, NSA_KV_HEADS // nkv, nq),
        in_specs=[
            pl.BlockSpec((tq, qw), lambda b, h, i: (b * nq + i, ZQ_NQ * HEAD_DIM // qw + h)),
            kv(ZQ_KS), kv(ZQ_KW), kv(ZQ_VS), kv(ZQ_VW),
            pl.BlockSpec((tq, LANES), lambda b, h, i: (b * nq + i, 0)),
            pl.BlockSpec((tq, qw), lambda b, h, i: (b * nq + i, h)),
            pl.BlockSpec((tq, kvw), lambda b, h, i: (b * nq + i, h)),
            pl.BlockSpec((seq, LANES), lambda b, h, i: (0, 0)),
        ],
        out_specs=pl.BlockSpec((tq, qw), lambda b, h, i: (b * nq + i, h)),
        out_shape=jax.ShapeDtypeStruct((t, all_q), BF16),
        scratch_shapes=[pltpu.VMEM((seq, 2 * kvw), BF16)],
        compiler_params=_cparams(("arbitrary", "arbitrary", "arbitrary")),
        name="nsa_attention",
    )(zq, zq, zq, zq, zq, zs, o_cmp, notsel, emask)


def _out_proj_kernel(of_ref, on_ref, wf_ref, wn_ref, x_ref, mod_ref, o_ref):
    n = o_ref.shape[1]
    for c0 in range(0, n, COL_CHUNK):
        cols = slice(c0, c0 + COL_CHUNK)
        acc = _dot(of_ref[...], wf_ref[:, cols]) + _dot(on_ref[...], wn_ref[:, cols])
        o_ref[:, cols] = x_ref[:, cols] + mod_ref[0, 2:3, cols] * acc


def _out_proj(o_fox, o_nsa, w_fox, w_nsa, x2d, mod, seq):
    t, d = x2d.shape
    tm = ROW_TILE
    tpb = seq // tm
    kf, kn = o_fox.shape[1], o_nsa.shape[1]
    return pl.pallas_call(
        _out_proj_kernel,
        grid=(t // tm,),
        in_specs=[
            pl.BlockSpec((tm, kf), lambda i: (i, 0)),
            pl.BlockSpec((tm, kn), lambda i: (i, 0)),
            pl.BlockSpec((kf, d), lambda i: (0, 0), pipeline_mode=pl.Buffered(1)),
            pl.BlockSpec((kn, d), lambda i: (0, 0), pipeline_mode=pl.Buffered(1)),
            pl.BlockSpec((tm, d), lambda i: (i, 0)),
            pl.BlockSpec((1, 6, d), lambda i: (i // tpb, 0, 0)),
        ],
        out_specs=pl.BlockSpec((tm, d), lambda i: (i, 0)),
        out_shape=jax.ShapeDtypeStruct((t, d), F32),
        compiler_params=_cparams(("arbitrary",)),
        name="out_proj",
    )(o_fox, o_nsa, w_fox, w_nsa, x2d, mod)


def _conv_rows(u, cw_ref, cb_ref):
    y = cb_ref[...] + cw_ref[0:1, :] * pltpu.roll(u, 2, axis=0)
    y = y + cw_ref[1:2, :] * pltpu.roll(u, 1, axis=0)
    return y + cw_ref[2:3, :] * u


def _ffn_up_kernel(x_ref, xh_ref, mod_ref, g_ref, wg_ref, wv_ref, cwg_ref, cwv_ref, cbg_ref, cbv_ref,
                   a_ref, h_scr, *, tiles_per_batch):
    i = pl.program_id(0)
    c = pl.program_id(1)
    tm = x_ref.shape[0]

    @pl.when(c == 0)
    def _():
        gain, shift = g_ref[...] * (1.0 + mod_ref[0, 4:5, :]), mod_ref[0, 3:4, :]
        halo = _norm_mod_rows(xh_ref[...], gain, shift)
        first = (i % tiles_per_batch) == 0
        h_scr[0:HALO, :] = jnp.where(first, 0.0, halo).astype(BF16)
        _norm_mod_to_scratch(x_ref, h_scr, HALO, tm, gain, shift)

    h = h_scr[...]
    yg = _conv_rows(_dot(h, wg_ref[...]), cwg_ref, cbg_ref)[HALO:, :]
    yv = _conv_rows(_dot(h, wv_ref[...]), cwv_ref, cbv_ref)[HALO:, :]
    a_ref[...] = ((yg * jax.nn.sigmoid(yg)) * yv).astype(a_ref.dtype)


def _ffn_down_kernel(a_ref, w_ref, x_ref, mod_ref, fg_ref, o_ref, *, final_norm):
    n = o_ref.shape[1]
    sumsq = jnp.zeros((o_ref.shape[0], 1), F32)
    for c0 in range(0, n, COL_CHUNK):
        cols = slice(c0, c0 + COL_CHUNK)
        y = x_ref[:, cols] + mod_ref[0, 5:6, cols] * _dot(a_ref[...], w_ref[:, cols])
        o_ref[:, cols] = y
        if final_norm:
            sumsq = sumsq + jnp.sum(y * y, axis=1, keepdims=True)
    if final_norm:
        r = lax.rsqrt(sumsq * (1.0 / n) + EPS)
        for c0 in range(0, n, COL_CHUNK):
            cols = slice(c0, c0 + COL_CHUNK)
            o_ref[:, cols] = (o_ref[:, cols] * r) * fg_ref[:, cols]


def _ffn(x2d, mod, g, w_up, conv_w, conv_b, w_down, final_g, seq, final_norm):
    t, d = x2d.shape
    dff = w_down.shape[0]
    tm, tf = FFN_TILE, FF_CHUNK
    tpb = seq // tm
    nc = dff // tf
    hb = tm // HALO
    act = pl.pallas_call(
        functools.partial(_ffn_up_kernel, tiles_per_batch=tpb),
        grid=(t // tm, nc),
        in_specs=[
            pl.BlockSpec((tm, d), lambda i, c: (i, 0)),
            pl.BlockSpec((HALO, d), lambda i, c: (jnp.maximum(i * hb - 1, 0), 0)),
            pl.BlockSpec((1, 6, d), lambda i, c: (i // tpb, 0, 0)),
            pl.BlockSpec((1, d), lambda i, c: (0, 0)),
            pl.BlockSpec((d, tf), lambda i, c: (0, c)),
            pl.BlockSpec((d, tf), lambda i, c: (0, nc + c)),
            pl.BlockSpec((CONV_WIDTH, tf), lambda i, c: (0, c)),
            pl.BlockSpec((CONV_WIDTH, tf), lambda i, c: (0, nc + c)),
            pl.BlockSpec((1, tf), lambda i, c: (0, c)),
            pl.BlockSpec((1, tf), lambda i, c: (0, nc + c)),
        ],
        out_specs=pl.BlockSpec((tm, tf), lambda i, c: (i, c)),
        out_shape=jax.ShapeDtypeStruct((t, dff), BF16),
        scratch_shapes=[pltpu.VMEM((HALO + tm, d), BF16)],
        compiler_params=_cparams(("arbitrary", "arbitrary")),
        name="ffn_up",
    )(x2d, x2d, mod, g, w_up, w_up, conv_w, conv_w, conv_b, conv_b)

    tr = ROW_TILE
    rpb = seq // tr
    return pl.pallas_call(
        functools.partial(_ffn_down_kernel, final_norm=final_norm),
        grid=(t // tr,),
        in_specs=[
            pl.BlockSpec((tr, dff), lambda i: (i, 0)),
            pl.BlockSpec((dff, d), lambda i: (0, 0), pipeline_mode=pl.Buffered(1)),
            pl.BlockSpec((tr, d), lambda i: (i, 0)),
            pl.BlockSpec((1, 6, d), lambda i: (i // rpb, 0, 0)),
            pl.BlockSpec((1, d), lambda i: (0, 0)),
        ],
        out_specs=pl.BlockSpec((tr, d), lambda i: (i, 0)),
        out_shape=jax.ShapeDtypeStruct((t, d), F32),
        compiler_params=_cparams(("arbitrary",)),
        name="ffn_down",
    )(act, w_down, x2d, mod, final_g)


def _rope_tables(seq):
    inv = ROPE_THETA ** (-jnp.arange(0, HEAD_DIM, 2, dtype=F32) / HEAD_DIM)
    ang = jnp.arange(seq, dtype=F32)[:, None] * inv[None, :]
    cos, sin = jnp.cos(ang), jnp.sin(ang)
    return jnp.concatenate([cos, cos], axis=-1), jnp.concatenate([-sin, sin], axis=-1)


def _select_mask_table(seq):
    key_blk = (np.arange(seq) // SEL_BLOCK).reshape(seq, 1)
    return jnp.asarray(np.where(np.arange(LANES).reshape(1, LANES) == key_blk, NEG_INF, 0.0), dtype=BF16)


def _split_w_in(w):
    hd = HEAD_DIM
    sizes = [FOX_HEADS * hd] * 3 + [FOX_HEADS] + [NSA_HEADS * hd] + [NSA_KV_HEADS * hd] * 6 + [3 * NSA_HEADS]
    w = w.astype(BF16)
    fq, fk, fv, ff, nq, kc, vc, ks, vs, kw, vw, ng = jnp.split(w, [int(o) for o in np.cumsum(sizes)[:-1]], axis=-1)
    pad = jnp.zeros((w.shape[0], LANES - FOX_HEADS - 3 * NSA_HEADS), w.dtype)
    return jnp.concatenate([fq, fk, fv, nq, ks, kw, vs, vw, kc, vc, ff, ng, pad], axis=-1)


def kernel(x, c, attn_norm_g, ffn_norm_g, w_ada, b_ada, w_in, b_fgate, cmp_pos_k, cmp_pos_v,
           w_cmp1_k, w_cmp2_k, w_cmp1_v, w_cmp2_v, w_out, w_up, conv_w, conv_b, w_down, final_norm_g):
    batch, seq, d = x.shape
    depth = w_ada.shape[0]
    assert seq % ROW_TILE == 0 and seq % FFN_TILE == 0 and seq % FOX_TQ == 0 and seq % NSA_TK == 0 and seq // SEL_BLOCK == LANES // 4
    assert seq >= WIN_KEYS and w_down.shape[1] % FF_CHUNK == 0 and d % COL_CHUNK == 0

    cosf, sinf = _rope_tables(seq)
    emask = _select_mask_table(seq)
    tri = jnp.asarray(np.tril(np.ones((LANES, LANES))), dtype=BF16)
    mod_all = _ada_mod(c, w_ada, b_ada)

    x2d = x.reshape(batch * seq, d)
    n_fox = FOX_HEADS * HEAD_DIM
    for l in range(depth):
        mod = mod_all[l].reshape(batch, 6, d)
        zq, zc, zs = _in_proj(x2d, mod, attn_norm_g[l].reshape(1, d), _split_w_in(w_in[l]), cosf, sinf, seq)
        bias_row = jnp.zeros((1, LANES), F32).at[0, :FOX_HEADS].set(b_fgate[l])
        cumc = _fgate(zs, bias_row, tri, batch, seq)
        o_fox = _fox_attention(zq, cumc, batch, seq)
        o_nsa = _nsa_attention(zq, zc, zs, cmp_pos_k[l], cmp_pos_v[l],
                               w_cmp1_k[l].astype(BF16), w_cmp1_v[l].astype(BF16),
                               w_cmp2_k[l].astype(BF16), w_cmp2_v[l].astype(BF16), emask, batch, seq)
        w_o = w_out[l].astype(BF16)
        x2d = _out_proj(o_fox, o_nsa, w_o[:n_fox], w_o[n_fox:], x2d, mod, seq)
        x2d = _ffn(x2d, mod, ffn_norm_g[l].reshape(1, d), w_up[l].astype(BF16), conv_w[l],
                   conv_b[l].reshape(1, -1), w_down[l].astype(BF16), final_norm_g.reshape(1, d), seq,
                   final_norm=(l == depth - 1))
    return x2d.reshape(batch, seq, d)
```

```python
import functools
import math

import numpy as np
import jax
import jax.numpy as jnp
from jax import lax
from jax.experimental import pallas as pl
from jax.experimental.pallas import tpu as pltpu

HEAD_DIM = 128
FOX_HEADS = 8
NSA_HEADS = 8
NSA_KV_HEADS = 2
NSA_GROUP = NSA_HEADS // NSA_KV_HEADS
CMP_BLOCK = 32
CMP_STRIDE = 16
SEL_BLOCK = 64
SEL_TOPK = 16
WINDOW = 512
CONV_WIDTH = 3
ROPE_THETA = 10000.0
FORCE_BONUS = 1000.0
NEG_INF = -1e30
TINY = 1e-30
EPS = 1e-6
LOG2E = math.log2(math.e)

LANES = 128
SUBLANES = 8
VMEM_LIMIT_BYTES = 56 * 1024 * 1024

ROW_TILE = 512
FFN_TILE = 1024
COL_CHUNK = 512
NORM_SUBROWS = 16
NORM_UNROLL = 8
FOX_TQ = 512
FOX_GROUP = 2
NSA_TQ = 128
NSA_SEL_TQ = 1024
NSA_KV_PER_STEP = 2
NSA_TK = 512
WIN_KEYS = WINDOW + NSA_TQ
HALO = 2 * SUBLANES
FF_CHUNK = 512

BF16 = jnp.bfloat16
F32 = jnp.float32

ZQ_FQ, ZQ_FK, ZQ_FV, ZQ_NQ, ZQ_KS, ZQ_KW, ZQ_VS, ZQ_VW = 0, 8, 16, 24, 32, 34, 36, 38
ZQ_HEADS = 40
ZQ_ROPED = tuple(range(ZQ_NQ, ZQ_KW + NSA_KV_HEADS))
ZC_HEADS = 4
ZS_GATE0 = FOX_HEADS


def _cparams(sem):
    return pltpu.CompilerParams(dimension_semantics=sem, vmem_limit_bytes=VMEM_LIMIT_BYTES)


def _dot(a, b):
    return jnp.dot(a, b, preferred_element_type=F32)


def _dot_nt(a, b):
    return lax.dot_general(a, b, (((1,), (1,)), ((), ())), preferred_element_type=F32)


def _ada_kernel(c_ref, w_ref, b_ref, o_ref):
    c = c_ref[...]
    ca = (c * jax.nn.sigmoid(c)).astype(BF16)
    o_ref[0] = _dot(ca, w_ref[0].astype(BF16)) + b_ref[0]


def _ada_mod(c, w_ada, b_ada):
    depth, d, n = w_ada.shape
    b = c.shape[0]
    tn = 1024
    return pl.pallas_call(
        _ada_kernel,
        grid=(depth, n // tn),
        in_specs=[
            pl.BlockSpec((b, d), lambda l, j: (0, 0)),
            pl.BlockSpec((1, d, tn), lambda l, j: (l, 0, j)),
            pl.BlockSpec((1, 1, tn), lambda l, j: (l, 0, j)),
        ],
        out_specs=pl.BlockSpec((1, b, tn), lambda l, j: (l, 0, j)),
        out_shape=jax.ShapeDtypeStruct((depth, b, n), F32),
        compiler_params=_cparams(("arbitrary", "arbitrary")),
        name="ada_mod",
    )(c, w_ada, b_ada.reshape(depth, 1, n))


def _norm_mod_rows(x, gain, shift):
    ms = jnp.mean(x * x, axis=-1, keepdims=True)
    return (x * lax.rsqrt(ms + EPS)) * gain + shift


def _norm_mod_to_scratch(x_ref, h_scr, row0, n_rows, gain, shift):
    nr = NORM_SUBROWS
    def body(r, _):
        rows = pl.ds(pl.multiple_of(r * nr, nr), nr)
        dst = pl.ds(pl.multiple_of(row0 + r * nr, nr), nr)
        h_scr[dst, :] = _norm_mod_rows(x_ref[rows, :], gain, shift).astype(BF16)
        return 0
    lax.fori_loop(0, n_rows // nr, body, 0, unroll=NORM_UNROLL)


def _rope_tile(t, cosf, sinf):
    return t * cosf + pltpu.roll(t, HEAD_DIM // 2, axis=1) * sinf


def _in_proj_kernel(x_ref, mod_ref, g_ref, w_ref, cos_ref, sin_ref, zq_ref, zc_ref, zs_ref, h_scr):
    tm = x_ref.shape[0]
    _norm_mod_to_scratch(x_ref, h_scr, 0, tm, g_ref[...] * (1.0 + mod_ref[0, 1:2, :]), mod_ref[0, 0:1, :])
    heads_per_chunk = COL_CHUNK // HEAD_DIM
    for c0 in range(0, ZQ_HEADS, heads_per_chunk):
        acc = _dot(h_scr[...], w_ref[:, c0 * HEAD_DIM:(c0 + heads_per_chunk) * HEAD_DIM])
        for hh in range(heads_per_chunk):
            t = acc[:, hh * HEAD_DIM:(hh + 1) * HEAD_DIM]
            if c0 + hh in ZQ_ROPED:
                t = _rope_tile(t, cos_ref[...], sin_ref[...])
            zq_ref[:, (c0 + hh) * HEAD_DIM:(c0 + hh + 1) * HEAD_DIM] = t.astype(BF16)
    base = ZQ_HEADS * HEAD_DIM
    acc = _dot(h_scr[...], w_ref[:, base:base + ZC_HEADS * HEAD_DIM])
    for hh in range(ZC_HEADS):
        t = acc[:, hh * HEAD_DIM:(hh + 1) * HEAD_DIM]
        if hh < NSA_KV_HEADS:
            t = _rope_tile(t, cos_ref[...], sin_ref[...])
        zc_ref[:, hh * HEAD_DIM:(hh + 1) * HEAD_DIM] = t
    base += ZC_HEADS * HEAD_DIM
    zs_ref[...] = _dot(h_scr[...], w_ref[:, base:base + LANES])


def _in_proj(x2d, mod, g, wcat, cosf, sinf, seq):
    t, d = x2d.shape
    tm = ROW_TILE
    tpb = seq // tm
    nw = wcat.shape[1]
    return pl.pallas_call(
        _in_proj_kernel,
        grid=(t // tm,),
        in_specs=[
            pl.BlockSpec((tm, d), lambda i: (i, 0)),
            pl.BlockSpec((1, 6, d), lambda i: (i // tpb, 0, 0)),
            pl.BlockSpec((1, d), lambda i: (0, 0)),
            pl.BlockSpec((d, nw), lambda i: (0, 0), pipeline_mode=pl.Buffered(1)),
            pl.BlockSpec((tm, HEAD_DIM), lambda i: (i % tpb, 0)),
            pl.BlockSpec((tm, HEAD_DIM), lambda i: (i % tpb, 0)),
        ],
        out_specs=[
            pl.BlockSpec((tm, ZQ_HEADS * HEAD_DIM), lambda i: (i, 0)),
            pl.BlockSpec((tm, ZC_HEADS * HEAD_DIM), lambda i: (i, 0)),
            pl.BlockSpec((tm, LANES), lambda i: (i, 0)),
        ],
        out_shape=[
            jax.ShapeDtypeStruct((t, ZQ_HEADS * HEAD_DIM), BF16),
            jax.ShapeDtypeStruct((t, ZC_HEADS * HEAD_DIM), F32),
            jax.ShapeDtypeStruct((t, LANES), F32),
        ],
        scratch_shapes=[pltpu.VMEM((tm, d), BF16)],
        compiler_params=_cparams(("arbitrary",)),
        name="in_proj",
    )(x2d, mod, g, wcat, cosf, sinf)


def _split3(v):
    hi = v.astype(BF16)
    r1 = v - hi.astype(F32)
    mid = r1.astype(BF16)
    lo = (r1 - mid.astype(F32)).astype(BF16)
    return hi, mid, lo


def _fgate_kernel(zs_ref, bias_ref, tri_ref, cumc_ref):
    seq = zs_ref.shape[0]
    tri = tri_ref[...]
    carry = jnp.zeros((1, LANES), F32)
    for c in range(seq // LANES):
        zf = zs_ref[c * LANES:(c + 1) * LANES, :] + bias_ref[...]
        lf = jnp.minimum(zf, 0.0) - jnp.log1p(jnp.exp(-jnp.abs(zf)))
        hi, mid, lo = _split3(lf)
        cs = (_dot(tri, lo) + _dot(tri, mid)) + _dot(tri, hi) + carry
        carry = cs[LANES - 1:LANES, :]
        cumc_ref[c * LANES:(c + 1) * LANES, :] = cs


def _fgate(zs, bias_row, tri, batch, seq):
    t = zs.shape[0]
    return pl.pallas_call(
        _fgate_kernel,
        grid=(batch,),
        in_specs=[
            pl.BlockSpec((seq, LANES), lambda b: (b, 0)),
            pl.BlockSpec((1, LANES), lambda b: (0, 0)),
            pl.BlockSpec((LANES, LANES), lambda b: (0, 0)),
        ],
        out_specs=pl.BlockSpec((seq, LANES), lambda b: (b, 0)),
        out_shape=jax.ShapeDtypeStruct((t, LANES), F32),
        compiler_params=_cparams(("arbitrary",)),
        name="fgate_cumsum",
    )(zs, bias_row, tri)


def _lane_pick(tile, lane_idx):
    lane = lax.broadcasted_iota(jnp.int32, tile.shape, 1)
    return jnp.sum(jnp.where(lane == lane_idx, tile, 0.0), axis=1, keepdims=True)


def _online_step(carry, s2, v_ones):
    m, acc = carry
    m_new = jnp.maximum(m, jnp.max(s2, axis=1, keepdims=True))
    p = jnp.exp2(s2 - m_new)
    acc = jnp.exp2(m - m_new) * acc + _dot(p.astype(BF16), v_ones)
    return m_new, acc


def _online_init(rows):
    return jnp.full((rows, 1), NEG_INF, F32), jnp.zeros((rows, 2 * HEAD_DIM), F32)


def _online_finish(carry):
    _, acc = carry
    return acc[:, :HEAD_DIM] / acc[:, HEAD_DIM:HEAD_DIM + 1]


def _ones_lane0(rows):
    return jnp.where(lax.broadcasted_iota(jnp.int32, (rows, LANES), 1) == 0, 1.0, 0.0).astype(BF16)


def _bias_lanes(pieces, head):
    npc = len(pieces)

    def place(n_in):
        row = lax.broadcasted_iota(jnp.int32, (n_in * LANES, 2 * LANES), 0)
        col = lax.broadcasted_iota(jnp.int32, (n_in * LANES, 2 * LANES), 1)
        return row, col
    hi_mid = jnp.concatenate(pieces[:2], axis=1)
    row, col = place(2)
    pc = jnp.where(row >= LANES, 1, 0)
    hit = row - pc * LANES == head
    w_a = jnp.where(hit, jnp.where(col == pc, 1.0, jnp.where(col == LANES + npc + pc, -1.0, 0.0)), 0.0).astype(BF16)
    row, col = place(1)
    w_b = jnp.where(row == head,
                    jnp.where(col == npc - 1, 1.0, jnp.where(col == LANES + 2 * npc - 1, -1.0, 0.0)), 0.0).astype(BF16)
    lane = lax.broadcasted_iota(jnp.int32, (1, 2 * LANES), 1)
    in_range = lambda lo, hi: jnp.where(lane >= lo, jnp.where(lane < hi, 1.0, 0.0), 0.0)
    ones = in_range(npc, 2 * npc) + in_range(LANES, LANES + npc)
    return (_dot(hi_mid, w_a) + _dot(pieces[2], w_b) + ones).astype(BF16)


def _fox_kernel(q_ref, k_ref, v_ref, cum_ref, o_ref, qaug_scr, kaug_scr, vaug_scr):
    hb = pl.program_id(1)
    tile, grp = FOX_TQ, FOX_GROUP
    seq = q_ref.shape[0]
    scale = 1.0 / math.sqrt(HEAD_DIM)
    c2 = scale * LOG2E
    head = lambda h: slice(h * HEAD_DIM, (h + 1) * HEAD_DIM)

    pieces = _split3(cum_ref[...] * (1.0 / scale))
    ones = _ones_lane0(seq)
    for g in range(grp):
        bias = _bias_lanes(pieces, hb * grp + g)
        qaug_scr[:, head(2 * g)] = q_ref[:, head(g)]
        qaug_scr[:, head(2 * g + 1)] = bias[:, :LANES]
        kaug_scr[:, head(2 * g)] = k_ref[:, head(g)]
        kaug_scr[:, head(2 * g + 1)] = bias[:, LANES:]
        vaug_scr[:, head(2 * g)] = v_ref[:, head(g)]
        vaug_scr[:, head(2 * g + 1)] = ones

    causal = (lax.broadcasted_iota(jnp.int32, (tile, tile), 1) <= lax.broadcasted_iota(jnp.int32, (tile, tile), 0))
    for i in range(seq // tile):
        qrows = slice(i * tile, (i + 1) * tile)
        for g in range(grp):
            carry = _online_init(tile)
            for j in range(i + 1):
                krows = slice(j * tile, (j + 1) * tile)
                s2 = _dot_nt(qaug_scr[qrows, 2 * g * HEAD_DIM:(2 * g + 2) * HEAD_DIM],
                             kaug_scr[krows, 2 * g * HEAD_DIM:(2 * g + 2) * HEAD_DIM]) * c2
                if j == i:
                    s2 = jnp.where(causal, s2, NEG_INF)
                carry = _online_step(carry, s2, vaug_scr[krows, 2 * g * HEAD_DIM:(2 * g + 2) * HEAD_DIM])
            o_ref[qrows, head(g)] = _online_finish(carry).astype(o_ref.dtype)


def _fox_attention(zq, cumc, batch, seq):
    t = zq.shape[0]
    grp = FOX_GROUP
    gw = grp * HEAD_DIM
    return pl.pallas_call(
        _fox_kernel,
        grid=(batch, FOX_HEADS // grp),
        in_specs=[
            pl.BlockSpec((seq, gw), lambda b, h: (b, ZQ_FQ // grp + h)),
            pl.BlockSpec((seq, gw), lambda b, h: (b, ZQ_FK // grp + h)),
            pl.BlockSpec((seq, gw), lambda b, h: (b, ZQ_FV // grp + h)),
            pl.BlockSpec((seq, LANES), lambda b, h: (b, 0)),
        ],
        out_specs=pl.BlockSpec((seq, gw), lambda b, h: (b, h)),
        out_shape=jax.ShapeDtypeStruct((t, FOX_HEADS * HEAD_DIM), BF16),
        scratch_shapes=[pltpu.VMEM((seq, 2 * gw), BF16)] * 3,
        compiler_params=_cparams(("arbitrary", "arbitrary")),
        name="fox_attention",
    )(zq, zq, zq, cumc)


def _compress_to(src_ref, hk, pos_ref, w1_ref, w2_ref, flat_scr, dst_scr):
    half = CMP_BLOCK // 2
    n_rows = dst_scr.shape[0]
    cols = slice(hk * HEAD_DIM, (hk + 1) * HEAD_DIM)
    for jp in range(half):
        rows = src_ref[pl.ds(jp, n_rows, stride=CMP_STRIDE), :]
        flat_scr[:, jp * HEAD_DIM:(jp + 1) * HEAD_DIM] = (rows + pos_ref[jp:jp + 1, :]).astype(BF16)
        nxt = pltpu.roll(rows, n_rows - 1, axis=0)
        flat_scr[:, (half + jp) * HEAD_DIM:(half + jp + 1) * HEAD_DIM] = (
            nxt + pos_ref[half + jp:half + jp + 1, :]).astype(BF16)
    pre = _dot(flat_scr[...], w1_ref[...])
    dst_scr[:, cols] = _dot(jax.nn.gelu(pre).astype(BF16), w2_ref[...]).astype(BF16)


def _nsa_kernel(q_ref, ks_ref, kw_ref, vs_ref, vw_ref, *rest):
    nkv = NSA_KV_PER_STEP
    zs_ref, ocmp_ref, nots_ref, emask_ref, o_ref, ksaug_scr = rest
    hk0 = pl.program_id(1) * nkv
    i = pl.program_id(2)
    tq, tk = NSA_TQ, NSA_TK
    q0 = i * tq
    head = lambda h: slice(h * HEAD_DIM, (h + 1) * HEAD_DIM)

    @pl.when(i == 0)
    def _():
        for hk in range(nkv):
            ksaug_scr[:, head(2 * hk)] = ks_ref[:, head(hk)]
            ksaug_scr[:, head(2 * hk + 1)] = emask_ref[...]

    tile = functools.partial(_nsa_tile, q0=q0, hk0=hk0, q_ref=q_ref, kw_ref=kw_ref, vs_ref=vs_ref, vw_ref=vw_ref,
                             zs_ref=zs_ref, ocmp_ref=ocmp_ref, nots_ref=nots_ref, o_ref=o_ref, ksaug_scr=ksaug_scr)
    for n_full in range(ks_ref.shape[0] // tk):
        pl.when(q0 // tk == n_full)(functools.partial(tile, n_full))


def _nsa_tile(n_full, *, q0, hk0, q_ref, kw_ref, vs_ref, vw_ref, zs_ref, ocmp_ref, nots_ref, o_ref, ksaug_scr):
    grp, nkv = NSA_GROUP, NSA_KV_PER_STEP
    head = lambda h: slice(h * HEAD_DIM, (h + 1) * HEAD_DIM)

    def stack4(tile):
        return jnp.concatenate([tile] * grp, axis=0)

    gl = jax.nn.sigmoid(zs_ref[...])
    pre = []
    for hk in range(nkv):
        q4 = jnp.concatenate([q_ref[:, head(hk * grp + g)] for g in range(grp)], axis=0)
        o_cmp = jnp.concatenate([ocmp_ref[:, head(hk * grp + g)] for g in range(grp)], axis=0)
        pre.append((q4, jnp.concatenate([q4, stack4(nots_ref[:, head(hk)])], axis=1), o_cmp))
    _nsa_sweeps(n_full, pre, gl, q0=q0, hk0=hk0, kw_ref=kw_ref, vs_ref=vs_ref, vw_ref=vw_ref, o_ref=o_ref,
                ksaug_scr=ksaug_scr)


def _nsa_select_kernel(q_ref, *rest):
    nkv = NSA_KV_HEADS
    kcf_refs, vcf_refs = rest[:nkv], rest[nkv:2 * nkv]
    (posk_ref, posv_ref, w1k_ref, w1v_ref, w2k_ref, w2v_ref,
     ocmp_ref, nots_ref, kcmp_scr, vcmp_scr, flat_scr, tr_scr) = rest[2 * nkv:]
    i = pl.program_id(1)
    tq, grp = NSA_SEL_TQ, NSA_GROUP
    n_selb = SEL_BLOCK // CMP_STRIDE
    n_sel = LANES // n_selb
    c2 = LOG2E / math.sqrt(HEAD_DIM)
    q0 = i * tq
    head = lambda h: slice(h * HEAD_DIM, (h + 1) * HEAD_DIM)

    @pl.when(i == 0)
    def _():
        for hk in range(nkv):
            _compress_to(kcf_refs[hk], hk, posk_ref, w1k_ref, w2k_ref, flat_scr, kcmp_scr)
            _compress_to(vcf_refs[hk], hk, posv_ref, w1v_ref, w2v_ref, flat_scr, vcmp_scr)

    rloc = lax.broadcasted_iota(jnp.int32, (tq, LANES), 0)
    lane = lax.broadcasted_iota(jnp.int32, (tq, LANES), 1)
    cvalid = jnp.where((lane * CMP_STRIDE + (CMP_BLOCK - 1)) <= q0 + rloc, 1.0, 0.0)
    cvalid4 = jnp.concatenate([cvalid] * grp, axis=0) > 0.5
    blk = lax.broadcasted_iota(jnp.int32, (n_sel, tq), 0)
    cur = lax.shift_right_logical(q0 + lax.broadcasted_iota(jnp.int32, (n_sel, tq), 1), int(math.log2(SEL_BLOCK)))
    forced = jnp.where(blk == 0, 1.0, jnp.where(blk == cur, 1.0, jnp.where(blk == cur - 1, 1.0, 0.0)))

    for hk in range(nkv):
        q4 = jnp.concatenate([q_ref[:, head(hk * grp + g)] for g in range(grp)], axis=0)
        sc = jnp.where(cvalid4, _dot_nt(q4, kcmp_scr[:, head(hk)]) * c2, NEG_INF)
        pc = jnp.where(cvalid4, jnp.exp2(sc - jnp.max(sc, axis=1, keepdims=True)), 0.0)
        pc = pc / jnp.maximum(jnp.sum(pc, axis=1, keepdims=True), TINY)
        o_cmp = _dot(pc.astype(BF16), vcmp_scr[:, head(hk)])
        psum = (pc[0:tq] + pc[tq:2 * tq]) + (pc[2 * tq:3 * tq] + pc[3 * tq:4 * tq])
        nsub = tq // LANES
        for u in range(nsub):
            tr_scr[hk * nsub + u] = psum[u * LANES:(u + 1) * LANES, :].T
        pooled = lambda t: ((tr_scr[t, pl.ds(0, n_sel, stride=n_selb), :] + tr_scr[t, pl.ds(1, n_sel, stride=n_selb), :])
                            + (tr_scr[t, pl.ds(2, n_sel, stride=n_selb), :] + tr_scr[t, pl.ds(3, n_sel, stride=n_selb), :]))
        imp = jnp.concatenate([pooled(hk * nsub + u) for u in range(nsub)], axis=1)
        score = jnp.where(blk <= cur, imp + forced * FORCE_BONUS, NEG_INF)
        rank = jnp.zeros((n_sel, tq), F32)
        for kk in range(n_sel):
            row = score[kk:kk + 1, :]
            earlier = jnp.where(blk > kk, 1.0, 0.0)
            rank = rank + jnp.where(row > score, 1.0, jnp.where(row == score, earlier, 0.0))
        dropped = jnp.where(rank < float(SEL_TOPK), 0.0, 1.0)
        pieces = []
        for u in range(nsub):
            tr_scr[hk * nsub + u, 0:n_sel, :] = dropped[:, u * LANES:(u + 1) * LANES]
            tr_scr[hk * nsub + u, n_sel:, :] = jnp.zeros((LANES - n_sel, LANES), F32)
            pieces.append(tr_scr[hk * nsub + u].T)
        nots_ref[:, head(hk)] = jnp.concatenate(pieces, axis=0).astype(BF16)
        for g in range(grp):
            ocmp_ref[:, head(hk * grp + g)] = o_cmp[g * tq:(g + 1) * tq, :]


def _nsa_sweeps(n_full, pre, gl, *, q0, hk0, kw_ref, vs_ref, vw_ref, o_ref, ksaug_scr):
    tq, tk, grp, nkv = NSA_TQ, NSA_TK, NSA_GROUP, NSA_KV_PER_STEP
    rows4 = grp * tq
    c2 = LOG2E / math.sqrt(HEAD_DIM)
    head = lambda h: slice(h * HEAD_DIM, (h + 1) * HEAD_DIM)

    def stack4(tile):
        return jnp.concatenate([tile] * grp, axis=0)

    def sel_step(j, carries, bias4):
        rows = slice(j * tk, (j + 1) * tk)
        out = []
        for hk in range(nkv):
            s2 = _dot_nt(pre[hk][1], ksaug_scr[rows, 2 * hk * HEAD_DIM:(2 * hk + 2) * HEAD_DIM]) * c2
            if bias4 is not None:
                s2 = s2 + bias4
            out.append(_online_step(carries[hk], s2, jnp.concatenate([vs_ref[rows, head(hk)], ones_tk], axis=1)))
        return tuple(out)

    ones_tk = _ones_lane0(tk)
    carries = tuple(_online_init(rows4) for _ in range(nkv))
    for j in range(n_full):
        carries = sel_step(j, carries, None)
    rk = lax.broadcasted_iota(jnp.int32, (tq, tk), 0)
    ck = lax.broadcasted_iota(jnp.int32, (tq, tk), 1)
    causal4 = stack4(jnp.where(n_full * tk + ck <= q0 + rk, 0.0, NEG_INF))
    carries = sel_step(n_full, carries, causal4)

    w0 = pl.multiple_of(jnp.maximum(q0 - WINDOW, 0), tq)
    rw = lax.broadcasted_iota(jnp.int32, (tq, WIN_KEYS), 0)
    cw = lax.broadcasted_iota(jnp.int32, (tq, WIN_KEYS), 1)
    dist = (q0 - w0) + rw - cw
    wbias4 = stack4(jnp.where(jnp.abs(2 * dist - (WINDOW - 1)) <= (WINDOW - 1), 0.0, NEG_INF))
    wrows = pl.ds(w0, WIN_KEYS)

    for hk in range(nkv):
        q4, _, o_cmp = pre[hk]
        o_sel = _online_finish(carries[hk])
        s2 = _dot_nt(q4, kw_ref[wrows, head(hk)]) * c2 + wbias4
        vw_ones = jnp.concatenate([vw_ref[wrows, head(hk)], _ones_lane0(WIN_KEYS)], axis=1)
        o_win = _online_finish(_online_step(_online_init(rows4), s2, vw_ones))
        gate = lambda c: jnp.concatenate(
            [_lane_pick(gl, ZS_GATE0 + 3 * ((hk0 + hk) * grp + g) + c) for g in range(grp)], axis=0)
        o4 = gate(0) * o_cmp + gate(1) * o_sel + gate(2) * o_win
        for g in range(grp):
            o_ref[:, head(hk * grp + g)] = o4[g * tq:(g + 1) * tq, :].astype(o_ref.dtype)


def _nsa_attention(zq, zc, zs, posk, posv, w1k, w1v, w2k, w2v, emask, batch, seq):
    t = zq.shape[0]
    all_q = NSA_HEADS * HEAD_DIM
    all_kv = NSA_KV_HEADS * HEAD_DIM

    ts = NSA_SEL_TQ
    ns = seq // ts
    const2 = lambda shape: pl.BlockSpec(shape, lambda b, i: (0,) * len(shape))
    o_cmp, notsel = pl.pallas_call(
        _nsa_select_kernel,
        grid=(batch, ns),
        in_specs=[
            pl.BlockSpec((ts, all_q), lambda b, i: (b * ns + i, ZQ_NQ * HEAD_DIM // all_q)),
            *[pl.BlockSpec((seq, HEAD_DIM), functools.partial(lambda b, i, h: (b, h), h=h)) for h in range(ZC_HEADS)],
            const2((CMP_BLOCK, HEAD_DIM)), const2((CMP_BLOCK, HEAD_DIM)),
            const2((CMP_BLOCK * HEAD_DIM, HEAD_DIM)), const2((CMP_BLOCK * HEAD_DIM, HEAD_DIM)),
            const2((HEAD_DIM, HEAD_DIM)), const2((HEAD_DIM, HEAD_DIM)),
        ],
        out_specs=[
            pl.BlockSpec((ts, all_q), lambda b, i: (b * ns + i, 0)),
            pl.BlockSpec((ts, all_kv), lambda b, i: (b * ns + i, 0)),
        ],
        out_shape=[jax.ShapeDtypeStruct((t, all_q), F32), jax.ShapeDtypeStruct((t, all_kv), BF16)],
        scratch_shapes=[
            pltpu.VMEM((LANES, all_kv), BF16),
            pltpu.VMEM((LANES, all_kv), BF16),
            pltpu.VMEM((LANES, CMP_BLOCK * HEAD_DIM), BF16),
            pltpu.VMEM((NSA_KV_HEADS * (ts // LANES), LANES, LANES), F32),
        ],
        compiler_params=_cparams(("arbitrary", "arbitrary")),
        name="nsa_select",
    )(zq, zc, zc, zc, zc, posk, posv, w1k, w1v, w2k, w2v)

    tq, nkv = NSA_TQ, NSA_KV_PER_STEP
    nq = seq // tq
    qw = nkv * NSA_GROUP * HEAD_DIM
    kvw = nkv * HEAD_DIM
    kv = lambda col: pl.BlockSpec((seq, kvw), lambda b, h, i: (b, col // nkv + h))
    return pl.pallas_call(
        _nsa_kernel,
        grid=(batch, NSA_KV_HEADS // nkv, nq),
        in_specs=[
            pl.BlockSpec((tq, qw), lambda b, h, i: (b * nq + i, ZQ_NQ * HEAD_DIM // qw + h)),
            kv(ZQ_KS), kv(ZQ_KW), kv(ZQ_VS), kv(ZQ_VW),
            pl.BlockSpec((tq, LANES), lambda b, h, i: (b * nq + i, 0)),
            pl.BlockSpec((tq, qw), lambda b, h, i: (b * nq + i, h)),
            pl.BlockSpec((tq, kvw), lambda b, h, i: (b * nq + i, h)),
            pl.BlockSpec((seq, LANES), lambda b, h, i: (0, 0)),
        ],
        out_specs=pl.BlockSpec((tq, qw), lambda b, h, i: (b * nq + i, h)),
        out_shape=jax.ShapeDtypeStruct((t, all_q), BF16),
        scratch_shapes=[pltpu.VMEM((seq, 2 * kvw), BF16)],
        compiler_params=_cparams(("arbitrary", "arbitrary", "arbitrary")),
        name="nsa_attention",
    )(zq, zq, zq, zq, zq, zs, o_cmp, notsel, emask)


def _out_proj_kernel(of_ref, on_ref, wf_ref, wn_ref, x_ref, mod_ref, o_ref):
    n = o_ref.shape[1]
    for c0 in range(0, n, COL_CHUNK):
        cols = slice(c0, c0 + COL_CHUNK)
        acc = _dot(of_ref[...], wf_ref[:, cols]) + _dot(on_ref[...], wn_ref[:, cols])
        o_ref[:, cols] = x_ref[:, cols] + mod_ref[0, 2:3, cols] * acc


def _out_proj(o_fox, o_nsa, w_fox, w_nsa, x2d, mod, seq):
    t, d = x2d.shape
    tm = ROW_TILE
    tpb = seq // tm
    kf, kn = o_fox.shape[1], o_nsa.shape[1]
    return pl.pallas_call(
        _out_proj_kernel,
        grid=(t // tm,),
        in_specs=[
            pl.BlockSpec((tm, kf), lambda i: (i, 0)),
            pl.BlockSpec((tm, kn), lambda i: (i, 0)),
            pl.BlockSpec((kf, d), lambda i: (0, 0), pipeline_mode=pl.Buffered(1)),
            pl.BlockSpec((kn, d), lambda i: (0, 0), pipeline_mode=pl.Buffered(1)),
            pl.BlockSpec((tm, d), lambda i: (i, 0)),
            pl.BlockSpec((1, 6, d), lambda i: (i // tpb, 0, 0)),
        ],
        out_specs=pl.BlockSpec((tm, d), lambda i: (i, 0)),
        out_shape=jax.ShapeDtypeStruct((t, d), F32),
        compiler_params=_cparams(("arbitrary",)),
        name="out_proj",
    )(o_fox, o_nsa, w_fox, w_nsa, x2d, mod)


def _conv_rows(u, cw_ref, cb_ref):
    y = cb_ref[...] + cw_ref[0:1, :] * pltpu.roll(u, 2, axis=0)
    y = y + cw_ref[1:2, :] * pltpu.roll(u, 1, axis=0)
    return y + cw_ref[2:3, :] * u


def _ffn_up_kernel(x_ref, xh_ref, mod_ref, g_ref, wg_ref, wv_ref, cwg_ref, cwv_ref, cbg_ref, cbv_ref,
                   a_ref, h_scr, *, tiles_per_batch):
    i = pl.program_id(0)
    c = pl.program_id(1)
    tm = x_ref.shape[0]

    @pl.when(c == 0)
    def _():
        gain, shift = g_ref[...] * (1.0 + mod_ref[0, 4:5, :]), mod_ref[0, 3:4, :]
        halo = _norm_mod_rows(xh_ref[...], gain, shift)
        first = (i % tiles_per_batch) == 0
        h_scr[0:HALO, :] = jnp.where(first, 0.0, halo).astype(BF16)
        _norm_mod_to_scratch(x_ref, h_scr, HALO, tm, gain, shift)

    h = h_scr[...]
    yg = _conv_rows(_dot(h, wg_ref[...]), cwg_ref, cbg_ref)[HALO:, :]
    yv = _conv_rows(_dot(h, wv_ref[...]), cwv_ref, cbv_ref)[HALO:, :]
    a_ref[...] = ((yg * jax.nn.sigmoid(yg)) * yv).astype(a_ref.dtype)


def _ffn_down_kernel(a_ref, w_ref, x_ref, mod_ref, fg_ref, o_ref, *, final_norm):
    n = o_ref.shape[1]
    sumsq = jnp.zeros((o_ref.shape[0], 1), F32)
    for c0 in range(0, n, COL_CHUNK):
        cols = slice(c0, c0 + COL_CHUNK)
        y = x_ref[:, cols] + mod_ref[0, 5:6, cols] * _dot(a_ref[...], w_ref[:, cols])
        o_ref[:, cols] = y
        if final_norm:
            sumsq = sumsq + jnp.sum(y * y, axis=1, keepdims=True)
    if final_norm:
        r = lax.rsqrt(sumsq * (1.0 / n) + EPS)
        for c0 in range(0, n, COL_CHUNK):
            cols = slice(c0, c0 + COL_CHUNK)
            o_ref[:, cols] = (o_ref[:, cols] * r) * fg_ref[:, cols]


def _ffn(x2d, mod, g, w_up, conv_w, conv_b, w_down, final_g, seq, final_norm):
    t, d = x2d.shape
    dff = w_down.shape[0]
    tm, tf = FFN_TILE, FF_CHUNK
    tpb = seq // tm
    nc = dff // tf
    hb = tm // HALO
    act = pl.pallas_call(
        functools.partial(_ffn_up_kernel, tiles_per_batch=tpb),
        grid=(t // tm, nc),
        in_specs=[
            pl.BlockSpec((tm, d), lambda i, c: (i, 0)),
            pl.BlockSpec((HALO, d), lambda i, c: (jnp.maximum(i * hb - 1, 0), 0)),
            pl.BlockSpec((1, 6, d), lambda i, c: (i // tpb, 0, 0)),
            pl.BlockSpec((1, d), lambda i, c: (0, 0)),
            pl.BlockSpec((d, tf), lambda i, c: (0, c)),
            pl.BlockSpec((d, tf), lambda i, c: (0, nc + c)),
            pl.BlockSpec((CONV_WIDTH, tf), lambda i, c: (0, c)),
            pl.BlockSpec((CONV_WIDTH, tf), lambda i, c: (0, nc + c)),
            pl.BlockSpec((1, tf), lambda i, c: (0, c)),
            pl.BlockSpec((1, tf), lambda i, c: (0, nc + c)),
        ],
        out_specs=pl.BlockSpec((tm, tf), lambda i, c: (i, c)),
        out_shape=jax.ShapeDtypeStruct((t, dff), BF16),
        scratch_shapes=[pltpu.VMEM((HALO + tm, d), BF16)],
        compiler_params=_cparams(("arbitrary", "arbitrary")),
        name="ffn_up",
    )(x2d, x2d, mod, g, w_up, w_up, conv_w, conv_w, conv_b, conv_b)

    tr = ROW_TILE
    rpb = seq // tr
    return pl.pallas_call(
        functools.partial(_ffn_down_kernel, final_norm=final_norm),
        grid=(t // tr,),
        in_specs=[
            pl.BlockSpec((tr, dff), lambda i: (i, 0)),
            pl.BlockSpec((dff, d), lambda i: (0, 0), pipeline_mode=pl.Buffered(1)),
            pl.BlockSpec((tr, d), lambda i: (i, 0)),
            pl.BlockSpec((1, 6, d), lambda i: (i // rpb, 0, 0)),
            pl.BlockSpec((1, d), lambda i: (0, 0)),
        ],
        out_specs=pl.BlockSpec((tr, d), lambda i: (i, 0)),
        out_shape=jax.ShapeDtypeStruct((t, d), F32),
        compiler_params=_cparams(("arbitrary",)),
        name="ffn_down",
    )(act, w_down, x2d, mod, final_g)


def _rope_tables(seq):
    inv = ROPE_THETA ** (-jnp.arange(0, HEAD_DIM, 2, dtype=F32) / HEAD_DIM)
    ang = jnp.arange(seq, dtype=F32)[:, None] * inv[None, :]
    cos, sin = jnp.cos(ang), jnp.sin(ang)
    return jnp.concatenate([cos, cos], axis=-1), jnp.concatenate([-sin, sin], axis=-1)


def _select_mask_table(seq):
    key_blk = (np.arange(seq) // SEL_BLOCK).reshape(seq, 1)
    return jnp.asarray(np.where(np.arange(LANES).reshape(1, LANES) == key_blk, NEG_INF, 0.0), dtype=BF16)


def _split_w_in(w):
    hd = HEAD_DIM
    sizes = [FOX_HEADS * hd] * 3 + [FOX_HEADS] + [NSA_HEADS * hd] + [NSA_KV_HEADS * hd] * 6 + [3 * NSA_HEADS]
    w = w.astype(BF16)
    fq, fk, fv, ff, nq, kc, vc, ks, vs, kw, vw, ng = jnp.split(w, [int(o) for o in np.cumsum(sizes)[:-1]], axis=-1)
    pad = jnp.zeros((w.shape[0], LANES - FOX_HEADS - 3 * NSA_HEADS), w.dtype)
    return jnp.concatenate([fq, fk, fv, nq, ks, kw, vs, vw, kc, vc, ff, ng, pad], axis=-1)


def kernel(x, c, attn_norm_g, ffn_norm_g, w_ada, b_ada, w_in, b_fgate, cmp_pos_k, cmp_pos_v,
           w_cmp1_k, w_cmp2_k, w_cmp1_v, w_cmp2_v, w_out, w_up, conv_w, conv_b, w_down, final_norm_g):
    batch, seq, d = x.shape
    depth = w_ada.shape[0]
    assert seq % ROW_TILE == 0 and seq % FFN_TILE == 0 and seq % FOX_TQ == 0 and seq % NSA_TK == 0 and seq // SEL_BLOCK == LANES // 4
    assert seq >= WIN_KEYS and w_down.shape[1] % FF_CHUNK == 0 and d % COL_CHUNK == 0

    cosf, sinf = _rope_tables(seq)
    emask = _select_mask_table(seq)
    tri = jnp.asarray(np.tril(np.ones((LANES, LANES))), dtype=BF16)
    mod_all = _ada_mod(c, w_ada, b_ada)

    x2d = x.reshape(batch * seq, d)
    n_fox = FOX_HEADS * HEAD_DIM
    for l in range(depth):
        mod = mod_all[l].reshape(batch, 6, d)
        zq, zc, zs = _in_proj(x2d, mod, attn_norm_g[l].reshape(1, d), _split_w_in(w_in[l]), cosf, sinf, seq)
        bias_row = jnp.zeros((1, LANES), F32).at[0, :FOX_HEADS].set(b_fgate[l])
        cumc = _fgate(zs, bias_row, tri, batch, seq)
        o_fox = _fox_attention(zq, cumc, batch, seq)
        o_nsa = _nsa_attention(zq, zc, zs, cmp_pos_k[l], cmp_pos_v[l],
                               w_cmp1_k[l].astype(BF16), w_cmp1_v[l].astype(BF16),
                               w_cmp2_k[l].astype(BF16), w_cmp2_v[l].astype(BF16), emask, batch, seq)
        w_o = w_out[l].astype(BF16)
        x2d = _out_proj(o_fox, o_nsa, w_o[:n_fox], w_o[n_fox:], x2d, mod, seq)
        x2d = _ffn(x2d, mod, ffn_norm_g[l].reshape(1, d), w_up[l].astype(BF16), conv_w[l],
                   conv_b[l].reshape(1, -1), w_down[l].astype(BF16), final_norm_g.reshape(1, d), seq,
                   final_norm=(l == depth - 1))
    return x2d.reshape(batch, seq, d)
```

```python
import functools
import math

import numpy as np
import jax
import jax.numpy as jnp
from jax import lax
from jax.experimental import pallas as pl
from jax.experimental.pallas import tpu as pltpu

HEAD_DIM = 128
FOX_HEADS = 8
NSA_HEADS = 8
NSA_KV_HEADS = 2
NSA_GROUP = NSA_HEADS // NSA_KV_HEADS
CMP_BLOCK = 32
CMP_STRIDE = 16
SEL_BLOCK = 64
SEL_TOPK = 16
WINDOW = 512
CONV_WIDTH = 3
ROPE_THETA = 10000.0
FORCE_BONUS = 1000.0
NEG_INF = -1e30
TINY = 1e-30
EPS = 1e-6
LOG2E = math.log2(math.e)

LANES = 128
SUBLANES = 8
VMEM_LIMIT_BYTES = 56 * 1024 * 1024

ROW_TILE = 512
FFN_TILE = 1024
COL_CHUNK = 512
NORM_SUBROWS = 16
NORM_UNROLL = 8
FOX_TQ = 512
FOX_GROUP = 2
NSA_TQ = 128
NSA_SEL_TQ = 1024
NSA_KV_PER_STEP = 2
NSA_TK = 512
WIN_KEYS = WINDOW + NSA_TQ
HALO = 2 * SUBLANES
FF_CHUNK = 512

BF16 = jnp.bfloat16
F32 = jnp.float32

ZQ_FQ, ZQ_FK, ZQ_FV, ZQ_NQ, ZQ_KC, ZQ_VC, ZQ_KS, ZQ_VS, ZQ_KW, ZQ_VW = 0, 8, 16, 24, 32, 34, 36, 38, 40, 42
ZQ_HEADS = 44
ZQ_ROPED = (*range(ZQ_NQ, ZQ_NQ + NSA_HEADS), *(first + u for first in (ZQ_KC, ZQ_KS, ZQ_KW)
                                                 for u in range(NSA_KV_HEADS)))
ZC_HEADS = 4
ZS_GATE0 = FOX_HEADS


def _cparams(sem):
    return pltpu.CompilerParams(dimension_semantics=sem, vmem_limit_bytes=VMEM_LIMIT_BYTES)


def _dot(a, b):
    return jnp.dot(a, b, preferred_element_type=F32)


def _dot_nt(a, b):
    return lax.dot_general(a, b, (((1,), (1,)), ((), ())), preferred_element_type=F32)


def _ada_kernel(c_ref, w_ref, b_ref, o_ref):
    c = c_ref[...]
    ca = (c * jax.nn.sigmoid(c)).astype(BF16)
    o_ref[0] = _dot(ca, w_ref[0].astype(BF16)) + b_ref[0]


def _ada_mod(c, w_ada, b_ada):
    depth, d, n = w_ada.shape
    b = c.shape[0]
    tn = 1024
    return pl.pallas_call(
        _ada_kernel,
        grid=(depth, n // tn),
        in_specs=[
            pl.BlockSpec((b, d), lambda l, j: (0, 0)),
            pl.BlockSpec((1, d, tn), lambda l, j: (l, 0, j)),
            pl.BlockSpec((1, 1, tn), lambda l, j: (l, 0, j)),
        ],
        out_specs=pl.BlockSpec((1, b, tn), lambda l, j: (l, 0, j)),
        out_shape=jax.ShapeDtypeStruct((depth, b, n), F32),
        compiler_params=_cparams(("arbitrary", "arbitrary")),
        name="ada_mod",
    )(c, w_ada, b_ada.reshape(depth, 1, n))


def _norm_mod_rows(x, gain, shift):
    ms = jnp.mean(x * x, axis=-1, keepdims=True)
    return (x * lax.rsqrt(ms + EPS)) * gain + shift


def _norm_mod_to_scratch(x_ref, h_scr, row0, n_rows, gain, shift):
    nr = NORM_SUBROWS
    def body(r, _):
        rows = pl.ds(pl.multiple_of(r * nr, nr), nr)
        dst = pl.ds(pl.multiple_of(row0 + r * nr, nr), nr)
        h_scr[dst, :] = _norm_mod_rows(x_ref[rows, :], gain, shift).astype(BF16)
        return 0
    lax.fori_loop(0, n_rows // nr, body, 0, unroll=NORM_UNROLL)


def _rope_tile(t, cosf, sinf):
    return t * cosf + pltpu.roll(t, HEAD_DIM // 2, axis=1) * sinf


def _in_proj_kernel(x_ref, mod_ref, g_ref, w_ref, cos_ref, sin_ref, zq_ref, zc_ref, zs_ref, h_scr):
    tm = x_ref.shape[0]
    _norm_mod_to_scratch(x_ref, h_scr, 0, tm, g_ref[...] * (1.0 + mod_ref[0, 1:2, :]), mod_ref[0, 0:1, :])
    heads_per_chunk = COL_CHUNK // HEAD_DIM
    for c0 in range(0, ZQ_HEADS, heads_per_chunk):
        acc = _dot(h_scr[...], w_ref[0, :, c0 * HEAD_DIM:(c0 + heads_per_chunk) * HEAD_DIM])
        for hh in range(heads_per_chunk):
            head = c0 + hh
            t = acc[:, hh * HEAD_DIM:(hh + 1) * HEAD_DIM]
            if head in ZQ_ROPED:
                t = _rope_tile(t, cos_ref[...], sin_ref[...])
            zq_ref[:, head * HEAD_DIM:(head + 1) * HEAD_DIM] = t.astype(BF16)
            if ZQ_KC <= head < ZQ_KC + ZC_HEADS:
                zc_ref[:, (head - ZQ_KC) * HEAD_DIM:(head - ZQ_KC + 1) * HEAD_DIM] = t
    base = ZQ_HEADS * HEAD_DIM
    zs_ref[...] = _dot(h_scr[...], w_ref[0, :, base:base + LANES])


def _in_proj(x2d, mod, g, wcat, cosf, sinf, seq, layer):
    t, d = x2d.shape
    tm = ROW_TILE
    tpb = seq // tm
    nw = wcat.shape[2]
    return pl.pallas_call(
        _in_proj_kernel,
        grid=(t // tm,),
        in_specs=[
            pl.BlockSpec((tm, d), lambda i: (i, 0)),
            pl.BlockSpec((1, 6, d), lambda i: (i // tpb, 0, 0)),
            pl.BlockSpec((1, d), lambda i: (0, 0)),
            pl.BlockSpec((1, d, nw), lambda i: (layer, 0, 0), pipeline_mode=pl.Buffered(1)),
            pl.BlockSpec((tm, HEAD_DIM), lambda i: (i % tpb, 0)),
            pl.BlockSpec((tm, HEAD_DIM), lambda i: (i % tpb, 0)),
        ],
        out_specs=[
            pl.BlockSpec((tm, ZQ_HEADS * HEAD_DIM), lambda i: (i, 0)),
            pl.BlockSpec((tm, ZC_HEADS * HEAD_DIM), lambda i: (i, 0)),
            pl.BlockSpec((tm, LANES), lambda i: (i, 0)),
        ],
        out_shape=[
            jax.ShapeDtypeStruct((t, ZQ_HEADS * HEAD_DIM), BF16),
            jax.ShapeDtypeStruct((t, ZC_HEADS * HEAD_DIM), F32),
            jax.ShapeDtypeStruct((t, LANES), F32),
        ],
        scratch_shapes=[pltpu.VMEM((tm, d), BF16)],
        compiler_params=_cparams(("arbitrary",)),
        name="in_proj",
    )(x2d, mod, g, wcat, cosf, sinf)


def _split3(v):
    hi = v.astype(BF16)
    r1 = v - hi.astype(F32)
    mid = r1.astype(BF16)
    lo = (r1 - mid.astype(F32)).astype(BF16)
    return hi, mid, lo


def _fgate_kernel(zs_ref, bias_ref, tri_ref, cumc_ref):
    seq = zs_ref.shape[0]
    tri = tri_ref[...]
    carry = jnp.zeros((1, LANES), F32)
    for c in range(seq // LANES):
        zf = zs_ref[c * LANES:(c + 1) * LANES, :] + bias_ref[...]
        lf = jnp.minimum(zf, 0.0) - jnp.log1p(jnp.exp(-jnp.abs(zf)))
        hi, mid, lo = _split3(lf)
        cs = (_dot(tri, lo) + _dot(tri, mid)) + _dot(tri, hi) + carry
        carry = cs[LANES - 1:LANES, :]
        cumc_ref[c * LANES:(c + 1) * LANES, :] = cs


def _fgate(zs, bias_row, tri, batch, seq):
    t = zs.shape[0]
    return pl.pallas_call(
        _fgate_kernel,
        grid=(batch,),
        in_specs=[
            pl.BlockSpec((seq, LANES), lambda b: (b, 0)),
            pl.BlockSpec((1, LANES), lambda b: (0, 0)),
            pl.BlockSpec((LANES, LANES), lambda b: (0, 0)),
        ],
        out_specs=pl.BlockSpec((seq, LANES), lambda b: (b, 0)),
        out_shape=jax.ShapeDtypeStruct((t, LANES), F32),
        compiler_params=_cparams(("arbitrary",)),
        name="fgate_cumsum",
    )(zs, bias_row, tri)


def _lane_pick(tile, lane_idx):
    lane = lax.broadcasted_iota(jnp.int32, tile.shape, 1)
    return jnp.sum(jnp.where(lane == lane_idx, tile, 0.0), axis=1, keepdims=True)


def _online_step(carry, s2, v_ones):
    m, acc = carry
    m_new = jnp.maximum(m, jnp.max(s2, axis=1, keepdims=True))
    p = jnp.exp2(s2 - m_new)
    acc = jnp.exp2(m - m_new) * acc + _dot(p.astype(BF16), v_ones)
    return m_new, acc


def _online_init(rows):
    return jnp.full((rows, 1), NEG_INF, F32), jnp.zeros((rows, 2 * HEAD_DIM), F32)


def _online_finish(carry):
    _, acc = carry
    return acc[:, :HEAD_DIM] / acc[:, HEAD_DIM:HEAD_DIM + 1]


def _ones_lane0(rows):
    return jnp.where(lax.broadcasted_iota(jnp.int32, (rows, LANES), 1) == 0, 1.0, 0.0).astype(BF16)


def _bias_lanes(pieces, head):
    npc = len(pieces)

    def place(n_in):
        row = lax.broadcasted_iota(jnp.int32, (n_in * LANES, 2 * LANES), 0)
        col = lax.broadcasted_iota(jnp.int32, (n_in * LANES, 2 * LANES), 1)
        return row, col
    hi_mid = jnp.concatenate(pieces[:2], axis=1)
    row, col = place(2)
    pc = jnp.where(row >= LANES, 1, 0)
    hit = row - pc * LANES == head
    w_a = jnp.where(hit, jnp.where(col == pc, 1.0, jnp.where(col == LANES + npc + pc, -1.0, 0.0)), 0.0).astype(BF16)
    row, col = place(1)
    w_b = jnp.where(row == head,
                    jnp.where(col == npc - 1, 1.0, jnp.where(col == LANES + 2 * npc - 1, -1.0, 0.0)), 0.0).astype(BF16)
    lane = lax.broadcasted_iota(jnp.int32, (1, 2 * LANES), 1)
    in_range = lambda lo, hi: jnp.where(lane >= lo, jnp.where(lane < hi, 1.0, 0.0), 0.0)
    ones = in_range(npc, 2 * npc) + in_range(LANES, LANES + npc)
    return (_dot(hi_mid, w_a) + _dot(pieces[2], w_b) + ones).astype(BF16)


def _fox_kernel(q_ref, k_ref, v_ref, cum_ref, o_ref, qaug_scr, kaug_scr, vaug_scr):
    hb = pl.program_id(1)
    tile, grp = FOX_TQ, FOX_GROUP
    seq = q_ref.shape[0]
    scale = 1.0 / math.sqrt(HEAD_DIM)
    c2 = scale * LOG2E
    head = lambda h: slice(h * HEAD_DIM, (h + 1) * HEAD_DIM)

    pieces = _split3(cum_ref[...] * (1.0 / scale))
    ones = _ones_lane0(seq)
    for g in range(grp):
        bias = _bias_lanes(pieces, hb * grp + g)
        qaug_scr[:, head(2 * g)] = q_ref[:, head(g)]
        qaug_scr[:, head(2 * g + 1)] = bias[:, :LANES]
        kaug_scr[:, head(2 * g)] = k_ref[:, head(g)]
        kaug_scr[:, head(2 * g + 1)] = bias[:, LANES:]
        vaug_scr[:, head(2 * g)] = v_ref[:, head(g)]
        vaug_scr[:, head(2 * g + 1)] = ones

    causal = (lax.broadcasted_iota(jnp.int32, (tile, tile), 1) <= lax.broadcasted_iota(jnp.int32, (tile, tile), 0))
    for i in range(seq // tile):
        qrows = slice(i * tile, (i + 1) * tile)
        for g in range(grp):
            carry = _online_init(tile)
            for j in range(i + 1):
                krows = slice(j * tile, (j + 1) * tile)
                s2 = _dot_nt(qaug_scr[qrows, 2 * g * HEAD_DIM:(2 * g + 2) * HEAD_DIM],
                             kaug_scr[krows, 2 * g * HEAD_DIM:(2 * g + 2) * HEAD_DIM]) * c2
                if j == i:
                    s2 = jnp.where(causal, s2, NEG_INF)
                carry = _online_step(carry, s2, vaug_scr[krows, 2 * g * HEAD_DIM:(2 * g + 2) * HEAD_DIM])
            o_ref[qrows, head(g)] = _online_finish(carry).astype(o_ref.dtype)


def _fox_attention(zq, cumc, batch, seq):
    t = zq.shape[0]
    grp = FOX_GROUP
    gw = grp * HEAD_DIM
    return pl.pallas_call(
        _fox_kernel,
        grid=(batch, FOX_HEADS // grp),
        in_specs=[
            pl.BlockSpec((seq, gw), lambda b, h: (b, ZQ_FQ // grp + h)),
            pl.BlockSpec((seq, gw), lambda b, h: (b, ZQ_FK // grp + h)),
            pl.BlockSpec((seq, gw), lambda b, h: (b, ZQ_FV // grp + h)),
            pl.BlockSpec((seq, LANES), lambda b, h: (b, 0)),
        ],
        out_specs=pl.BlockSpec((seq, gw), lambda b, h: (b, h)),
        out_shape=jax.ShapeDtypeStruct((t, FOX_HEADS * HEAD_DIM), BF16),
        scratch_shapes=[pltpu.VMEM((seq, 2 * gw), BF16)] * 3,
        compiler_params=_cparams(("arbitrary", "arbitrary")),
        name="fox_attention",
    )(zq, zq, zq, cumc)


def _compress_to(src_ref, hk, pos_ref, w1_ref, w2_ref, flat_scr, dst_scr):
    half = CMP_BLOCK // 2
    n_rows = dst_scr.shape[0]
    cols = slice(hk * HEAD_DIM, (hk + 1) * HEAD_DIM)
    for jp in range(half):
        rows = src_ref[pl.ds(jp, n_rows, stride=CMP_STRIDE), :]
        flat_scr[:, jp * HEAD_DIM:(jp + 1) * HEAD_DIM] = (rows + pos_ref[jp:jp + 1, :]).astype(BF16)
        nxt = pltpu.roll(rows, n_rows - 1, axis=0)
        flat_scr[:, (half + jp) * HEAD_DIM:(half + jp + 1) * HEAD_DIM] = (
            nxt + pos_ref[half + jp:half + jp + 1, :]).astype(BF16)
    pre = _dot(flat_scr[...], w1_ref[...])
    dst_scr[:, cols] = _dot(jax.nn.gelu(pre).astype(BF16), w2_ref[...]).astype(BF16)


def _nsa_kernel(q_ref, ks_ref, kw_ref, vs_ref, vw_ref, *rest):
    nkv = NSA_KV_PER_STEP
    zs_ref, ocmp_ref, nots_ref, emask_ref, o_ref, ksaug_scr = rest
    hk0 = pl.program_id(1) * nkv
    i = pl.program_id(2)
    tq, tk = NSA_TQ, NSA_TK
    q0 = i * tq
    head = lambda h: slice(h * HEAD_DIM, (h + 1) * HEAD_DIM)

    @pl.when(i == 0)
    def _():
        for hk in range(nkv):
            ksaug_scr[:, head(2 * hk)] = ks_ref[:, head(hk)]
            ksaug_scr[:, head(2 * hk + 1)] = emask_ref[...]

    tile = functools.partial(_nsa_tile, q0=q0, hk0=hk0, q_ref=q_ref, kw_ref=kw_ref, vs_ref=vs_ref, vw_ref=vw_ref,
                             zs_ref=zs_ref, ocmp_ref=ocmp_ref, nots_ref=nots_ref, o_ref=o_ref, ksaug_scr=ksaug_scr)
    for n_full in range(ks_ref.shape[0] // tk):
        pl.when(q0 // tk == n_full)(functools.partial(tile, n_full))


def _nsa_tile(n_full, *, q0, hk0, q_ref, kw_ref, vs_ref, vw_ref, zs_ref, ocmp_ref, nots_ref, o_ref, ksaug_scr):
    grp, nkv = NSA_GROUP, NSA_KV_PER_STEP
    head = lambda h: slice(h * HEAD_DIM, (h + 1) * HEAD_DIM)

    def stack4(tile):
        return jnp.concatenate([tile] * grp, axis=0)

    gl = jax.nn.sigmoid(zs_ref[...])
    pre = []
    for hk in range(nkv):
        q4 = jnp.concatenate([q_ref[:, head(hk * grp + g)] for g in range(grp)], axis=0)
        o_cmp = jnp.concatenate([ocmp_ref[:, head(hk * grp + g)] for g in range(grp)], axis=0)
        pre.append((q4, jnp.concatenate([q4, stack4(nots_ref[:, head(hk)])], axis=1), o_cmp))
    _nsa_sweeps(n_full, pre, gl, q0=q0, hk0=hk0, kw_ref=kw_ref, vs_ref=vs_ref, vw_ref=vw_ref, o_ref=o_ref,
                ksaug_scr=ksaug_scr)


def _nsa_select_kernel(q_ref, *rest):
    nkv = NSA_KV_HEADS
    kcf_refs, vcf_refs = rest[:nkv], rest[nkv:2 * nkv]
    (posk_ref, posv_ref, w1k_ref, w1v_ref, w2k_ref, w2v_ref,
     ocmp_ref, nots_ref, kcmp_scr, vcmp_scr, flat_scr, tr_scr) = rest[2 * nkv:]
    i = pl.program_id(1)
    tq, grp = NSA_SEL_TQ, NSA_GROUP
    n_selb = SEL_BLOCK // CMP_STRIDE
    n_sel = LANES // n_selb
    c2 = LOG2E / math.sqrt(HEAD_DIM)
    q0 = i * tq
    head = lambda h: slice(h * HEAD_DIM, (h + 1) * HEAD_DIM)

    @pl.when(i == 0)
    def _():
        for hk in range(nkv):
            _compress_to(kcf_refs[hk], hk, posk_ref, w1k_ref, w2k_ref, flat_scr, kcmp_scr)
            _compress_to(vcf_refs[hk], hk, posv_ref, w1v_ref, w2v_ref, flat_scr, vcmp_scr)

    rloc = lax.broadcasted_iota(jnp.int32, (tq, LANES), 0)
    lane = lax.broadcasted_iota(jnp.int32, (tq, LANES), 1)
    cvalid = jnp.where((lane * CMP_STRIDE + (CMP_BLOCK - 1)) <= q0 + rloc, 1.0, 0.0)
    cvalid4 = jnp.concatenate([cvalid] * grp, axis=0) > 0.5
    blk = lax.broadcasted_iota(jnp.int32, (n_sel, tq), 0)
    cur = lax.shift_right_logical(q0 + lax.broadcasted_iota(jnp.int32, (n_sel, tq), 1), int(math.log2(SEL_BLOCK)))
    forced = jnp.where(blk == 0, 1.0, jnp.where(blk == cur, 1.0, jnp.where(blk == cur - 1, 1.0, 0.0)))

    for hk in range(nkv):
        q4 = jnp.concatenate([q_ref[:, head(hk * grp + g)] for g in range(grp)], axis=0)
        sc = jnp.where(cvalid4, _dot_nt(q4, kcmp_scr[:, head(hk)]) * c2, NEG_INF)
        pc = jnp.where(cvalid4, jnp.exp2(sc - jnp.max(sc, axis=1, keepdims=True)), 0.0)
        pc = pc / jnp.maximum(jnp.sum(pc, axis=1, keepdims=True), TINY)
        o_cmp = _dot(pc.astype(BF16), vcmp_scr[:, head(hk)])
        psum = (pc[0:tq] + pc[tq:2 * tq]) + (pc[2 * tq:3 * tq] + pc[3 * tq:4 * tq])
        nsub = tq // LANES
        for u in range(nsub):
            tr_scr[hk * nsub + u] = psum[u * LANES:(u + 1) * LANES, :].T
        pooled = lambda t: ((tr_scr[t, pl.ds(0, n_sel, stride=n_selb), :] + tr_scr[t, pl.ds(1, n_sel, stride=n_selb), :])
                            + (tr_scr[t, pl.ds(2, n_sel, stride=n_selb), :] + tr_scr[t, pl.ds(3, n_sel, stride=n_selb), :]))
        imp = jnp.concatenate([pooled(hk * nsub + u) for u in range(nsub)], axis=1)
        score = jnp.where(blk <= cur, imp + forced * FORCE_BONUS, NEG_INF)
        rank = jnp.zeros((n_sel, tq), F32)
        for kk in range(n_sel):
            row = score[kk:kk + 1, :]
            earlier = jnp.where(blk > kk, 1.0, 0.0)
            rank = rank + jnp.where(row > score, 1.0, jnp.where(row == score, earlier, 0.0))
        dropped = jnp.where(rank < float(SEL_TOPK), 0.0, 1.0)
        pieces = []
        for u in range(nsub):
            tr_scr[hk * nsub + u, 0:n_sel, :] = dropped[:, u * LANES:(u + 1) * LANES]
            tr_scr[hk * nsub + u, n_sel:, :] = jnp.zeros((LANES - n_sel, LANES), F32)
            pieces.append(tr_scr[hk * nsub + u].T)
        nots_ref[:, head(hk)] = jnp.concatenate(pieces, axis=0).astype(BF16)
        for g in range(grp):
            ocmp_ref[:, head(hk * grp + g)] = o_cmp[g * tq:(g + 1) * tq, :]


def _nsa_sweeps(n_full, pre, gl, *, q0, hk0, kw_ref, vs_ref, vw_ref, o_ref, ksaug_scr):
    tq, tk, grp, nkv = NSA_TQ, NSA_TK, NSA_GROUP, NSA_KV_PER_STEP
    rows4 = grp * tq
    c2 = LOG2E / math.sqrt(HEAD_DIM)
    head = lambda h: slice(h * HEAD_DIM, (h + 1) * HEAD_DIM)

    def stack4(tile):
        return jnp.concatenate([tile] * grp, axis=0)

    def sel_step(j, carries, bias4):
        rows = slice(j * tk, (j + 1) * tk)
        out = []
        for hk in range(nkv):
            s2 = _dot_nt(pre[hk][1], ksaug_scr[rows, 2 * hk * HEAD_DIM:(2 * hk + 2) * HEAD_DIM]) * c2
            if bias4 is not None:
                s2 = s2 + bias4
            out.append(_online_step(carries[hk], s2, jnp.concatenate([vs_ref[rows, head(hk)], ones_tk], axis=1)))
        return tuple(out)

    ones_tk = _ones_lane0(tk)
    carries = tuple(_online_init(rows4) for _ in range(nkv))
    for j in range(n_full):
        carries = sel_step(j, carries, None)
    rk = lax.broadcasted_iota(jnp.int32, (tq, tk), 0)
    ck = lax.broadcasted_iota(jnp.int32, (tq, tk), 1)
    causal4 = stack4(jnp.where(n_full * tk + ck <= q0 + rk, 0.0, NEG_INF))
    carries = sel_step(n_full, carries, causal4)

    w0 = pl.multiple_of(jnp.maximum(q0 - WINDOW, 0), tq)
    rw = lax.broadcasted_iota(jnp.int32, (tq, WIN_KEYS), 0)
    cw = lax.broadcasted_iota(jnp.int32, (tq, WIN_KEYS), 1)
    dist = (q0 - w0) + rw - cw
    wbias4 = stack4(jnp.where(jnp.abs(2 * dist - (WINDOW - 1)) <= (WINDOW - 1), 0.0, NEG_INF))
    wrows = pl.ds(w0, WIN_KEYS)

    for hk in range(nkv):
        q4, _, o_cmp = pre[hk]
        o_sel = _online_finish(carries[hk])
        s2 = _dot_nt(q4, kw_ref[wrows, head(hk)]) * c2 + wbias4
        vw_ones = jnp.concatenate([vw_ref[wrows, head(hk)], _ones_lane0(WIN_KEYS)], axis=1)
        o_win = _online_finish(_online_step(_online_init(rows4), s2, vw_ones))
        gate = lambda c: jnp.concatenate(
            [_lane_pick(gl, ZS_GATE0 + 3 * ((hk0 + hk) * grp + g) + c) for g in range(grp)], axis=0)
        o4 = gate(0) * o_cmp + gate(1) * o_sel + gate(2) * o_win
        for g in range(grp):
            o_ref[:, head(hk * grp + g)] = o4[g * tq:(g + 1) * tq, :].astype(o_ref.dtype)


def _nsa_attention(zq, zc, zs, posk, posv, w1k, w1v, w2k, w2v, emask, batch, seq):
    t = zq.shape[0]
    all_q = NSA_HEADS * HEAD_DIM
    all_kv = NSA_KV_HEADS * HEAD_DIM

    ts = NSA_SEL_TQ
    ns = seq // ts
    const2 = lambda shape: pl.BlockSpec(shape, lambda b, i: (0,) * len(shape))
    o_cmp, notsel = pl.pallas_call(
        _nsa_select_kernel,
        grid=(batch, ns),
        in_specs=[
            pl.BlockSpec((ts, all_q), lambda b, i: (b * ns + i, ZQ_NQ * HEAD_DIM // all_q)),
            *[pl.BlockSpec((seq, HEAD_DIM), functools.partial(lambda b, i, h: (b, h), h=h)) for h in range(ZC_HEADS)],
            const2((CMP_BLOCK, HEAD_DIM)), const2((CMP_BLOCK, HEAD_DIM)),
            const2((CMP_BLOCK * HEAD_DIM, HEAD_DIM)), const2((CMP_BLOCK * HEAD_DIM, HEAD_DIM)),
            const2((HEAD_DIM, HEAD_DIM)), const2((HEAD_DIM, HEAD_DIM)),
        ],
        out_specs=[
            pl.BlockSpec((ts, all_q), lambda b, i: (b * ns + i, 0)),
            pl.BlockSpec((ts, all_kv), lambda b, i: (b * ns + i, 0)),
        ],
        out_shape=[jax.ShapeDtypeStruct((t, all_q), F32), jax.ShapeDtypeStruct((t, all_kv), BF16)],
        scratch_shapes=[
            pltpu.VMEM((LANES, all_kv), BF16),
            pltpu.VMEM((LANES, all_kv), BF16),
            pltpu.VMEM((LANES, CMP_BLOCK * HEAD_DIM), BF16),
            pltpu.VMEM((NSA_KV_HEADS * (ts // LANES), LANES, LANES), F32),
        ],
        compiler_params=_cparams(("arbitrary", "arbitrary")),
        name="nsa_select",
    )(zq, zc, zc, zc, zc, posk, posv, w1k, w1v, w2k, w2v)

    tq, nkv = NSA_TQ, NSA_KV_PER_STEP
    nq = seq // tq
    qw = nkv * NSA_GROUP * HEAD_DIM
    kvw = nkv * HEAD_DIM
    kv = lambda col: pl.BlockSpec((seq, kvw), lambda b, h, i: (b, col // nkv + h))
    return pl.pallas_call(
        _nsa_kernel,
        grid=(batch, NSA_KV_HEADS // nkv, nq),
        in_specs=[
            pl.BlockSpec((tq, qw), lambda b, h, i: (b * nq + i, ZQ_NQ * HEAD_DIM // qw + h)),
            kv(ZQ_KS), kv(ZQ_KW), kv(ZQ_VS), kv(ZQ_VW),
            pl.BlockSpec((tq, LANES), lambda b, h, i: (b * nq + i, 0)),
            pl.BlockSpec((tq, qw), lambda b, h, i: (b * nq + i, h)),
            pl.BlockSpec((tq, kvw), lambda b, h, i: (b * nq + i, h)),
            pl.BlockSpec((seq, LANES), lambda b, h, i: (0, 0)),
        ],
        out_specs=pl.BlockSpec((tq, qw), lambda b, h, i: (b * nq + i, h)),
        out_shape=jax.ShapeDtypeStruct((t, all_q), BF16),
        scratch_shapes=[pltpu.VMEM((seq, 2 * kvw), BF16)],
        compiler_params=_cparams(("arbitrary", "arbitrary", "arbitrary")),
        name="nsa_attention",
    )(zq, zq, zq, zq, zq, zs, o_cmp, notsel, emask)


def _out_proj_kernel(of_ref, on_ref, wf_ref, wn_ref, x_ref, mod_ref, o_ref):
    n = o_ref.shape[1]
    for c0 in range(0, n, COL_CHUNK):
        cols = slice(c0, c0 + COL_CHUNK)
        acc = _dot(of_ref[...], wf_ref[:, cols]) + _dot(on_ref[...], wn_ref[:, cols])
        o_ref[:, cols] = x_ref[:, cols] + mod_ref[0, 2:3, cols] * acc


def _out_proj(o_fox, o_nsa, w_fox, w_nsa, x2d, mod, seq):
    t, d = x2d.shape
    tm = ROW_TILE
    tpb = seq // tm
    kf, kn = o_fox.shape[1], o_nsa.shape[1]
    return pl.pallas_call(
        _out_proj_kernel,
        grid=(t // tm,),
        in_specs=[
            pl.BlockSpec((tm, kf), lambda i: (i, 0)),
            pl.BlockSpec((tm, kn), lambda i: (i, 0)),
            pl.BlockSpec((kf, d), lambda i: (0, 0), pipeline_mode=pl.Buffered(1)),
            pl.BlockSpec((kn, d), lambda i: (0, 0), pipeline_mode=pl.Buffered(1)),
            pl.BlockSpec((tm, d), lambda i: (i, 0)),
            pl.BlockSpec((1, 6, d), lambda i: (i // tpb, 0, 0)),
        ],
        out_specs=pl.BlockSpec((tm, d), lambda i: (i, 0)),
        out_shape=jax.ShapeDtypeStruct((t, d), F32),
        compiler_params=_cparams(("arbitrary",)),
        name="out_proj",
    )(o_fox, o_nsa, w_fox, w_nsa, x2d, mod)


def _conv_rows(u, cw_ref, cb_ref):
    y = cb_ref[...] + cw_ref[0:1, :] * pltpu.roll(u, 2, axis=0)
    y = y + cw_ref[1:2, :] * pltpu.roll(u, 1, axis=0)
    return y + cw_ref[2:3, :] * u


def _ffn_up_kernel(x_ref, xh_ref, mod_ref, g_ref, wg_ref, wv_ref, cwg_ref, cwv_ref, cbg_ref, cbv_ref,
                   a_ref, h_scr, *, tiles_per_batch):
    i = pl.program_id(0)
    c = pl.program_id(1)
    tm = x_ref.shape[0]

    @pl.when(c == 0)
    def _():
        gain, shift = g_ref[...] * (1.0 + mod_ref[0, 4:5, :]), mod_ref[0, 3:4, :]
        halo = _norm_mod_rows(xh_ref[...], gain, shift)
        first = (i % tiles_per_batch) == 0
        h_scr[0:HALO, :] = jnp.where(first, 0.0, halo).astype(BF16)
        _norm_mod_to_scratch(x_ref, h_scr, HALO, tm, gain, shift)

    h = h_scr[...]
    yg = _conv_rows(_dot(h, wg_ref[0]), cwg_ref, cbg_ref)[HALO:, :]
    yv = _conv_rows(_dot(h, wv_ref[0]), cwv_ref, cbv_ref)[HALO:, :]
    a_ref[...] = ((yg * jax.nn.sigmoid(yg)) * yv).astype(a_ref.dtype)


def _ffn_down_kernel(a_ref, w_ref, x_ref, mod_ref, fg_ref, o_ref, *, final_norm):
    n = o_ref.shape[1]
    sumsq = jnp.zeros((o_ref.shape[0], 1), F32)
    for c0 in range(0, n, COL_CHUNK):
        cols = slice(c0, c0 + COL_CHUNK)
        y = x_ref[:, cols] + mod_ref[0, 5:6, cols] * _dot(a_ref[...], w_ref[0, :, cols])
        o_ref[:, cols] = y
        if final_norm:
            sumsq = sumsq + jnp.sum(y * y, axis=1, keepdims=True)
    if final_norm:
        r = lax.rsqrt(sumsq * (1.0 / n) + EPS)
        for c0 in range(0, n, COL_CHUNK):
            cols = slice(c0, c0 + COL_CHUNK)
            o_ref[:, cols] = (o_ref[:, cols] * r) * fg_ref[:, cols]


def _ffn(x2d, mod, g, w_up, conv_w, conv_b, w_down, final_g, seq, layer, final_norm):
    t, d = x2d.shape
    dff = w_down.shape[1]
    tm, tf = FFN_TILE, FF_CHUNK
    tpb = seq // tm
    nc = dff // tf
    hb = tm // HALO
    act = pl.pallas_call(
        functools.partial(_ffn_up_kernel, tiles_per_batch=tpb),
        grid=(t // tm, nc),
        in_specs=[
            pl.BlockSpec((tm, d), lambda i, c: (i, 0)),
            pl.BlockSpec((HALO, d), lambda i, c: (jnp.maximum(i * hb - 1, 0), 0)),
            pl.BlockSpec((1, 6, d), lambda i, c: (i // tpb, 0, 0)),
            pl.BlockSpec((1, d), lambda i, c: (0, 0)),
            pl.BlockSpec((1, d, tf), lambda i, c: (layer, 0, c)),
            pl.BlockSpec((1, d, tf), lambda i, c: (layer, 0, nc + c)),
            pl.BlockSpec((CONV_WIDTH, tf), lambda i, c: (0, c)),
            pl.BlockSpec((CONV_WIDTH, tf), lambda i, c: (0, nc + c)),
            pl.BlockSpec((1, tf), lambda i, c: (0, c)),
            pl.BlockSpec((1, tf), lambda i, c: (0, nc + c)),
        ],
        out_specs=pl.BlockSpec((tm, tf), lambda i, c: (i, c)),
        out_shape=jax.ShapeDtypeStruct((t, dff), BF16),
        scratch_shapes=[pltpu.VMEM((HALO + tm, d), BF16)],
        compiler_params=_cparams(("arbitrary", "arbitrary")),
        name="ffn_up",
    )(x2d, x2d, mod, g, w_up, w_up, conv_w, conv_w, conv_b, conv_b)

    tr = ROW_TILE
    rpb = seq // tr
    return pl.pallas_call(
        functools.partial(_ffn_down_kernel, final_norm=final_norm),
        grid=(t // tr,),
        in_specs=[
            pl.BlockSpec((tr, dff), lambda i: (i, 0)),
            pl.BlockSpec((1, dff, d), lambda i: (layer, 0, 0), pipeline_mode=pl.Buffered(1)),
            pl.BlockSpec((tr, d), lambda i: (i, 0)),
            pl.BlockSpec((1, 6, d), lambda i: (i // rpb, 0, 0)),
            pl.BlockSpec((1, d), lambda i: (0, 0)),
        ],
        out_specs=pl.BlockSpec((tr, d), lambda i: (i, 0)),
        out_shape=jax.ShapeDtypeStruct((t, d), F32),
        compiler_params=_cparams(("arbitrary",)),
        name="ffn_down",
    )(act, w_down, x2d, mod, final_g)


def _rope_tables(seq):
    inv = ROPE_THETA ** (-jnp.arange(0, HEAD_DIM, 2, dtype=F32) / HEAD_DIM)
    ang = jnp.arange(seq, dtype=F32)[:, None] * inv[None, :]
    cos, sin = jnp.cos(ang), jnp.sin(ang)
    return jnp.concatenate([cos, cos], axis=-1), jnp.concatenate([-sin, sin], axis=-1)


def _select_mask_table(seq):
    key_blk = (np.arange(seq) // SEL_BLOCK).reshape(seq, 1)
    return jnp.asarray(np.where(np.arange(LANES).reshape(1, LANES) == key_blk, NEG_INF, 0.0), dtype=BF16)


def _split_w_in(w):
    w = w.astype(BF16)
    fox = 3 * FOX_HEADS * HEAD_DIM
    nsa0 = fox + FOX_HEADS
    nsa1 = w.shape[-1] - 3 * NSA_HEADS
    pad = jnp.zeros((*w.shape[:-1], LANES - FOX_HEADS - 3 * NSA_HEADS), w.dtype)
    return jnp.concatenate([w[..., :fox], w[..., nsa0:nsa1], w[..., fox:nsa0], w[..., nsa1:], pad], axis=-1)


def kernel(x, c, attn_norm_g, ffn_norm_g, w_ada, b_ada, w_in, b_fgate, cmp_pos_k, cmp_pos_v,
           w_cmp1_k, w_cmp2_k, w_cmp1_v, w_cmp2_v, w_out, w_up, conv_w, conv_b, w_down, final_norm_g):
    batch, seq, d = x.shape
    depth = w_ada.shape[0]
    assert seq % ROW_TILE == 0 and seq % FFN_TILE == 0 and seq % FOX_TQ == 0 and seq % NSA_TK == 0 and seq // SEL_BLOCK == LANES // 4
    assert seq >= WIN_KEYS and w_down.shape[1] % FF_CHUNK == 0 and d % COL_CHUNK == 0

    cosf, sinf = _rope_tables(seq)
    emask = _select_mask_table(seq)
    tri = jnp.asarray(np.tril(np.ones((LANES, LANES))), dtype=BF16)
    mod_all = _ada_mod(c, w_ada, b_ada)

    x2d = x.reshape(batch * seq, d)
    n_fox = FOX_HEADS * HEAD_DIM
    w_up_b, w_down_b = w_up.astype(BF16), w_down.astype(BF16)
    w_in_b = _split_w_in(w_in)
    for l in range(depth):
        mod = mod_all[l].reshape(batch, 6, d)
        zq, zc, zs = _in_proj(x2d, mod, attn_norm_g[l].reshape(1, d), w_in_b, cosf, sinf, seq, l)
        bias_row = jnp.zeros((1, LANES), F32).at[0, :FOX_HEADS].set(b_fgate[l])
        cumc = _fgate(zs, bias_row, tri, batch, seq)
        o_fox = _fox_attention(zq, cumc, batch, seq)
        o_nsa = _nsa_attention(zq, zc, zs, cmp_pos_k[l], cmp_pos_v[l],
                               w_cmp1_k[l].astype(BF16), w_cmp1_v[l].astype(BF16),
                               w_cmp2_k[l].astype(BF16), w_cmp2_v[l].astype(BF16), emask, batch, seq)
        w_o = w_out[l].astype(BF16)
        x2d = _out_proj(o_fox, o_nsa, w_o[:n_fox], w_o[n_fox:], x2d, mod, seq)
        x2d = _ffn(x2d, mod, ffn_norm_g[l].reshape(1, d), w_up_b, conv_w[l], conv_b[l].reshape(1, -1), w_down_b,
                   final_norm_g.reshape(1, d), seq, layer=l, final_norm=(l == depth - 1))
    return x2d.reshape(batch, seq, d)
```

```python
import functools
import math

import numpy as np
import jax
import jax.numpy as jnp
from jax import lax
from jax.experimental import pallas as pl
from jax.experimental.pallas import tpu as pltpu

HEAD_DIM = 128
FOX_HEADS = 8
NSA_HEADS = 8
NSA_KV_HEADS = 2
NSA_GROUP = NSA_HEADS // NSA_KV_HEADS
CMP_BLOCK = 32
CMP_STRIDE = 16
SEL_BLOCK = 64
SEL_TOPK = 16
WINDOW = 512
CONV_WIDTH = 3
ROPE_THETA = 10000.0
FORCE_BONUS = 1000.0
NEG_INF = -1e30
TINY = 1e-30
EPS = 1e-6
LOG2E = math.log2(math.e)

LANES = 128
SUBLANES = 8
VMEM_LIMIT_BYTES = 56 * 1024 * 1024

ROW_TILE = 512
FFN_TILE = 1024
COL_CHUNK = 512
NORM_SUBROWS = 16
NORM_UNROLL = 8
FOX_TQ = 512
FOX_GROUP = 2
NSA_TQ = 128
NSA_SEL_TQ = 1024
NSA_KV_PER_STEP = 2
NSA_TK = 512
WIN_KEYS = WINDOW + NSA_TQ
HALO = 2 * SUBLANES
FF_CHUNK = 512

BF16 = jnp.bfloat16
F32 = jnp.float32

ZQ_FQ, ZQ_FK, ZQ_FV, ZQ_NQ, ZQ_KC, ZQ_VC, ZQ_KS, ZQ_VS, ZQ_KW, ZQ_VW = 0, 8, 16, 24, 32, 34, 36, 38, 40, 42
ZQ_HEADS = 44
ZQ_ROPED = (*range(ZQ_NQ, ZQ_NQ + NSA_HEADS), *(first + u for first in (ZQ_KC, ZQ_KS, ZQ_KW)
                                                 for u in range(NSA_KV_HEADS)))
ZC_HEADS = 4
ZS_GATE0 = FOX_HEADS


def _cparams(sem):
    return pltpu.CompilerParams(dimension_semantics=sem, vmem_limit_bytes=VMEM_LIMIT_BYTES)


def _dot(a, b):
    return jnp.dot(a, b, preferred_element_type=F32)


def _dot_nt(a, b):
    return lax.dot_general(a, b, (((1,), (1,)), ((), ())), preferred_element_type=F32)


def _ada_kernel(c_ref, w_ref, b_ref, o_ref):
    c = c_ref[...]
    ca = (c * jax.nn.sigmoid(c)).astype(BF16)
    o_ref[0] = _dot(ca, w_ref[0].astype(BF16)) + b_ref[0]


def _ada_mod(c, w_ada, b_ada):
    depth, d, n = w_ada.shape
    b = c.shape[0]
    tn = 1024
    return pl.pallas_call(
        _ada_kernel,
        grid=(depth, n // tn),
        in_specs=[
            pl.BlockSpec((b, d), lambda l, j: (0, 0)),
            pl.BlockSpec((1, d, tn), lambda l, j: (l, 0, j)),
            pl.BlockSpec((1, 1, tn), lambda l, j: (l, 0, j)),
        ],
        out_specs=pl.BlockSpec((1, b, tn), lambda l, j: (l, 0, j)),
        out_shape=jax.ShapeDtypeStruct((depth, b, n), F32),
        compiler_params=_cparams(("arbitrary", "arbitrary")),
        name="ada_mod",
    )(c, w_ada, b_ada.reshape(depth, 1, n))


def _norm_mod_rows(x, gain, shift):
    ms = jnp.mean(x * x, axis=-1, keepdims=True)
    return (x * lax.rsqrt(ms + EPS)) * gain + shift


def _norm_mod_to_scratch(x_ref, h_scr, row0, n_rows, gain, shift):
    nr = NORM_SUBROWS
    def body(r, _):
        rows = pl.ds(pl.multiple_of(r * nr, nr), nr)
        dst = pl.ds(pl.multiple_of(row0 + r * nr, nr), nr)
        h_scr[dst, :] = _norm_mod_rows(x_ref[rows, :], gain, shift).astype(BF16)
        return 0
    lax.fori_loop(0, n_rows // nr, body, 0, unroll=NORM_UNROLL)


def _rope_tile(t, cosf, sinf):
    return t * cosf + pltpu.roll(t, HEAD_DIM // 2, axis=1) * sinf


def _in_proj_kernel(x_ref, mod_ref, g_ref, wfox_ref, wnsa_ref, cos_ref, sin_ref, zq_ref, zc_ref, zs_ref, h_scr):
    tm = x_ref.shape[0]
    _norm_mod_to_scratch(x_ref, h_scr, 0, tm, g_ref[...] * (1.0 + mod_ref[0, 1:2, :]), mod_ref[0, 0:1, :])
    heads_per_chunk = COL_CHUNK // HEAD_DIM
    for c0 in range(0, ZQ_HEADS, heads_per_chunk):
        w_ref, first = (wfox_ref, c0) if c0 < ZQ_NQ else (wnsa_ref, c0 - ZQ_NQ)
        acc = _dot(h_scr[...], w_ref[0, :, first * HEAD_DIM:(first + heads_per_chunk) * HEAD_DIM])
        for hh in range(heads_per_chunk):
            head = c0 + hh
            t = acc[:, hh * HEAD_DIM:(hh + 1) * HEAD_DIM]
            if head in ZQ_ROPED:
                t = _rope_tile(t, cos_ref[...], sin_ref[...])
            zq_ref[:, head * HEAD_DIM:(head + 1) * HEAD_DIM] = t.astype(BF16)
            if ZQ_KC <= head < ZQ_KC + ZC_HEADS:
                zc_ref[:, (head - ZQ_KC) * HEAD_DIM:(head - ZQ_KC + 1) * HEAD_DIM] = t
    base = (ZQ_HEADS - ZQ_NQ) * HEAD_DIM
    zs_ref[...] = _dot(h_scr[...], wnsa_ref[0, :, base:base + LANES])


def _in_proj(x2d, mod, g, w_fox, w_nsa, cosf, sinf, seq, layer):
    t, d = x2d.shape
    tm = ROW_TILE
    tpb = seq // tm
    n_fox = ZQ_NQ * HEAD_DIM
    return pl.pallas_call(
        _in_proj_kernel,
        grid=(t // tm,),
        in_specs=[
            pl.BlockSpec((tm, d), lambda i: (i, 0)),
            pl.BlockSpec((1, 6, d), lambda i: (i // tpb, 0, 0)),
            pl.BlockSpec((1, d), lambda i: (0, 0)),
            pl.BlockSpec((1, d, n_fox), lambda i: (layer, 0, 0), pipeline_mode=pl.Buffered(1)),
            pl.BlockSpec((1, d, w_nsa.shape[2]), lambda i: (layer, 0, 0), pipeline_mode=pl.Buffered(1)),
            pl.BlockSpec((tm, HEAD_DIM), lambda i: (i % tpb, 0)),
            pl.BlockSpec((tm, HEAD_DIM), lambda i: (i % tpb, 0)),
        ],
        out_specs=[
            pl.BlockSpec((tm, ZQ_HEADS * HEAD_DIM), lambda i: (i, 0)),
            pl.BlockSpec((tm, ZC_HEADS * HEAD_DIM), lambda i: (i, 0)),
            pl.BlockSpec((tm, LANES), lambda i: (i, 0)),
        ],
        out_shape=[
            jax.ShapeDtypeStruct((t, ZQ_HEADS * HEAD_DIM), BF16),
            jax.ShapeDtypeStruct((t, ZC_HEADS * HEAD_DIM), F32),
            jax.ShapeDtypeStruct((t, LANES), F32),
        ],
        scratch_shapes=[pltpu.VMEM((tm, d), BF16)],
        compiler_params=_cparams(("arbitrary",)),
        name="in_proj",
    )(x2d, mod, g, w_fox, w_nsa, cosf, sinf)


def _split3(v):
    hi = v.astype(BF16)
    r1 = v - hi.astype(F32)
    mid = r1.astype(BF16)
    lo = (r1 - mid.astype(F32)).astype(BF16)
    return hi, mid, lo


def _fgate_kernel(zs_ref, bias_ref, tri_ref, cumc_ref):
    seq = zs_ref.shape[0]
    tri = tri_ref[...]
    carry = jnp.zeros((1, LANES), F32)
    for c in range(seq // LANES):
        zf = zs_ref[c * LANES:(c + 1) * LANES, :] + bias_ref[...]
        lf = jnp.minimum(zf, 0.0) - jnp.log1p(jnp.exp(-jnp.abs(zf)))
        hi, mid, lo = _split3(lf)
        cs = (_dot(tri, lo) + _dot(tri, mid)) + _dot(tri, hi) + carry
        carry = cs[LANES - 1:LANES, :]
        cumc_ref[c * LANES:(c + 1) * LANES, :] = cs


def _fgate(zs, bias_row, tri, batch, seq):
    t = zs.shape[0]
    return pl.pallas_call(
        _fgate_kernel,
        grid=(batch,),
        in_specs=[
            pl.BlockSpec((seq, LANES), lambda b: (b, 0)),
            pl.BlockSpec((1, LANES), lambda b: (0, 0)),
            pl.BlockSpec((LANES, LANES), lambda b: (0, 0)),
        ],
        out_specs=pl.BlockSpec((seq, LANES), lambda b: (b, 0)),
        out_shape=jax.ShapeDtypeStruct((t, LANES), F32),
        compiler_params=_cparams(("arbitrary",)),
        name="fgate_cumsum",
    )(zs, bias_row, tri)


def _lane_pick(tile, lane_idx):
    lane = lax.broadcasted_iota(jnp.int32, tile.shape, 1)
    return jnp.sum(jnp.where(lane == lane_idx, tile, 0.0), axis=1, keepdims=True)


def _online_step(carry, s2, v_ones):
    m, acc = carry
    m_new = jnp.maximum(m, jnp.max(s2, axis=1, keepdims=True))
    p = jnp.exp2(s2 - m_new)
    acc = jnp.exp2(m - m_new) * acc + _dot(p.astype(BF16), v_ones)
    return m_new, acc


def _online_init(rows):
    return jnp.full((rows, 1), NEG_INF, F32), jnp.zeros((rows, 2 * HEAD_DIM), F32)


def _online_finish(carry):
    _, acc = carry
    return acc[:, :HEAD_DIM] / acc[:, HEAD_DIM:HEAD_DIM + 1]


def _ones_lane0(rows):
    return jnp.where(lax.broadcasted_iota(jnp.int32, (rows, LANES), 1) == 0, 1.0, 0.0).astype(BF16)


def _bias_lanes(pieces, head):
    npc = len(pieces)

    def place(n_in):
        row = lax.broadcasted_iota(jnp.int32, (n_in * LANES, 2 * LANES), 0)
        col = lax.broadcasted_iota(jnp.int32, (n_in * LANES, 2 * LANES), 1)
        return row, col
    hi_mid = jnp.concatenate(pieces[:2], axis=1)
    row, col = place(2)
    pc = jnp.where(row >= LANES, 1, 0)
    hit = row - pc * LANES == head
    w_a = jnp.where(hit, jnp.where(col == pc, 1.0, jnp.where(col == LANES + npc + pc, -1.0, 0.0)), 0.0).astype(BF16)
    row, col = place(1)
    w_b = jnp.where(row == head,
                    jnp.where(col == npc - 1, 1.0, jnp.where(col == LANES + 2 * npc - 1, -1.0, 0.0)), 0.0).astype(BF16)
    lane = lax.broadcasted_iota(jnp.int32, (1, 2 * LANES), 1)
    in_range = lambda lo, hi: jnp.where(lane >= lo, jnp.where(lane < hi, 1.0, 0.0), 0.0)
    ones = in_range(npc, 2 * npc) + in_range(LANES, LANES + npc)
    return (_dot(hi_mid, w_a) + _dot(pieces[2], w_b) + ones).astype(BF16)


def _fox_kernel(q_ref, k_ref, v_ref, cum_ref, o_ref, qaug_scr, kaug_scr, vaug_scr):
    hb = pl.program_id(1)
    tile, grp = FOX_TQ, FOX_GROUP
    seq = q_ref.shape[0]
    scale = 1.0 / math.sqrt(HEAD_DIM)
    c2 = scale * LOG2E
    head = lambda h: slice(h * HEAD_DIM, (h + 1) * HEAD_DIM)

    pieces = _split3(cum_ref[...] * (1.0 / scale))
    ones = _ones_lane0(seq)
    for g in range(grp):
        bias = _bias_lanes(pieces, hb * grp + g)
        qaug_scr[:, head(2 * g)] = q_ref[:, head(g)]
        qaug_scr[:, head(2 * g + 1)] = bias[:, :LANES]
        kaug_scr[:, head(2 * g)] = k_ref[:, head(g)]
        kaug_scr[:, head(2 * g + 1)] = bias[:, LANES:]
        vaug_scr[:, head(2 * g)] = v_ref[:, head(g)]
        vaug_scr[:, head(2 * g + 1)] = ones

    causal = (lax.broadcasted_iota(jnp.int32, (tile, tile), 1) <= lax.broadcasted_iota(jnp.int32, (tile, tile), 0))
    for i in range(seq // tile):
        qrows = slice(i * tile, (i + 1) * tile)
        for g in range(grp):
            carry = _online_init(tile)
            for j in range(i + 1):
                krows = slice(j * tile, (j + 1) * tile)
                s2 = _dot_nt(qaug_scr[qrows, 2 * g * HEAD_DIM:(2 * g + 2) * HEAD_DIM],
                             kaug_scr[krows, 2 * g * HEAD_DIM:(2 * g + 2) * HEAD_DIM]) * c2
                if j == i:
                    s2 = jnp.where(causal, s2, NEG_INF)
                carry = _online_step(carry, s2, vaug_scr[krows, 2 * g * HEAD_DIM:(2 * g + 2) * HEAD_DIM])
            o_ref[qrows, head(g)] = _online_finish(carry).astype(o_ref.dtype)


def _fox_attention(zq, cumc, batch, seq):
    t = zq.shape[0]
    grp = FOX_GROUP
    gw = grp * HEAD_DIM
    return pl.pallas_call(
        _fox_kernel,
        grid=(batch, FOX_HEADS // grp),
        in_specs=[
            pl.BlockSpec((seq, gw), lambda b, h: (b, ZQ_FQ // grp + h)),
            pl.BlockSpec((seq, gw), lambda b, h: (b, ZQ_FK // grp + h)),
            pl.BlockSpec((seq, gw), lambda b, h: (b, ZQ_FV // grp + h)),
            pl.BlockSpec((seq, LANES), lambda b, h: (b, 0)),
        ],
        out_specs=pl.BlockSpec((seq, gw), lambda b, h: (b, h)),
        out_shape=jax.ShapeDtypeStruct((t, FOX_HEADS * HEAD_DIM), BF16),
        scratch_shapes=[pltpu.VMEM((seq, 2 * gw), BF16)] * 3,
        compiler_params=_cparams(("arbitrary", "arbitrary")),
        name="fox_attention",
    )(zq, zq, zq, cumc)


def _compress_to(src_ref, hk, pos_ref, w1_ref, w2_ref, flat_scr, dst_scr):
    half = CMP_BLOCK // 2
    n_rows = dst_scr.shape[0]
    cols = slice(hk * HEAD_DIM, (hk + 1) * HEAD_DIM)
    for jp in range(half):
        rows = src_ref[pl.ds(jp, n_rows, stride=CMP_STRIDE), :]
        flat_scr[:, jp * HEAD_DIM:(jp + 1) * HEAD_DIM] = (rows + pos_ref[jp:jp + 1, :]).astype(BF16)
        nxt = pltpu.roll(rows, n_rows - 1, axis=0)
        flat_scr[:, (half + jp) * HEAD_DIM:(half + jp + 1) * HEAD_DIM] = (
            nxt + pos_ref[half + jp:half + jp + 1, :]).astype(BF16)
    pre = _dot(flat_scr[...], w1_ref[...])
    dst_scr[:, cols] = _dot(jax.nn.gelu(pre).astype(BF16), w2_ref[...]).astype(BF16)


def _nsa_kernel(q_ref, ks_ref, kw_ref, vs_ref, vw_ref, *rest):
    nkv = NSA_KV_PER_STEP
    zs_ref, ocmp_ref, nots_ref, emask_ref, o_ref, ksaug_scr = rest
    hk0 = pl.program_id(1) * nkv
    i = pl.program_id(2)
    tq, tk = NSA_TQ, NSA_TK
    q0 = i * tq
    head = lambda h: slice(h * HEAD_DIM, (h + 1) * HEAD_DIM)

    @pl.when(i == 0)
    def _():
        for hk in range(nkv):
            ksaug_scr[:, head(2 * hk)] = ks_ref[:, head(hk)]
            ksaug_scr[:, head(2 * hk + 1)] = emask_ref[...]

    tile = functools.partial(_nsa_tile, q0=q0, hk0=hk0, q_ref=q_ref, kw_ref=kw_ref, vs_ref=vs_ref, vw_ref=vw_ref,
                             zs_ref=zs_ref, ocmp_ref=ocmp_ref, nots_ref=nots_ref, o_ref=o_ref, ksaug_scr=ksaug_scr)
    for n_full in range(ks_ref.shape[0] // tk):
        pl.when(q0 // tk == n_full)(functools.partial(tile, n_full))


def _nsa_tile(n_full, *, q0, hk0, q_ref, kw_ref, vs_ref, vw_ref, zs_ref, ocmp_ref, nots_ref, o_ref, ksaug_scr):
    grp, nkv = NSA_GROUP, NSA_KV_PER_STEP
    head = lambda h: slice(h * HEAD_DIM, (h + 1) * HEAD_DIM)

    def stack4(tile):
        return jnp.concatenate([tile] * grp, axis=0)

    gl = jax.nn.sigmoid(zs_ref[...])
    pre = []
    for hk in range(nkv):
        q4 = jnp.concatenate([q_ref[:, head(hk * grp + g)] for g in range(grp)], axis=0)
        o_cmp = jnp.concatenate([ocmp_ref[:, head(hk * grp + g)] for g in range(grp)], axis=0)
        pre.append((q4, jnp.concatenate([q4, stack4(nots_ref[:, head(hk)])], axis=1), o_cmp))
    _nsa_sweeps(n_full, pre, gl, q0=q0, hk0=hk0, kw_ref=kw_ref, vs_ref=vs_ref, vw_ref=vw_ref, o_ref=o_ref,
                ksaug_scr=ksaug_scr)


def _nsa_select_kernel(q_ref, *rest):
    nkv = NSA_KV_HEADS
    kcf_refs, vcf_refs = rest[:nkv], rest[nkv:2 * nkv]
    (posk_ref, posv_ref, w1k_ref, w1v_ref, w2k_ref, w2v_ref,
     ocmp_ref, nots_ref, kcmp_scr, vcmp_scr, flat_scr, tr_scr) = rest[2 * nkv:]
    i = pl.program_id(1)
    tq, grp = NSA_SEL_TQ, NSA_GROUP
    n_selb = SEL_BLOCK // CMP_STRIDE
    n_sel = LANES // n_selb
    c2 = LOG2E / math.sqrt(HEAD_DIM)
    q0 = i * tq
    head = lambda h: slice(h * HEAD_DIM, (h + 1) * HEAD_DIM)

    @pl.when(i == 0)
    def _():
        for hk in range(nkv):
            _compress_to(kcf_refs[hk], hk, posk_ref, w1k_ref, w2k_ref, flat_scr, kcmp_scr)
            _compress_to(vcf_refs[hk], hk, posv_ref, w1v_ref, w2v_ref, flat_scr, vcmp_scr)

    rloc = lax.broadcasted_iota(jnp.int32, (tq, LANES), 0)
    lane = lax.broadcasted_iota(jnp.int32, (tq, LANES), 1)
    cvalid = jnp.where((lane * CMP_STRIDE + (CMP_BLOCK - 1)) <= q0 + rloc, 1.0, 0.0)
    cvalid4 = jnp.concatenate([cvalid] * grp, axis=0) > 0.5
    blk = lax.broadcasted_iota(jnp.int32, (n_sel, tq), 0)
    cur = lax.shift_right_logical(q0 + lax.broadcasted_iota(jnp.int32, (n_sel, tq), 1), int(math.log2(SEL_BLOCK)))
    forced = jnp.where(blk == 0, 1.0, jnp.where(blk == cur, 1.0, jnp.where(blk == cur - 1, 1.0, 0.0)))

    for hk in range(nkv):
        q4 = jnp.concatenate([q_ref[:, head(hk * grp + g)] for g in range(grp)], axis=0)
        sc = jnp.where(cvalid4, _dot_nt(q4, kcmp_scr[:, head(hk)]) * c2, NEG_INF)
        pc = jnp.where(cvalid4, jnp.exp2(sc - jnp.max(sc, axis=1, keepdims=True)), 0.0)
        pc = pc / jnp.maximum(jnp.sum(pc, axis=1, keepdims=True), TINY)
        o_cmp = _dot(pc.astype(BF16), vcmp_scr[:, head(hk)])
        psum = (pc[0:tq] + pc[tq:2 * tq]) + (pc[2 * tq:3 * tq] + pc[3 * tq:4 * tq])
        nsub = tq // LANES
        for u in range(nsub):
            tr_scr[hk * nsub + u] = psum[u * LANES:(u + 1) * LANES, :].T
        pooled = lambda t: ((tr_scr[t, pl.ds(0, n_sel, stride=n_selb), :] + tr_scr[t, pl.ds(1, n_sel, stride=n_selb), :])
                            + (tr_scr[t, pl.ds(2, n_sel, stride=n_selb), :] + tr_scr[t, pl.ds(3, n_sel, stride=n_selb), :]))
        imp = jnp.concatenate([pooled(hk * nsub + u) for u in range(nsub)], axis=1)
        score = jnp.where(blk <= cur, imp + forced * FORCE_BONUS, NEG_INF)
        rank = jnp.zeros((n_sel, tq), F32)
        for kk in range(n_sel):
            row = score[kk:kk + 1, :]
            earlier = jnp.where(blk > kk, 1.0, 0.0)
            rank = rank + jnp.where(row > score, 1.0, jnp.where(row == score, earlier, 0.0))
        dropped = jnp.where(rank < float(SEL_TOPK), 0.0, 1.0)
        pieces = []
        for u in range(nsub):
            tr_scr[hk * nsub + u, 0:n_sel, :] = dropped[:, u * LANES:(u + 1) * LANES]
            tr_scr[hk * nsub + u, n_sel:, :] = jnp.zeros((LANES - n_sel, LANES), F32)
            pieces.append(tr_scr[hk * nsub + u].T)
        nots_ref[:, head(hk)] = jnp.concatenate(pieces, axis=0).astype(BF16)
        for g in range(grp):
            ocmp_ref[:, head(hk * grp + g)] = o_cmp[g * tq:(g + 1) * tq, :]


def _nsa_sweeps(n_full, pre, gl, *, q0, hk0, kw_ref, vs_ref, vw_ref, o_ref, ksaug_scr):
    tq, tk, grp, nkv = NSA_TQ, NSA_TK, NSA_GROUP, NSA_KV_PER_STEP
    rows4 = grp * tq
    c2 = LOG2E / math.sqrt(HEAD_DIM)
    head = lambda h: slice(h * HEAD_DIM, (h + 1) * HEAD_DIM)

    def stack4(tile):
        return jnp.concatenate([tile] * grp, axis=0)

    def sel_step(j, carries, bias4):
        rows = slice(j * tk, (j + 1) * tk)
        out = []
        for hk in range(nkv):
            s2 = _dot_nt(pre[hk][1], ksaug_scr[rows, 2 * hk * HEAD_DIM:(2 * hk + 2) * HEAD_DIM]) * c2
            if bias4 is not None:
                s2 = s2 + bias4
            out.append(_online_step(carries[hk], s2, jnp.concatenate([vs_ref[rows, head(hk)], ones_tk], axis=1)))
        return tuple(out)

    ones_tk = _ones_lane0(tk)
    carries = tuple(_online_init(rows4) for _ in range(nkv))
    for j in range(n_full):
        carries = sel_step(j, carries, None)
    rk = lax.broadcasted_iota(jnp.int32, (tq, tk), 0)
    ck = lax.broadcasted_iota(jnp.int32, (tq, tk), 1)
    causal4 = stack4(jnp.where(n_full * tk + ck <= q0 + rk, 0.0, NEG_INF))
    carries = sel_step(n_full, carries, causal4)

    w0 = pl.multiple_of(jnp.maximum(q0 - WINDOW, 0), tq)
    rw = lax.broadcasted_iota(jnp.int32, (tq, WIN_KEYS), 0)
    cw = lax.broadcasted_iota(jnp.int32, (tq, WIN_KEYS), 1)
    dist = (q0 - w0) + rw - cw
    wbias4 = stack4(jnp.where(jnp.abs(2 * dist - (WINDOW - 1)) <= (WINDOW - 1), 0.0, NEG_INF))
    wrows = pl.ds(w0, WIN_KEYS)

    for hk in range(nkv):
        q4, _, o_cmp = pre[hk]
        o_sel = _online_finish(carries[hk])
        s2 = _dot_nt(q4, kw_ref[wrows, head(hk)]) * c2 + wbias4
        vw_ones = jnp.concatenate([vw_ref[wrows, head(hk)], _ones_lane0(WIN_KEYS)], axis=1)
        o_win = _online_finish(_online_step(_online_init(rows4), s2, vw_ones))
        gate = lambda c: jnp.concatenate(
            [_lane_pick(gl, ZS_GATE0 + 3 * ((hk0 + hk) * grp + g) + c) for g in range(grp)], axis=0)
        o4 = gate(0) * o_cmp + gate(1) * o_sel + gate(2) * o_win
        for g in range(grp):
            o_ref[:, head(hk * grp + g)] = o4[g * tq:(g + 1) * tq, :].astype(o_ref.dtype)


def _nsa_attention(zq, zc, zs, posk, posv, w1k, w1v, w2k, w2v, emask, batch, seq):
    t = zq.shape[0]
    all_q = NSA_HEADS * HEAD_DIM
    all_kv = NSA_KV_HEADS * HEAD_DIM

    ts = NSA_SEL_TQ
    ns = seq // ts
    const2 = lambda shape: pl.BlockSpec(shape, lambda b, i: (0,) * len(shape))
    o_cmp, notsel = pl.pallas_call(
        _nsa_select_kernel,
        grid=(batch, ns),
        in_specs=[
            pl.BlockSpec((ts, all_q), lambda b, i: (b * ns + i, ZQ_NQ * HEAD_DIM // all_q)),
            *[pl.BlockSpec((seq, HEAD_DIM), functools.partial(lambda b, i, h: (b, h), h=h)) for h in range(ZC_HEADS)],
            const2((CMP_BLOCK, HEAD_DIM)), const2((CMP_BLOCK, HEAD_DIM)),
            const2((CMP_BLOCK * HEAD_DIM, HEAD_DIM)), const2((CMP_BLOCK * HEAD_DIM, HEAD_DIM)),
            const2((HEAD_DIM, HEAD_DIM)), const2((HEAD_DIM, HEAD_DIM)),
        ],
        out_specs=[
            pl.BlockSpec((ts, all_q), lambda b, i: (b * ns + i, 0)),
            pl.BlockSpec((ts, all_kv), lambda b, i: (b * ns + i, 0)),
        ],
        out_shape=[jax.ShapeDtypeStruct((t, all_q), F32), jax.ShapeDtypeStruct((t, all_kv), BF16)],
        scratch_shapes=[
            pltpu.VMEM((LANES, all_kv), BF16),
            pltpu.VMEM((LANES, all_kv), BF16),
            pltpu.VMEM((LANES, CMP_BLOCK * HEAD_DIM), BF16),
            pltpu.VMEM((NSA_KV_HEADS * (ts // LANES), LANES, LANES), F32),
        ],
        compiler_params=_cparams(("arbitrary", "arbitrary")),
        name="nsa_select",
    )(zq, zc, zc, zc, zc, posk, posv, w1k, w1v, w2k, w2v)

    tq, nkv = NSA_TQ, NSA_KV_PER_STEP
    nq = seq // tq
    qw = nkv * NSA_GROUP * HEAD_DIM
    kvw = nkv * HEAD_DIM
    kv = lambda col: pl.BlockSpec((seq, kvw), lambda b, h, i: (b, col // nkv + h))
    return pl.pallas_call(
        _nsa_kernel,
        grid=(batch, NSA_KV_HEADS // nkv, nq),
        in_specs=[
            pl.BlockSpec((tq, qw), lambda b, h, i: (b * nq + i, ZQ_NQ * HEAD_DIM // qw + h)),
            kv(ZQ_KS), kv(ZQ_KW), kv(ZQ_VS), kv(ZQ_VW),
            pl.BlockSpec((tq, LANES), lambda b, h, i: (b * nq + i, 0)),
            pl.BlockSpec((tq, qw), lambda b, h, i: (b * nq + i, h)),
            pl.BlockSpec((tq, kvw), lambda b, h, i: (b * nq + i, h)),
            pl.BlockSpec((seq, LANES), lambda b, h, i: (0, 0)),
        ],
        out_specs=pl.BlockSpec((tq, qw), lambda b, h, i: (b * nq + i, h)),
        out_shape=jax.ShapeDtypeStruct((t, all_q), BF16),
        scratch_shapes=[pltpu.VMEM((seq, 2 * kvw), BF16)],
        compiler_params=_cparams(("arbitrary", "arbitrary", "arbitrary")),
        name="nsa_attention",
    )(zq, zq, zq, zq, zq, zs, o_cmp, notsel, emask)


def _out_proj_kernel(of_ref, on_ref, wf_ref, wn_ref, x_ref, mod_ref, o_ref):
    n = o_ref.shape[1]
    for c0 in range(0, n, COL_CHUNK):
        cols = slice(c0, c0 + COL_CHUNK)
        acc = _dot(of_ref[...], wf_ref[:, cols]) + _dot(on_ref[...], wn_ref[:, cols])
        o_ref[:, cols] = x_ref[:, cols] + mod_ref[0, 2:3, cols] * acc


def _out_proj(o_fox, o_nsa, w_fox, w_nsa, x2d, mod, seq):
    t, d = x2d.shape
    tm = ROW_TILE
    tpb = seq // tm
    kf, kn = o_fox.shape[1], o_nsa.shape[1]
    return pl.pallas_call(
        _out_proj_kernel,
        grid=(t // tm,),
        in_specs=[
            pl.BlockSpec((tm, kf), lambda i: (i, 0)),
            pl.BlockSpec((tm, kn), lambda i: (i, 0)),
            pl.BlockSpec((kf, d), lambda i: (0, 0), pipeline_mode=pl.Buffered(1)),
            pl.BlockSpec((kn, d), lambda i: (0, 0), pipeline_mode=pl.Buffered(1)),
            pl.BlockSpec((tm, d), lambda i: (i, 0)),
            pl.BlockSpec((1, 6, d), lambda i: (i // tpb, 0, 0)),
        ],
        out_specs=pl.BlockSpec((tm, d), lambda i: (i, 0)),
        out_shape=jax.ShapeDtypeStruct((t, d), F32),
        compiler_params=_cparams(("arbitrary",)),
        name="out_proj",
    )(o_fox, o_nsa, w_fox, w_nsa, x2d, mod)


def _conv_rows(u, cw_ref, cb_ref):
    y = cb_ref[...] + cw_ref[0:1, :] * pltpu.roll(u, 2, axis=0)
    y = y + cw_ref[1:2, :] * pltpu.roll(u, 1, axis=0)
    return y + cw_ref[2:3, :] * u


def _ffn_up_kernel(x_ref, xh_ref, mod_ref, g_ref, wg_ref, wv_ref, cwg_ref, cwv_ref, cbg_ref, cbv_ref,
                   a_ref, h_scr, *, tiles_per_batch):
    i = pl.program_id(0)
    c = pl.program_id(1)
    tm = x_ref.shape[0]

    @pl.when(c == 0)
    def _():
        gain, shift = g_ref[...] * (1.0 + mod_ref[0, 4:5, :]), mod_ref[0, 3:4, :]
        halo = _norm_mod_rows(xh_ref[...], gain, shift)
        first = (i % tiles_per_batch) == 0
        h_scr[0:HALO, :] = jnp.where(first, 0.0, halo).astype(BF16)
        _norm_mod_to_scratch(x_ref, h_scr, HALO, tm, gain, shift)

    h = h_scr[...]
    yg = _conv_rows(_dot(h, wg_ref[0]), cwg_ref, cbg_ref)[HALO:, :]
    yv = _conv_rows(_dot(h, wv_ref[0]), cwv_ref, cbv_ref)[HALO:, :]
    a_ref[...] = ((yg * jax.nn.sigmoid(yg)) * yv).astype(a_ref.dtype)


def _ffn_down_kernel(a_ref, w_ref, x_ref, mod_ref, fg_ref, o_ref, *, final_norm):
    n = o_ref.shape[1]
    sumsq = jnp.zeros((o_ref.shape[0], 1), F32)
    for c0 in range(0, n, COL_CHUNK):
        cols = slice(c0, c0 + COL_CHUNK)
        y = x_ref[:, cols] + mod_ref[0, 5:6, cols] * _dot(a_ref[...], w_ref[0, :, cols])
        o_ref[:, cols] = y
        if final_norm:
            sumsq = sumsq + jnp.sum(y * y, axis=1, keepdims=True)
    if final_norm:
        r = lax.rsqrt(sumsq * (1.0 / n) + EPS)
        for c0 in range(0, n, COL_CHUNK):
            cols = slice(c0, c0 + COL_CHUNK)
            o_ref[:, cols] = (o_ref[:, cols] * r) * fg_ref[:, cols]


def _ffn(x2d, mod, g, w_up, conv_w, conv_b, w_down, final_g, seq, layer, final_norm):
    t, d = x2d.shape
    dff = w_down.shape[1]
    tm, tf = FFN_TILE, FF_CHUNK
    tpb = seq // tm
    nc = dff // tf
    hb = tm // HALO
    act = pl.pallas_call(
        functools.partial(_ffn_up_kernel, tiles_per_batch=tpb),
        grid=(t // tm, nc),
        in_specs=[
            pl.BlockSpec((tm, d), lambda i, c: (i, 0)),
            pl.BlockSpec((HALO, d), lambda i, c: (jnp.maximum(i * hb - 1, 0), 0)),
            pl.BlockSpec((1, 6, d), lambda i, c: (i // tpb, 0, 0)),
            pl.BlockSpec((1, d), lambda i, c: (0, 0)),
            pl.BlockSpec((1, d, tf), lambda i, c: (layer, 0, c)),
            pl.BlockSpec((1, d, tf), lambda i, c: (layer, 0, nc + c)),
            pl.BlockSpec((CONV_WIDTH, tf), lambda i, c: (0, c)),
            pl.BlockSpec((CONV_WIDTH, tf), lambda i, c: (0, nc + c)),
            pl.BlockSpec((1, tf), lambda i, c: (0, c)),
            pl.BlockSpec((1, tf), lambda i, c: (0, nc + c)),
        ],
        out_specs=pl.BlockSpec((tm, tf), lambda i, c: (i, c)),
        out_shape=jax.ShapeDtypeStruct((t, dff), BF16),
        scratch_shapes=[pltpu.VMEM((HALO + tm, d), BF16)],
        compiler_params=_cparams(("arbitrary", "arbitrary")),
        name="ffn_up",
    )(x2d, x2d, mod, g, w_up, w_up, conv_w, conv_w, conv_b, conv_b)

    tr = ROW_TILE
    rpb = seq // tr
    return pl.pallas_call(
        functools.partial(_ffn_down_kernel, final_norm=final_norm),
        grid=(t // tr,),
        in_specs=[
            pl.BlockSpec((tr, dff), lambda i: (i, 0)),
            pl.BlockSpec((1, dff, d), lambda i: (layer, 0, 0), pipeline_mode=pl.Buffered(1)),
            pl.BlockSpec((tr, d), lambda i: (i, 0)),
            pl.BlockSpec((1, 6, d), lambda i: (i // rpb, 0, 0)),
            pl.BlockSpec((1, d), lambda i: (0, 0)),
        ],
        out_specs=pl.BlockSpec((tr, d), lambda i: (i, 0)),
        out_shape=jax.ShapeDtypeStruct((t, d), F32),
        compiler_params=_cparams(("arbitrary",)),
        name="ffn_down",
    )(act, w_down, x2d, mod, final_g)


def _rope_tables(seq):
    inv = ROPE_THETA ** (-jnp.arange(0, HEAD_DIM, 2, dtype=F32) / HEAD_DIM)
    ang = jnp.arange(seq, dtype=F32)[:, None] * inv[None, :]
    cos, sin = jnp.cos(ang), jnp.sin(ang)
    return jnp.concatenate([cos, cos], axis=-1), jnp.concatenate([-sin, sin], axis=-1)


def _select_mask_table(seq):
    key_blk = (np.arange(seq) // SEL_BLOCK).reshape(seq, 1)
    return jnp.asarray(np.where(np.arange(LANES).reshape(1, LANES) == key_blk, NEG_INF, 0.0), dtype=BF16)


def _split_w_in(w):
    w = w.astype(BF16)
    fox = 3 * FOX_HEADS * HEAD_DIM
    nsa0 = fox + FOX_HEADS
    nsa1 = w.shape[-1] - 3 * NSA_HEADS
    pad = jnp.zeros((*w.shape[:-1], LANES - FOX_HEADS - 3 * NSA_HEADS), w.dtype)
    return w, jnp.concatenate([w[..., nsa0:nsa1], w[..., fox:nsa0], w[..., nsa1:], pad], axis=-1)


def kernel(x, c, attn_norm_g, ffn_norm_g, w_ada, b_ada, w_in, b_fgate, cmp_pos_k, cmp_pos_v,
           w_cmp1_k, w_cmp2_k, w_cmp1_v, w_cmp2_v, w_out, w_up, conv_w, conv_b, w_down, final_norm_g):
    batch, seq, d = x.shape
    depth = w_ada.shape[0]
    assert seq % ROW_TILE == 0 and seq % FFN_TILE == 0 and seq % FOX_TQ == 0 and seq % NSA_TK == 0 and seq // SEL_BLOCK == LANES // 4
    assert seq >= WIN_KEYS and w_down.shape[1] % FF_CHUNK == 0 and d % COL_CHUNK == 0

    cosf, sinf = _rope_tables(seq)
    emask = _select_mask_table(seq)
    tri = jnp.asarray(np.tril(np.ones((LANES, LANES))), dtype=BF16)
    mod_all = _ada_mod(c, w_ada, b_ada)

    x2d = x.reshape(batch * seq, d)
    n_fox = FOX_HEADS * HEAD_DIM
    w_up_b, w_down_b = w_up.astype(BF16), w_down.astype(BF16)
    w_fox_b, w_nsa_b = _split_w_in(w_in)
    for l in range(depth):
        mod = mod_all[l].reshape(batch, 6, d)
        zq, zc, zs = _in_proj(x2d, mod, attn_norm_g[l].reshape(1, d), w_fox_b, w_nsa_b, cosf, sinf, seq, l)
        bias_row = jnp.zeros((1, LANES), F32).at[0, :FOX_HEADS].set(b_fgate[l])
        cumc = _fgate(zs, bias_row, tri, batch, seq)
        o_fox = _fox_attention(zq, cumc, batch, seq)
        o_nsa = _nsa_attention(zq, zc, zs, cmp_pos_k[l], cmp_pos_v[l],
                               w_cmp1_k[l].astype(BF16), w_cmp1_v[l].astype(BF16),
                               w_cmp2_k[l].astype(BF16), w_cmp2_v[l].astype(BF16), emask, batch, seq)
        w_o = w_out[l].astype(BF16)
        x2d = _out_proj(o_fox, o_nsa, w_o[:n_fox], w_o[n_fox:], x2d, mod, seq)
        x2d = _ffn(x2d, mod, ffn_norm_g[l].reshape(1, d), w_up_b, conv_w[l], conv_b[l].reshape(1, -1), w_down_b,
                   final_norm_g.reshape(1, d), seq, layer=l, final_norm=(l == depth - 1))
    return x2d.reshape(batch, seq, d)
```

```python
import functools
import math

import numpy as np
import jax
import jax.numpy as jnp
from jax import lax
from jax.experimental import pallas as pl
from jax.experimental.pallas import tpu as pltpu

HEAD_DIM = 128
FOX_HEADS = 8
NSA_HEADS = 8
NSA_KV_HEADS = 2
NSA_GROUP = NSA_HEADS // NSA_KV_HEADS
CMP_BLOCK = 32
CMP_STRIDE = 16
SEL_BLOCK = 64
SEL_TOPK = 16
WINDOW = 512
CONV_WIDTH = 3
ROPE_THETA = 10000.0
FORCE_BONUS = 1000.0
NEG_INF = -1e30
TINY = 1e-30
EPS = 1e-6
LOG2E = math.log2(math.e)

LANES = 128
SUBLANES = 8
VMEM_LIMIT_BYTES = 56 * 1024 * 1024

ROW_TILE = 512
FFN_TILE = 1024
COL_CHUNK = 512
NORM_SUBROWS = 16
NORM_UNROLL = 8
FOX_TQ = 512
FOX_GROUP = 2
NSA_TQ = 128
NSA_SEL_TQ = 1024
NSA_KV_PER_STEP = 2
NSA_TK = 512
WIN_KEYS = WINDOW + NSA_TQ
HALO = 2 * SUBLANES
FF_CHUNK = 512

BF16 = jnp.bfloat16
F32 = jnp.float32

ZQ_FQ, ZQ_FK, ZQ_FV, ZQ_NQ, ZQ_KC, ZQ_VC, ZQ_KS, ZQ_VS, ZQ_KW, ZQ_VW = 0, 8, 16, 24, 32, 34, 36, 38, 40, 42
ZQ_HEADS = 44
ZQ_ROPED = (*range(ZQ_NQ, ZQ_NQ + NSA_HEADS), *(first + u for first in (ZQ_KC, ZQ_KS, ZQ_KW)
                                                 for u in range(NSA_KV_HEADS)))
ZC_HEADS = 4
ZS_GATE0 = FOX_HEADS


def _cparams(sem):
    return pltpu.CompilerParams(dimension_semantics=sem, vmem_limit_bytes=VMEM_LIMIT_BYTES)


def _dot(a, b):
    return jnp.dot(a, b, preferred_element_type=F32)


def _dot_nt(a, b):
    return lax.dot_general(a, b, (((1,), (1,)), ((), ())), preferred_element_type=F32)


def _ada_kernel(c_ref, w_ref, b_ref, o_ref):
    c = c_ref[...]
    ca = (c * jax.nn.sigmoid(c)).astype(BF16)
    o_ref[0] = _dot(ca, w_ref[0].astype(BF16)) + b_ref[0]


def _ada_mod(c, w_ada, b_ada):
    depth, d, n = w_ada.shape
    b = c.shape[0]
    tn = 1024
    return pl.pallas_call(
        _ada_kernel,
        grid=(depth, n // tn),
        in_specs=[
            pl.BlockSpec((b, d), lambda l, j: (0, 0)),
            pl.BlockSpec((1, d, tn), lambda l, j: (l, 0, j)),
            pl.BlockSpec((1, 1, tn), lambda l, j: (l, 0, j)),
        ],
        out_specs=pl.BlockSpec((1, b, tn), lambda l, j: (l, 0, j)),
        out_shape=jax.ShapeDtypeStruct((depth, b, n), F32),
        compiler_params=_cparams(("arbitrary", "arbitrary")),
        name="ada_mod",
    )(c, w_ada, b_ada.reshape(depth, 1, n))


def _norm_mod_rows(x, gain, shift):
    ms = jnp.mean(x * x, axis=-1, keepdims=True)
    return (x * lax.rsqrt(ms + EPS)) * gain + shift


def _norm_mod_to_scratch(x_ref, h_scr, row0, n_rows, gain, shift):
    nr = NORM_SUBROWS
    def body(r, _):
        rows = pl.ds(pl.multiple_of(r * nr, nr), nr)
        dst = pl.ds(pl.multiple_of(row0 + r * nr, nr), nr)
        h_scr[dst, :] = _norm_mod_rows(x_ref[rows, :], gain, shift).astype(BF16)
        return 0
    lax.fori_loop(0, n_rows // nr, body, 0, unroll=NORM_UNROLL)


def _rope_tile(t, cosf, sinf):
    return t * cosf + pltpu.roll(t, HEAD_DIM // 2, axis=1) * sinf


def _in_proj_kernel(x_ref, mod_ref, g_ref, wfox_ref, wnsa_ref, cos_ref, sin_ref, zq_ref, zc_ref, zs_ref, h_scr):
    tm = x_ref.shape[0]
    _norm_mod_to_scratch(x_ref, h_scr, 0, tm, g_ref[...] * (1.0 + mod_ref[0, 1:2, :]), mod_ref[0, 0:1, :])
    heads_per_chunk = COL_CHUNK // HEAD_DIM
    for c0 in range(0, ZQ_HEADS, heads_per_chunk):
        w_ref, first = (wfox_ref, c0) if c0 < ZQ_NQ else (wnsa_ref, c0 - ZQ_NQ)
        acc = _dot(h_scr[...], w_ref[0, :, first * HEAD_DIM:(first + heads_per_chunk) * HEAD_DIM])
        for hh in range(heads_per_chunk):
            head = c0 + hh
            t = acc[:, hh * HEAD_DIM:(hh + 1) * HEAD_DIM]
            if head in ZQ_ROPED:
                t = _rope_tile(t, cos_ref[...], sin_ref[...])
            zq_ref[:, head * HEAD_DIM:(head + 1) * HEAD_DIM] = t.astype(BF16)
            if ZQ_KC <= head < ZQ_KC + ZC_HEADS:
                zc_ref[:, (head - ZQ_KC) * HEAD_DIM:(head - ZQ_KC + 1) * HEAD_DIM] = t
    base = (ZQ_HEADS - ZQ_NQ) * HEAD_DIM
    zs_ref[...] = _dot(h_scr[...], wnsa_ref[0, :, base:base + LANES])


def _in_proj(x2d, mod, g, w_fox, w_nsa, cosf, sinf, seq, layer):
    t, d = x2d.shape
    tm = ROW_TILE
    tpb = seq // tm
    n_fox = ZQ_NQ * HEAD_DIM
    return pl.pallas_call(
        _in_proj_kernel,
        grid=(t // tm,),
        in_specs=[
            pl.BlockSpec((tm, d), lambda i: (i, 0)),
            pl.BlockSpec((1, 6, d), lambda i: (i // tpb, 0, 0)),
            pl.BlockSpec((1, d), lambda i: (0, 0)),
            pl.BlockSpec((1, d, n_fox), lambda i: (layer, 0, 0), pipeline_mode=pl.Buffered(1)),
            pl.BlockSpec((1, d, w_nsa.shape[2]), lambda i: (layer, 0, 0), pipeline_mode=pl.Buffered(1)),
            pl.BlockSpec((tm, HEAD_DIM), lambda i: (i % tpb, 0)),
            pl.BlockSpec((tm, HEAD_DIM), lambda i: (i % tpb, 0)),
        ],
        out_specs=[
            pl.BlockSpec((tm, ZQ_HEADS * HEAD_DIM), lambda i: (i, 0)),
            pl.BlockSpec((tm, ZC_HEADS * HEAD_DIM), lambda i: (i, 0)),
            pl.BlockSpec((tm, LANES), lambda i: (i, 0)),
        ],
        out_shape=[
            jax.ShapeDtypeStruct((t, ZQ_HEADS * HEAD_DIM), BF16),
            jax.ShapeDtypeStruct((t, ZC_HEADS * HEAD_DIM), F32),
            jax.ShapeDtypeStruct((t, LANES), F32),
        ],
        scratch_shapes=[pltpu.VMEM((tm, d), BF16)],
        compiler_params=_cparams(("arbitrary",)),
        name="in_proj",
    )(x2d, mod, g, w_fox, w_nsa, cosf, sinf)


def _split3(v):
    hi = v.astype(BF16)
    r1 = v - hi.astype(F32)
    mid = r1.astype(BF16)
    lo = (r1 - mid.astype(F32)).astype(BF16)
    return hi, mid, lo


def _fgate_kernel(zs_ref, bias_ref, tri_ref, cumc_ref):
    seq = zs_ref.shape[0]
    tri = tri_ref[...]
    carry = jnp.zeros((1, LANES), F32)
    for c in range(seq // LANES):
        zf = zs_ref[c * LANES:(c + 1) * LANES, :] + bias_ref[...]
        lf = jnp.minimum(zf, 0.0) - jnp.log1p(jnp.exp(-jnp.abs(zf)))
        hi, mid, lo = _split3(lf)
        cs = (_dot(tri, lo) + _dot(tri, mid)) + _dot(tri, hi) + carry
        carry = cs[LANES - 1:LANES, :]
        cumc_ref[c * LANES:(c + 1) * LANES, :] = cs


def _fgate(zs, bias_row, tri, batch, seq):
    t = zs.shape[0]
    return pl.pallas_call(
        _fgate_kernel,
        grid=(batch,),
        in_specs=[
            pl.BlockSpec((seq, LANES), lambda b: (b, 0)),
            pl.BlockSpec((1, LANES), lambda b: (0, 0)),
            pl.BlockSpec((LANES, LANES), lambda b: (0, 0)),
        ],
        out_specs=pl.BlockSpec((seq, LANES), lambda b: (b, 0)),
        out_shape=jax.ShapeDtypeStruct((t, LANES), F32),
        compiler_params=_cparams(("arbitrary",)),
        name="fgate_cumsum",
    )(zs, bias_row, tri)


def _lane_pick(tile, lane_idx):
    lane = lax.broadcasted_iota(jnp.int32, tile.shape, 1)
    return jnp.sum(jnp.where(lane == lane_idx, tile, 0.0), axis=1, keepdims=True)


def _online_step(carry, s2, v_ones):
    m, acc = carry
    m_new = jnp.maximum(m, jnp.max(s2, axis=1, keepdims=True))
    p = jnp.exp2(s2 - m_new)
    acc = jnp.exp2(m - m_new) * acc + _dot(p.astype(BF16), v_ones)
    return m_new, acc


def _online_init(rows):
    return jnp.full((rows, 1), NEG_INF, F32), jnp.zeros((rows, 2 * HEAD_DIM), F32)


def _online_finish(carry):
    _, acc = carry
    return acc[:, :HEAD_DIM] / acc[:, HEAD_DIM:HEAD_DIM + 1]


def _ones_lane0(rows):
    return jnp.where(lax.broadcasted_iota(jnp.int32, (rows, LANES), 1) == 0, 1.0, 0.0).astype(BF16)


def _bias_lanes(pieces, head):
    npc = len(pieces)

    def place(n_in):
        row = lax.broadcasted_iota(jnp.int32, (n_in * LANES, 2 * LANES), 0)
        col = lax.broadcasted_iota(jnp.int32, (n_in * LANES, 2 * LANES), 1)
        return row, col
    hi_mid = jnp.concatenate(pieces[:2], axis=1)
    row, col = place(2)
    pc = jnp.where(row >= LANES, 1, 0)
    hit = row - pc * LANES == head
    w_a = jnp.where(hit, jnp.where(col == pc, 1.0, jnp.where(col == LANES + npc + pc, -1.0, 0.0)), 0.0).astype(BF16)
    row, col = place(1)
    w_b = jnp.where(row == head,
                    jnp.where(col == npc - 1, 1.0, jnp.where(col == LANES + 2 * npc - 1, -1.0, 0.0)), 0.0).astype(BF16)
    lane = lax.broadcasted_iota(jnp.int32, (1, 2 * LANES), 1)
    in_range = lambda lo, hi: jnp.where(lane >= lo, jnp.where(lane < hi, 1.0, 0.0), 0.0)
    ones = in_range(npc, 2 * npc) + in_range(LANES, LANES + npc)
    return (_dot(hi_mid, w_a) + _dot(pieces[2], w_b) + ones).astype(BF16)


def _fox_kernel(q_ref, k_ref, v_ref, cum_ref, o_ref, qaug_scr, kaug_scr, vaug_scr):
    hb = pl.program_id(1)
    tile, grp = FOX_TQ, FOX_GROUP
    seq = q_ref.shape[0]
    scale = 1.0 / math.sqrt(HEAD_DIM)
    c2 = scale * LOG2E
    head = lambda h: slice(h * HEAD_DIM, (h + 1) * HEAD_DIM)

    pieces = _split3(cum_ref[...] * (1.0 / scale))
    ones = _ones_lane0(seq)
    for g in range(grp):
        bias = _bias_lanes(pieces, hb * grp + g)
        qaug_scr[:, head(2 * g)] = q_ref[:, head(g)]
        qaug_scr[:, head(2 * g + 1)] = bias[:, :LANES]
        kaug_scr[:, head(2 * g)] = k_ref[:, head(g)]
        kaug_scr[:, head(2 * g + 1)] = bias[:, LANES:]
        vaug_scr[:, head(2 * g)] = v_ref[:, head(g)]
        vaug_scr[:, head(2 * g + 1)] = ones

    causal = (lax.broadcasted_iota(jnp.int32, (tile, tile), 1) <= lax.broadcasted_iota(jnp.int32, (tile, tile), 0))
    for i in range(seq // tile):
        qrows = slice(i * tile, (i + 1) * tile)
        for g in range(grp):
            carry = _online_init(tile)
            for j in range(i + 1):
                krows = slice(j * tile, (j + 1) * tile)
                s2 = _dot_nt(qaug_scr[qrows, 2 * g * HEAD_DIM:(2 * g + 2) * HEAD_DIM],
                             kaug_scr[krows, 2 * g * HEAD_DIM:(2 * g + 2) * HEAD_DIM]) * c2
                if j == i:
                    s2 = jnp.where(causal, s2, NEG_INF)
                carry = _online_step(carry, s2, vaug_scr[krows, 2 * g * HEAD_DIM:(2 * g + 2) * HEAD_DIM])
            o_ref[qrows, head(g)] = _online_finish(carry).astype(o_ref.dtype)


def _fox_attention(zq, cumc, batch, seq):
    t = zq.shape[0]
    grp = FOX_GROUP
    gw = grp * HEAD_DIM
    return pl.pallas_call(
        _fox_kernel,
        grid=(batch, FOX_HEADS // grp),
        in_specs=[
            pl.BlockSpec((seq, gw), lambda b, h: (b, ZQ_FQ // grp + h)),
            pl.BlockSpec((seq, gw), lambda b, h: (b, ZQ_FK // grp + h)),
            pl.BlockSpec((seq, gw), lambda b, h: (b, ZQ_FV // grp + h)),
            pl.BlockSpec((seq, LANES), lambda b, h: (b, 0)),
        ],
        out_specs=pl.BlockSpec((seq, gw), lambda b, h: (b, h)),
        out_shape=jax.ShapeDtypeStruct((t, FOX_HEADS * HEAD_DIM), BF16),
        scratch_shapes=[pltpu.VMEM((seq, 2 * gw), BF16)] * 3,
        compiler_params=_cparams(("arbitrary", "arbitrary")),
        name="fox_attention",
    )(zq, zq, zq, cumc)


def _compress_to(src_ref, hk, pos_ref, w1_ref, w2_ref, flat_scr, dst_scr):
    half = CMP_BLOCK // 2
    n_rows = dst_scr.shape[0]
    cols = slice(hk * HEAD_DIM, (hk + 1) * HEAD_DIM)
    for jp in range(half):
        rows = src_ref[pl.ds(jp, n_rows, stride=CMP_STRIDE), :]
        flat_scr[:, jp * HEAD_DIM:(jp + 1) * HEAD_DIM] = (rows + pos_ref[jp:jp + 1, :]).astype(BF16)
        nxt = pltpu.roll(rows, n_rows - 1, axis=0)
        flat_scr[:, (half + jp) * HEAD_DIM:(half + jp + 1) * HEAD_DIM] = (
            nxt + pos_ref[half + jp:half + jp + 1, :]).astype(BF16)
    pre = _dot(flat_scr[...], w1_ref[...])
    dst_scr[:, cols] = _dot(jax.nn.gelu(pre).astype(BF16), w2_ref[...]).astype(BF16)


def _nsa_kernel(q_ref, ks_ref, kw_ref, vs_ref, vw_ref, *rest):
    nkv = NSA_KV_PER_STEP
    zs_ref, ocmp_ref, nots_ref, emask_ref, o_ref, ksaug_scr = rest
    hk0 = pl.program_id(1) * nkv
    i = pl.program_id(2)
    tq, tk = NSA_TQ, NSA_TK
    q0 = i * tq
    head = lambda h: slice(h * HEAD_DIM, (h + 1) * HEAD_DIM)

    @pl.when(i == 0)
    def _():
        for hk in range(nkv):
            ksaug_scr[:, head(2 * hk)] = ks_ref[:, head(hk)]
            ksaug_scr[:, head(2 * hk + 1)] = emask_ref[...]

    tile = functools.partial(_nsa_tile, q0=q0, hk0=hk0, q_ref=q_ref, kw_ref=kw_ref, vs_ref=vs_ref, vw_ref=vw_ref,
                             zs_ref=zs_ref, ocmp_ref=ocmp_ref, nots_ref=nots_ref, o_ref=o_ref, ksaug_scr=ksaug_scr)
    for n_full in range(ks_ref.shape[0] // tk):
        pl.when(q0 // tk == n_full)(functools.partial(tile, n_full))


def _nsa_tile(n_full, *, q0, hk0, q_ref, kw_ref, vs_ref, vw_ref, zs_ref, ocmp_ref, nots_ref, o_ref, ksaug_scr):
    grp, nkv = NSA_GROUP, NSA_KV_PER_STEP
    head = lambda h: slice(h * HEAD_DIM, (h + 1) * HEAD_DIM)

    def stack4(tile):
        return jnp.concatenate([tile] * grp, axis=0)

    gl = jax.nn.sigmoid(zs_ref[...])
    pre = []
    for hk in range(nkv):
        q4 = jnp.concatenate([q_ref[:, head(hk * grp + g)] for g in range(grp)], axis=0)
        o_cmp = jnp.concatenate([ocmp_ref[:, head(hk * grp + g)] for g in range(grp)], axis=0)
        pre.append((q4, jnp.concatenate([q4, stack4(nots_ref[:, head(hk)])], axis=1), o_cmp))
    _nsa_sweeps(n_full, pre, gl, q0=q0, hk0=hk0, kw_ref=kw_ref, vs_ref=vs_ref, vw_ref=vw_ref, o_ref=o_ref,
                ksaug_scr=ksaug_scr)


def _nsa_select_kernel(q_ref, *rest):
    nkv = NSA_KV_HEADS
    kcf_refs, vcf_refs = rest[:nkv], rest[nkv:2 * nkv]
    (posk_ref, posv_ref, w1k_ref, w1v_ref, w2k_ref, w2v_ref,
     ocmp_ref, nots_ref, kcmp_scr, vcmp_scr, flat_scr, tr_scr) = rest[2 * nkv:]
    i = pl.program_id(1)
    tq, grp = NSA_SEL_TQ, NSA_GROUP
    n_selb = SEL_BLOCK // CMP_STRIDE
    n_sel = LANES // n_selb
    c2 = LOG2E / math.sqrt(HEAD_DIM)
    q0 = i * tq
    head = lambda h: slice(h * HEAD_DIM, (h + 1) * HEAD_DIM)

    @pl.when(i == 0)
    def _():
        for hk in range(nkv):
            _compress_to(kcf_refs[hk], hk, posk_ref, w1k_ref, w2k_ref, flat_scr, kcmp_scr)
            _compress_to(vcf_refs[hk], hk, posv_ref, w1v_ref, w2v_ref, flat_scr, vcmp_scr)

    rloc = lax.broadcasted_iota(jnp.int32, (tq, LANES), 0)
    lane = lax.broadcasted_iota(jnp.int32, (tq, LANES), 1)
    cvalid = jnp.where((lane * CMP_STRIDE + (CMP_BLOCK - 1)) <= q0 + rloc, 1.0, 0.0)
    cvalid4 = jnp.concatenate([cvalid] * grp, axis=0) > 0.5
    blk = lax.broadcasted_iota(jnp.int32, (n_sel, tq), 0)
    cur = lax.shift_right_logical(q0 + lax.broadcasted_iota(jnp.int32, (n_sel, tq), 1), int(math.log2(SEL_BLOCK)))
    forced = jnp.where(blk == 0, 1.0, jnp.where(blk == cur, 1.0, jnp.where(blk == cur - 1, 1.0, 0.0)))

    for hk in range(nkv):
        q4 = jnp.concatenate([q_ref[:, head(hk * grp + g)] for g in range(grp)], axis=0)
        sc = jnp.where(cvalid4, _dot_nt(q4, kcmp_scr[:, head(hk)]) * c2, NEG_INF)
        pc = jnp.where(cvalid4, jnp.exp2(sc - jnp.max(sc, axis=1, keepdims=True)), 0.0)
        pc = pc / jnp.maximum(jnp.sum(pc, axis=1, keepdims=True), TINY)
        o_cmp = _dot(pc.astype(BF16), vcmp_scr[:, head(hk)])
        psum = (pc[0:tq] + pc[tq:2 * tq]) + (pc[2 * tq:3 * tq] + pc[3 * tq:4 * tq])
        nsub = tq // LANES
        for u in range(nsub):
            tr_scr[hk * nsub + u] = psum[u * LANES:(u + 1) * LANES, :].T
        pooled = lambda t: ((tr_scr[t, pl.ds(0, n_sel, stride=n_selb), :] + tr_scr[t, pl.ds(1, n_sel, stride=n_selb), :])
                            + (tr_scr[t, pl.ds(2, n_sel, stride=n_selb), :] + tr_scr[t, pl.ds(3, n_sel, stride=n_selb), :]))
        imp = jnp.concatenate([pooled(hk * nsub + u) for u in range(nsub)], axis=1)
        score = jnp.where(blk <= cur, imp + forced * FORCE_BONUS, NEG_INF)
        rank = jnp.zeros((n_sel, tq), F32)
        for kk in range(n_sel):
            row = score[kk:kk + 1, :]
            earlier = jnp.where(blk > kk, 1.0, 0.0)
            rank = rank + jnp.where(row > score, 1.0, jnp.where(row == score, earlier, 0.0))
        dropped = jnp.where(rank < float(SEL_TOPK), 0.0, 1.0)
        pieces = []
        for u in range(nsub):
            tr_scr[hk * nsub + u, 0:n_sel, :] = dropped[:, u * LANES:(u + 1) * LANES]
            tr_scr[hk * nsub + u, n_sel:, :] = jnp.zeros((LANES - n_sel, LANES), F32)
            pieces.append(tr_scr[hk * nsub + u].T)
        nots_ref[:, head(hk)] = jnp.concatenate(pieces, axis=0).astype(BF16)
        for g in range(grp):
            ocmp_ref[:, head(hk * grp + g)] = o_cmp[g * tq:(g + 1) * tq, :]


def _nsa_sweeps(n_full, pre, gl, *, q0, hk0, kw_ref, vs_ref, vw_ref, o_ref, ksaug_scr):
    tq, tk, grp, nkv = NSA_TQ, NSA_TK, NSA_GROUP, NSA_KV_PER_STEP
    rows4 = grp * tq
    c2 = LOG2E / math.sqrt(HEAD_DIM)
    head = lambda h: slice(h * HEAD_DIM, (h + 1) * HEAD_DIM)

    def stack4(tile):
        return jnp.concatenate([tile] * grp, axis=0)

    def sel_step(j, carries, bias4):
        rows = slice(j * tk, (j + 1) * tk)
        out = []
        for hk in range(nkv):
            s2 = _dot_nt(pre[hk][1], ksaug_scr[rows, 2 * hk * HEAD_DIM:(2 * hk + 2) * HEAD_DIM]) * c2
            if bias4 is not None:
                s2 = s2 + bias4
            out.append(_online_step(carries[hk], s2, jnp.concatenate([vs_ref[rows, head(hk)], ones_tk], axis=1)))
        return tuple(out)

    ones_tk = _ones_lane0(tk)
    carries = tuple(_online_init(rows4) for _ in range(nkv))
    for j in range(n_full):
        carries = sel_step(j, carries, None)
    rk = lax.broadcasted_iota(jnp.int32, (tq, tk), 0)
    ck = lax.broadcasted_iota(jnp.int32, (tq, tk), 1)
    causal4 = stack4(jnp.where(n_full * tk + ck <= q0 + rk, 0.0, NEG_INF))
    carries = sel_step(n_full, carries, causal4)

    w0 = pl.multiple_of(jnp.maximum(q0 - WINDOW, 0), tq)
    rw = lax.broadcasted_iota(jnp.int32, (tq, WIN_KEYS), 0)
    cw = lax.broadcasted_iota(jnp.int32, (tq, WIN_KEYS), 1)
    dist = (q0 - w0) + rw - cw
    wbias4 = stack4(jnp.where(jnp.abs(2 * dist - (WINDOW - 1)) <= (WINDOW - 1), 0.0, NEG_INF))
    wrows = pl.ds(w0, WIN_KEYS)

    for hk in range(nkv):
        q4, _, o_cmp = pre[hk]
        o_sel = _online_finish(carries[hk])
        s2 = _dot_nt(q4, kw_ref[wrows, head(hk)]) * c2 + wbias4
        vw_ones = jnp.concatenate([vw_ref[wrows, head(hk)], _ones_lane0(WIN_KEYS)], axis=1)
        o_win = _online_finish(_online_step(_online_init(rows4), s2, vw_ones))
        gate = lambda c: jnp.concatenate(
            [_lane_pick(gl, ZS_GATE0 + 3 * ((hk0 + hk) * grp + g) + c) for g in range(grp)], axis=0)
        o4 = gate(0) * o_cmp + gate(1) * o_sel + gate(2) * o_win
        for g in range(grp):
            o_ref[:, head(hk * grp + g)] = o4[g * tq:(g + 1) * tq, :].astype(o_ref.dtype)


def _nsa_attention(zq, zc, zs, posk, posv, w1k, w1v, w2k, w2v, emask, batch, seq):
    t = zq.shape[0]
    all_q = NSA_HEADS * HEAD_DIM
    all_kv = NSA_KV_HEADS * HEAD_DIM

    ts = NSA_SEL_TQ
    ns = seq // ts
    const2 = lambda shape: pl.BlockSpec(shape, lambda b, i: (0,) * len(shape))
    o_cmp, notsel = pl.pallas_call(
        _nsa_select_kernel,
        grid=(batch, ns),
        in_specs=[
            pl.BlockSpec((ts, all_q), lambda b, i: (b * ns + i, ZQ_NQ * HEAD_DIM // all_q)),
            *[pl.BlockSpec((seq, HEAD_DIM), functools.partial(lambda b, i, h: (b, h), h=h)) for h in range(ZC_HEADS)],
            const2((CMP_BLOCK, HEAD_DIM)), const2((CMP_BLOCK, HEAD_DIM)),
            const2((CMP_BLOCK * HEAD_DIM, HEAD_DIM)), const2((CMP_BLOCK * HEAD_DIM, HEAD_DIM)),
            const2((HEAD_DIM, HEAD_DIM)), const2((HEAD_DIM, HEAD_DIM)),
        ],
        out_specs=[
            pl.BlockSpec((ts, all_q), lambda b, i: (b * ns + i, 0)),
            pl.BlockSpec((ts, all_kv), lambda b, i: (b * ns + i, 0)),
        ],
        out_shape=[jax.ShapeDtypeStruct((t, all_q), F32), jax.ShapeDtypeStruct((t, all_kv), BF16)],
        scratch_shapes=[
            pltpu.VMEM((LANES, all_kv), BF16),
            pltpu.VMEM((LANES, all_kv), BF16),
            pltpu.VMEM((LANES, CMP_BLOCK * HEAD_DIM), BF16),
            pltpu.VMEM((NSA_KV_HEADS * (ts // LANES), LANES, LANES), F32),
        ],
        compiler_params=_cparams(("arbitrary", "arbitrary")),
        name="nsa_select",
    )(zq, zc, zc, zc, zc, posk, posv, w1k, w1v, w2k, w2v)

    tq, nkv = NSA_TQ, NSA_KV_PER_STEP
    nq = seq // tq
    qw = nkv * NSA_GROUP * HEAD_DIM
    kvw = nkv * HEAD_DIM
    kv = lambda col: pl.BlockSpec((seq, kvw), lambda b, h, i: (b, col // nkv + h))
    return pl.pallas_call(
        _nsa_kernel,
        grid=(batch, NSA_KV_HEADS // nkv, nq),
        in_specs=[
            pl.BlockSpec((tq, qw), lambda b, h, i: (b * nq + i, ZQ_NQ * HEAD_DIM // qw + h)),
            kv(ZQ_KS), kv(ZQ_KW), kv(ZQ_VS), kv(ZQ_VW),
            pl.BlockSpec((tq, LANES), lambda b, h, i: (b * nq + i, 0)),
            pl.BlockSpec((tq, qw), lambda b, h, i: (b * nq + i, h)),
            pl.BlockSpec((tq, kvw), lambda b, h, i: (b * nq + i, h)),
            pl.BlockSpec((seq, LANES), lambda b, h, i: (0, 0)),
        ],
        out_specs=pl.BlockSpec((tq, qw), lambda b, h, i: (b * nq + i, h)),
        out_shape=jax.ShapeDtypeStruct((t, all_q), BF16),
        scratch_shapes=[pltpu.VMEM((seq, 2 * kvw), BF16)],
        compiler_params=_cparams(("arbitrary", "arbitrary", "arbitrary")),
        name="nsa_attention",
    )(zq, zq, zq, zq, zq, zs, o_cmp, notsel, emask)


def _out_proj_kernel(of_ref, on_ref, wf_ref, wn_ref, x_ref, mod_ref, o_ref):
    n = o_ref.shape[1]
    for c0 in range(0, n, COL_CHUNK):
        cols = slice(c0, c0 + COL_CHUNK)
        acc = _dot(of_ref[...], wf_ref[:, cols]) + _dot(on_ref[...], wn_ref[:, cols])
        o_ref[:, cols] = x_ref[:, cols] + mod_ref[0, 2:3, cols] * acc


def _out_proj(o_fox, o_nsa, w_fox, w_nsa, x2d, mod, seq):
    t, d = x2d.shape
    tm = ROW_TILE
    tpb = seq // tm
    kf, kn = o_fox.shape[1], o_nsa.shape[1]
    return pl.pallas_call(
        _out_proj_kernel,
        grid=(t // tm,),
        in_specs=[
            pl.BlockSpec((tm, kf), lambda i: (i, 0)),
            pl.BlockSpec((tm, kn), lambda i: (i, 0)),
            pl.BlockSpec((kf, d), lambda i: (0, 0), pipeline_mode=pl.Buffered(1)),
            pl.BlockSpec((kn, d), lambda i: (0, 0), pipeline_mode=pl.Buffered(1)),
            pl.BlockSpec((tm, d), lambda i: (i, 0)),
            pl.BlockSpec((1, 6, d), lambda i: (i // tpb, 0, 0)),
        ],
        out_specs=pl.BlockSpec((tm, d), lambda i: (i, 0)),
        out_shape=jax.ShapeDtypeStruct((t, d), F32),
        compiler_params=_cparams(("arbitrary",)),
        name="out_proj",
    )(o_fox, o_nsa, w_fox, w_nsa, x2d, mod)


def _conv_rows(u, cw_ref, cb_ref):
    y = cb_ref[...] + cw_ref[0:1, :] * pltpu.roll(u, 2, axis=0)
    y = y + cw_ref[1:2, :] * pltpu.roll(u, 1, axis=0)
    return y + cw_ref[2:3, :] * u


def _ffn_up_kernel(x_ref, xh_ref, mod_ref, g_ref, wg_ref, wv_ref, cwg_ref, cwv_ref, cbg_ref, cbv_ref,
                   a_ref, h_scr, *, tiles_per_batch):
    i = pl.program_id(0)
    c = pl.program_id(1)
    tm = x_ref.shape[0]

    @pl.when(c == 0)
    def _():
        gain, shift = g_ref[...] * (1.0 + mod_ref[0, 4:5, :]), mod_ref[0, 3:4, :]
        halo = _norm_mod_rows(xh_ref[...], gain, shift)
        first = (i % tiles_per_batch) == 0
        h_scr[0:HALO, :] = jnp.where(first, 0.0, halo).astype(BF16)
        _norm_mod_to_scratch(x_ref, h_scr, HALO, tm, gain, shift)

    h = h_scr[...]
    yg = _conv_rows(_dot(h, wg_ref[0]), cwg_ref, cbg_ref)[HALO:, :]
    yv = _conv_rows(_dot(h, wv_ref[0]), cwv_ref, cbv_ref)[HALO:, :]
    half = 0.5 * yg
    a_ref[...] = ((half + half * jnp.tanh(half)) * yv).astype(a_ref.dtype)


def _ffn_down_kernel(a_ref, w_ref, x_ref, mod_ref, fg_ref, o_ref, *, final_norm):
    n = o_ref.shape[1]
    sumsq = jnp.zeros((o_ref.shape[0], 1), F32)
    for c0 in range(0, n, COL_CHUNK):
        cols = slice(c0, c0 + COL_CHUNK)
        y = x_ref[:, cols] + mod_ref[0, 5:6, cols] * _dot(a_ref[...], w_ref[0, :, cols])
        o_ref[:, cols] = y
        if final_norm:
            sumsq = sumsq + jnp.sum(y * y, axis=1, keepdims=True)
    if final_norm:
        r = lax.rsqrt(sumsq * (1.0 / n) + EPS)
        for c0 in range(0, n, COL_CHUNK):
            cols = slice(c0, c0 + COL_CHUNK)
            o_ref[:, cols] = (o_ref[:, cols] * r) * fg_ref[:, cols]


def _ffn(x2d, mod, g, w_up, conv_w, conv_b, w_down, final_g, seq, layer, final_norm):
    t, d = x2d.shape
    dff = w_down.shape[1]
    tm, tf = FFN_TILE, FF_CHUNK
    tpb = seq // tm
    nc = dff // tf
    hb = tm // HALO
    act = pl.pallas_call(
        functools.partial(_ffn_up_kernel, tiles_per_batch=tpb),
        grid=(t // tm, nc),
        in_specs=[
            pl.BlockSpec((tm, d), lambda i, c: (i, 0)),
            pl.BlockSpec((HALO, d), lambda i, c: (jnp.maximum(i * hb - 1, 0), 0)),
            pl.BlockSpec((1, 6, d), lambda i, c: (i // tpb, 0, 0)),
            pl.BlockSpec((1, d), lambda i, c: (0, 0)),
            pl.BlockSpec((1, d, tf), lambda i, c: (layer, 0, c)),
            pl.BlockSpec((1, d, tf), lambda i, c: (layer, 0, nc + c)),
            pl.BlockSpec((CONV_WIDTH, tf), lambda i, c: (0, c)),
            pl.BlockSpec((CONV_WIDTH, tf), lambda i, c: (0, nc + c)),
            pl.BlockSpec((1, tf), lambda i, c: (0, c)),
            pl.BlockSpec((1, tf), lambda i, c: (0, nc + c)),
        ],
        out_specs=pl.BlockSpec((tm, tf), lambda i, c: (i, c)),
        out_shape=jax.ShapeDtypeStruct((t, dff), BF16),
        scratch_shapes=[pltpu.VMEM((HALO + tm, d), BF16)],
        compiler_params=_cparams(("arbitrary", "arbitrary")),
        name="ffn_up",
    )(x2d, x2d, mod, g, w_up, w_up, conv_w, conv_w, conv_b, conv_b)

    tr = ROW_TILE
    rpb = seq // tr
    return pl.pallas_call(
        functools.partial(_ffn_down_kernel, final_norm=final_norm),
        grid=(t // tr,),
        in_specs=[
            pl.BlockSpec((tr, dff), lambda i: (i, 0)),
            pl.BlockSpec((1, dff, d), lambda i: (layer, 0, 0), pipeline_mode=pl.Buffered(1)),
            pl.BlockSpec((tr, d), lambda i: (i, 0)),
            pl.BlockSpec((1, 6, d), lambda i: (i // rpb, 0, 0)),
            pl.BlockSpec((1, d), lambda i: (0, 0)),
        ],
        out_specs=pl.BlockSpec((tr, d), lambda i: (i, 0)),
        out_shape=jax.ShapeDtypeStruct((t, d), F32),
        compiler_params=_cparams(("arbitrary",)),
        name="ffn_down",
    )(act, w_down, x2d, mod, final_g)


def _rope_tables(seq):
    inv = ROPE_THETA ** (-jnp.arange(0, HEAD_DIM, 2, dtype=F32) / HEAD_DIM)
    ang = jnp.arange(seq, dtype=F32)[:, None] * inv[None, :]
    cos, sin = jnp.cos(ang), jnp.sin(ang)
    return jnp.concatenate([cos, cos], axis=-1), jnp.concatenate([-sin, sin], axis=-1)


def _select_mask_table(seq):
    key_blk = (np.arange(seq) // SEL_BLOCK).reshape(seq, 1)
    return jnp.asarray(np.where(np.arange(LANES).reshape(1, LANES) == key_blk, NEG_INF, 0.0), dtype=BF16)


def _split_w_in(w):
    w = w.astype(BF16)
    fox = 3 * FOX_HEADS * HEAD_DIM
    nsa0 = fox + FOX_HEADS
    nsa1 = w.shape[-1] - 3 * NSA_HEADS
    pad = jnp.zeros((*w.shape[:-1], LANES - FOX_HEADS - 3 * NSA_HEADS), w.dtype)
    return w, jnp.concatenate([w[..., nsa0:nsa1], w[..., fox:nsa0], w[..., nsa1:], pad], axis=-1)


def kernel(x, c, attn_norm_g, ffn_norm_g, w_ada, b_ada, w_in, b_fgate, cmp_pos_k, cmp_pos_v,
           w_cmp1_k, w_cmp2_k, w_cmp1_v, w_cmp2_v, w_out, w_up, conv_w, conv_b, w_down, final_norm_g):
    batch, seq, d = x.shape
    depth = w_ada.shape[0]
    assert seq % ROW_TILE == 0 and seq % FFN_TILE == 0 and seq % FOX_TQ == 0 and seq % NSA_TK == 0 and seq // SEL_BLOCK == LANES // 4
    assert seq >= WIN_KEYS and w_down.shape[1] % FF_CHUNK == 0 and d % COL_CHUNK == 0

    cosf, sinf = _rope_tables(seq)
    emask = _select_mask_table(seq)
    tri = jnp.asarray(np.tril(np.ones((LANES, LANES))), dtype=BF16)
    mod_all = _ada_mod(c, w_ada, b_ada)

    x2d = x.reshape(batch * seq, d)
    n_fox = FOX_HEADS * HEAD_DIM
    w_up_b, w_down_b = w_up.astype(BF16), w_down.astype(BF16)
    w_fox_b, w_nsa_b = _split_w_in(w_in)
    for l in range(depth):
        mod = mod_all[l].reshape(batch, 6, d)
        zq, zc, zs = _in_proj(x2d, mod, attn_norm_g[l].reshape(1, d), w_fox_b, w_nsa_b, cosf, sinf, seq, l)
        bias_row = jnp.zeros((1, LANES), F32).at[0, :FOX_HEADS].set(b_fgate[l])
        cumc = _fgate(zs, bias_row, tri, batch, seq)
        o_fox = _fox_attention(zq, cumc, batch, seq)
        o_nsa = _nsa_attention(zq, zc, zs, cmp_pos_k[l], cmp_pos_v[l],
                               w_cmp1_k[l].astype(BF16), w_cmp1_v[l].astype(BF16),
                               w_cmp2_k[l].astype(BF16), w_cmp2_v[l].astype(BF16), emask, batch, seq)
        w_o = w_out[l].astype(BF16)
        x2d = _out_proj(o_fox, o_nsa, w_o[:n_fox], w_o[n_fox:], x2d, mod, seq)
        x2d = _ffn(x2d, mod, ffn_norm_g[l].reshape(1, d), w_up_b, conv_w[l], conv_b[l].reshape(1, -1), w_down_b,
                   final_norm_g.reshape(1, d), seq, layer=l, final_norm=(l == depth - 1))
    return x2d.reshape(batch, seq, d)
```
